```python
import math
import jax, jax.numpy as jnp
from jax import lax
import numpy as np

D_MODEL = 2048
BATCH = 4
SEQ = 2048
DEPTH = 2
DEC_BATCH = 128
DEC_SEQ = 4
PAST_LEN = 8192
PAGE_SIZE = 128

A_HEADS = 16
A_HEAD_DIM = 64
A_WIDTH = A_HEADS * A_HEAD_DIM
DECAY_LORA = 64
ICLR_LORA = 64
GATE_LORA = 160
A_PROJ = 3 * A_WIDTH + DECAY_LORA + ICLR_LORA + GATE_LORA
GN_EPS = 64e-5
Q_HEADS = 16
KV_HEADS = 4
GROUP = Q_HEADS // KV_HEADS
HEAD_DIM = 64
Q_W = Q_HEADS * HEAD_DIM
KV_W = KV_HEADS * HEAD_DIM
WINDOW = 128
ROPE_DIM = HEAD_DIM // 4
ROPE_THETA = 500000.0
B_PROJ = Q_W + 2 * KV_W
IN_PROJ = A_PROJ + B_PROJ
D_FF = 5632
N_MOD = 9
RMS_EPS = 1e-6

kernel_name = 'hybrid_rwkv7_swa_sink_macaron_adaln_step'


def rms_norm(x, gain):
    xf = x.astype(jnp.float32)
    y = xf * lax.rsqrt(jnp.mean(xf * xf, axis=-1, keepdims=True) + RMS_EPS)
    return (y * gain.astype(jnp.float32)).astype(x.dtype)


def ada_norm(x, gain, shift, scale):
    return rms_norm(x, gain) * (1 + scale) + shift


def swiglu(h, w_gate, w_up, w_down):
    return (jax.nn.silu(h @ w_gate) * (h @ w_up)) @ w_down


def rope_partial(x, pos):
    half = ROPE_DIM // 2
    inv = 1.0 / (ROPE_THETA ** (jnp.arange(0, ROPE_DIM, 2, dtype=jnp.float32) / ROPE_DIM))
    ang = pos.astype(jnp.float32)[:, None] * inv[None, :]
    cos = jnp.cos(ang)[:, None, :]
    sin = jnp.sin(ang)[:, None, :]
    xr = x[..., :ROPE_DIM].astype(jnp.float32)
    x1, x2 = xr[..., :half], xr[..., half:]
    rot = jnp.concatenate([x1 * cos - x2 * sin, x2 * cos + x1 * sin], axis=-1).astype(x.dtype)
    return jnp.concatenate([rot, x[..., ROPE_DIM:]], axis=-1)


def sink_attention(q, k, v, mask, sinks):
    s = jnp.einsum('...qkgd,...skd->...kgqs', q, k).astype(jnp.float32) * (HEAD_DIM ** -0.5)
    s = jnp.where(mask, s, -jnp.inf)
    sink = jnp.broadcast_to(sinks.astype(jnp.float32).reshape(KV_HEADS, GROUP, 1, 1), s.shape[:-1] + (1,))
    p = jax.nn.softmax(jnp.concatenate([s, sink], axis=-1), axis=-1)[..., :-1]
    return jnp.einsum('...kgqs,...skd->...qkgd', p.astype(v.dtype), v)


def swa_mixer(pb, pos, k_prev, v_prev, sinks):
    B, T, _ = pb.shape
    q = rope_partial(pb[..., :Q_W].reshape(B, T, Q_HEADS, HEAD_DIM), pos)
    k = rope_partial(pb[..., Q_W:Q_W + KV_W].reshape(B, T, KV_HEADS, HEAD_DIM), pos)
    v = pb[..., Q_W + KV_W:].reshape(B, T, KV_HEADS, HEAD_DIM)
    qg = q.reshape(B, T, KV_HEADS, GROUP, HEAD_DIM)
    if k_prev is None:
        nb = T // WINDOW
        qb = qg.reshape(B, nb, WINDOW, KV_HEADS, GROUP, HEAD_DIM)
        kb = k.reshape(B, nb, WINDOW, KV_HEADS, HEAD_DIM)
        vb = v.reshape(B, nb, WINDOW, KV_HEADS, HEAD_DIM)
        pad = jnp.zeros_like(kb[:, :1])
        kk = jnp.concatenate([jnp.concatenate([pad, kb[:, :-1]], axis=1), kb], axis=2)
        vv = jnp.concatenate([jnp.concatenate([pad, vb[:, :-1]], axis=1), vb], axis=2)
        i = jnp.arange(WINDOW)[:, None]
        j = jnp.arange(2 * WINDOW)[None, :]
        diff = WINDOW + i - j
        band = (diff >= 0) & (diff < WINDOW)
        first = (jnp.arange(nb) == 0)[:, None, None] & (j < WINDOW)[None]
        mask = (band[None] & ~first)[:, None, None]
        o = sink_attention(qb, kk, vv, mask, sinks).reshape(B, T, Q_W)
        new_k, new_v = k[:, -WINDOW:], v[:, -WINDOW:]
    else:
        kk = jnp.concatenate([k_prev.astype(k.dtype), k], axis=1)
        vv = jnp.concatenate([v_prev.astype(v.dtype), v], axis=1)
        kpos = jnp.concatenate([pos[0] - WINDOW + jnp.arange(WINDOW), pos])
        diff = pos[:, None] - kpos[None, :]
        mask = (diff >= 0) & (diff < WINDOW)
        o = sink_attention(qg, kk, vv, mask, sinks).reshape(B, T, Q_W)
        new_k, new_v = kk[:, -WINDOW:], vv[:, -WINDOW:]
    return o, new_k, new_v


def rwkv7_mixer(pa, shift_prev, wkv_prev, mu, w0, w2, a0, a2, g2, k_k, k_a, r_k, ln_w, ln_b):
    B, T, _ = pa.shape
    prev = jnp.concatenate([shift_prev[:, None].astype(pa.dtype), pa[:, :-1]], axis=1)
    xm = pa + (prev - pa) * mu
    r = xm[..., :A_WIDTH]
    k = xm[..., A_WIDTH:2 * A_WIDTH]
    v = xm[..., 2 * A_WIDTH:3 * A_WIDTH]
    o0 = 3 * A_WIDTH
    wd = xm[..., o0:o0 + DECAY_LORA]
    ad = xm[..., o0 + DECAY_LORA:o0 + DECAY_LORA + ICLR_LORA]
    gd = xm[..., o0 + DECAY_LORA + ICLR_LORA:]
    w_log = -jax.nn.softplus(-(w0 + jnp.tanh(wd) @ w2)) - 0.5
    a = jax.nn.sigmoid(a0 + ad @ a2)
    g = jax.nn.sigmoid(gd) @ g2
    heads = lambda t: t.reshape(B, T, A_HEADS, A_HEAD_DIM).astype(jnp.float32)
    kk = heads(k * k_k)
    kk = kk / jnp.maximum(jnp.sqrt(jnp.sum(kk * kk, axis=-1, keepdims=True)), 1e-12)
    k = k * (1 + (a - 1) * k_a)
    rh, kh, vh, ah = heads(r), heads(k), heads(v), heads(a)
    decay = jnp.exp(-jnp.exp(heads(w_log)))

    def step(S, inp):
        r_t, d_t, k_t, v_t, kk_t, a_t = inp
        sk = jnp.einsum('bhij,bhj->bhi', S, kk_t)
        S = S * d_t[:, :, None, :] - sk[..., :, None] * (kk_t * a_t)[:, :, None, :] + v_t[..., :, None] * k_t[:, :, None, :]
        return S, jnp.einsum('bhij,bhj->bhi', S, r_t)

    seq = tuple(jnp.swapaxes(t, 0, 1) for t in (rh, decay, kh, vh, kk, ah))
    S_T, o = lax.scan(step, wkv_prev.astype(jnp.float32), seq)
    o = jnp.swapaxes(o, 0, 1)
    mean = jnp.mean(o, axis=-1, keepdims=True)
    var = jnp.mean(jnp.square(o - mean), axis=-1, keepdims=True)
    o = (o - mean) * lax.rsqrt(var + GN_EPS)
    o = o * ln_w.astype(jnp.float32).reshape(A_HEADS, A_HEAD_DIM) + ln_b.astype(jnp.float32).reshape(A_HEADS, A_HEAD_DIM)
    bonus = jnp.sum(rh * kh * r_k.astype(jnp.float32), axis=-1, keepdims=True) * vh
    y = (o + bonus).reshape(B, T, A_WIDTH).astype(pa.dtype) * g
    return y, pa[:, -1], S_T


def decoder_layer(x, c, pos, shift_prev, wkv_prev, k_prev, v_prev,
                  norm_g, w_mod, b_mod, ffn_w_gate, ffn_w_up, ffn_w_down, w_in,
                  mu, w0, w2, a0, a2, g2, k_k, k_a, r_k, ln_w, ln_b, sinks,
                  w_branch_a, w_branch_b, w_merge_gate, w_out):
    mod = (jax.nn.silu(c) @ w_mod + b_mod).reshape(c.shape[0], N_MOD, 1, D_MODEL)
    h = ada_norm(x, norm_g[0], mod[:, 0], mod[:, 1])
    x = x + 0.5 * mod[:, 2] * swiglu(h, ffn_w_gate[0], ffn_w_up[0], ffn_w_down[0])
    h = ada_norm(x, norm_g[1], mod[:, 3], mod[:, 4])
    proj = h @ w_in
    ya, new_shift, new_wkv = rwkv7_mixer(proj[..., :A_PROJ], shift_prev, wkv_prev, mu, w0, w2, a0, a2, g2, k_k, k_a, r_k, ln_w, ln_b)
    yb, new_k, new_v = swa_mixer(proj[..., A_PROJ:], pos, k_prev, v_prev, sinks)
    gates = jax.nn.sigmoid(h @ w_merge_gate)
    merged = gates[..., :D_MODEL] * (ya @ w_branch_a) + gates[..., D_MODEL:] * (yb @ w_branch_b)
    x = x + mod[:, 5] * (merged @ w_out)
    h = ada_norm(x, norm_g[2], mod[:, 6], mod[:, 7])
    x = x + 0.5 * mod[:, 8] * swiglu(h, ffn_w_gate[1], ffn_w_up[1], ffn_w_down[1])
    return x, new_shift, new_wkv, new_k, new_v


def setup_inputs(seed: int = 0) -> dict:
    key = jax.random.key(seed)
    keys = iter(jax.random.split(key, 48))

    def nrm(shape, scale=1.0):
        return jax.random.normal(next(keys), shape, jnp.float32) * scale

    def uni(shape, lo, hi):
        return jax.random.uniform(next(keys), shape, jnp.float32, lo, hi)

    D, L = D_MODEL, DEPTH
    return {
        'x_prompt': nrm((BATCH, SEQ, D)),
        'x_sample': nrm((DEC_BATCH, DEC_SEQ, D)),
        'c_prompt': nrm((BATCH, D)),
        'c_sample': nrm((DEC_BATCH, D)),
        'state_shift': nrm((L, DEC_BATCH, A_PROJ)),
        'state_wkv': nrm((L, DEC_BATCH, A_HEADS, A_HEAD_DIM, A_HEAD_DIM)),
        'cache_k': nrm((L, DEC_BATCH, WINDOW, KV_HEADS, HEAD_DIM)),
        'cache_v': nrm((L, DEC_BATCH, WINDOW, KV_HEADS, HEAD_DIM)),
        'norm_g': 1.0 + nrm((L, 3, D), 0.02),
        'w_mod': nrm((L, D, N_MOD * D), 0.5 * D ** -0.5),
        'b_mod': nrm((L, N_MOD * D), 0.01),
        'ffn_w_gate': nrm((L, 2, D, D_FF), D ** -0.5),
        'ffn_w_up': nrm((L, 2, D, D_FF), D ** -0.5),
        'ffn_w_down': nrm((L, 2, D_FF, D), D_FF ** -0.5),
        'w_in': nrm((L, D, IN_PROJ), D ** -0.5),
        'rwkv_mu': uni((L, A_PROJ), 0.0, 1.0),
        'rwkv_w0': uni((L, A_WIDTH), -6.0, -1.0),
        'rwkv_w2': nrm((L, DECAY_LORA, A_WIDTH), 0.5 * DECAY_LORA ** -0.5),
        'rwkv_a0': nrm((L, A_WIDTH), 0.1),
        'rwkv_a2': nrm((L, ICLR_LORA, A_WIDTH), ICLR_LORA ** -0.5),
        'rwkv_g2': nrm((L, GATE_LORA, A_WIDTH), GATE_LORA ** -0.5),
        'rwkv_k_k': 0.85 + nrm((L, A_WIDTH), 0.05),
        'rwkv_k_a': 1.0 + nrm((L, A_WIDTH), 0.05),
        'rwkv_r_k': nrm((L, A_HEADS, A_HEAD_DIM), 0.1),
        'rwkv_ln_w': 1.0 + nrm((L, A_WIDTH), 0.02),
        'rwkv_ln_b': nrm((L, A_WIDTH), 0.02),
        'attn_sinks': nrm((L, Q_HEADS), 1.0),
        'w_branch_a': nrm((L, A_WIDTH, D), A_WIDTH ** -0.5),
        'w_branch_b': nrm((L, Q_W, D), Q_W ** -0.5),
        'w_merge_gate': nrm((L, D, 2 * D), D ** -0.5),
        'w_out': nrm((L, D, D), D ** -0.5),
        'final_norm': 1.0 + nrm((D,), 0.02),
    }


def reference(x_prompt, x_sample, c_prompt, c_sample, state_shift, state_wkv, cache_k, cache_v,
              norm_g, w_mod, b_mod, ffn_w_gate, ffn_w_up, ffn_w_down, w_in,
              rwkv_mu, rwkv_w0, rwkv_w2, rwkv_a0, rwkv_a2, rwkv_g2, rwkv_k_k, rwkv_k_a, rwkv_r_k,
              rwkv_ln_w, rwkv_ln_b, attn_sinks, w_branch_a, w_branch_b, w_merge_gate, w_out, final_norm):
    pos_prompt = jnp.arange(SEQ)
    pos_sample = PAST_LEN + jnp.arange(DEC_SEQ)
    zero_shift = jnp.zeros((BATCH, A_PROJ), x_prompt.dtype)
    zero_wkv = jnp.zeros((BATCH, A_HEADS, A_HEAD_DIM, A_HEAD_DIM), jnp.float32)
    xp, xs = x_prompt, x_sample
    sh_p, wk_p, k_p, v_p = [], [], [], []
    sh_s, wk_s, k_s, v_s = [], [], [], []
    for l in range(DEPTH):
        lw = (norm_g[l], w_mod[l], b_mod[l], ffn_w_gate[l], ffn_w_up[l], ffn_w_down[l], w_in[l],
              rwkv_mu[l], rwkv_w0[l], rwkv_w2[l], rwkv_a0[l], rwkv_a2[l], rwkv_g2[l], rwkv_k_k[l],
              rwkv_k_a[l], rwkv_r_k[l], rwkv_ln_w[l], rwkv_ln_b[l], attn_sinks[l],
              w_branch_a[l], w_branch_b[l], w_merge_gate[l], w_out[l])
        xp, a1, a2, a3, a4 = decoder_layer(xp, c_prompt, pos_prompt, zero_shift, zero_wkv, None, None, *lw)
        sh_p.append(a1); wk_p.append(a2); k_p.append(a3); v_p.append(a4)
        xs, b1, b2, b3, b4 = decoder_layer(xs, c_sample, pos_sample, state_shift[l], state_wkv[l], cache_k[l], cache_v[l], *lw)
        sh_s.append(b1); wk_s.append(b2); k_s.append(b3); v_s.append(b4)
    y_prompt = rms_norm(xp, final_norm)
    y_sample = rms_norm(xs, final_norm)
    return (y_prompt, y_sample,
            jnp.stack(sh_p), jnp.stack(wk_p), jnp.stack(k_p), jnp.stack(v_p),
            jnp.stack(sh_s), jnp.stack(wk_s), jnp.stack(k_s), jnp.stack(v_s))
```

```python
import functools

import jax
import jax.numpy as jnp
from jax import lax
from jax.experimental import pallas as pl
from jax.experimental.pallas import tpu as pltpu

D_MODEL = 2048
DEPTH = 2
A_HEADS = 16
A_HEAD_DIM = 64
A_WIDTH = A_HEADS * A_HEAD_DIM
DECAY_LORA = 64
ICLR_LORA = 64
GATE_LORA = 160
LORA_W = DECAY_LORA + ICLR_LORA + GATE_LORA
A_PROJ = 3 * A_WIDTH + LORA_W
GN_EPS = 64e-5
Q_HEADS = 16
KV_HEADS = 4
GROUP = Q_HEADS // KV_HEADS
HEAD_DIM = 64
Q_W = Q_HEADS * HEAD_DIM
KV_W = KV_HEADS * HEAD_DIM
WINDOW = 128
ROPE_DIM = HEAD_DIM // 4
ROPE_HALF = ROPE_DIM // 2
ROPE_THETA = 500000.0
D_FF = 5632
N_MOD = 9
RMS_EPS = 1e-6

PAST_LEN = 8192
LANE = 128
SUBLANE = 8
SEG = 256
A_PAD = 3584
LORA_PAD = A_PAD - 3 * A_WIDTH
Q_OFF = A_PAD
K_OFF = Q_OFF + Q_W
V_OFF = K_OFF + KV_W
PROJ_W = V_OFF + KV_W
QK_W = Q_W + KV_W
VMEM_LIMIT = 56 * 1024 * 1024

F32 = jnp.float32
BF16 = jnp.bfloat16


def _cparams(n_axes):
    return pltpu.CompilerParams(dimension_semantics=("arbitrary",) * n_axes,
                                vmem_limit_bytes=VMEM_LIMIT)


def _bdot(a, b):
    return jnp.dot(a, b, preferred_element_type=F32)


def _seg_sum(x, bsel):
    hi = x.astype(BF16)
    lo = (x - hi.astype(F32)).astype(BF16)
    parts = []
    for c in range(x.shape[1] // SEG):
        sl = slice(c * SEG, (c + 1) * SEG)
        parts.append(_bdot(hi[:, sl], bsel) + _bdot(lo[:, sl], bsel))
    return jnp.concatenate(parts, axis=1)


def _mod_kernel(c_ref, w_ref, b_ref, o_ref):
    c = c_ref[...]
    a = (c * jax.nn.sigmoid(c)).astype(BF16)
    o_ref[...] = _bdot(a, w_ref[...]) + b_ref[...]


def _modulation(c, w_mod, b_mod):
    R = c.shape[0]
    N = w_mod.shape[1]
    tn = 1024
    return pl.pallas_call(
        _mod_kernel,
        grid=(N // tn,),
        in_specs=[pl.BlockSpec((R, D_MODEL), lambda j: (0, 0)),
                  pl.BlockSpec((D_MODEL, tn), lambda j: (0, j)),
                  pl.BlockSpec((1, tn), lambda j: (0, j))],
        out_specs=pl.BlockSpec((R, tn), lambda j: (0, j)),
        out_shape=jax.ShapeDtypeStruct((R, N), F32),
        compiler_params=_cparams(1),
        name="modulation",
    )(c, w_mod, b_mod.reshape(1, N))


def _ada_norm_kernel(x_ref, g_ref, sh_ref, sc_ref, o_ref):
    x = x_ref[...]
    y = x * lax.rsqrt(jnp.mean(x * x, axis=-1, keepdims=True) + RMS_EPS)
    y = y * g_ref[...]
    o_ref[...] = (y * (1.0 + sc_ref[...]) + sh_ref[...]).astype(o_ref.dtype)


def _ada_norm(x, gain, shift, scale, gmap, tm):
    M = x.shape[0]
    R = shift.shape[1]
    mspec = pl.BlockSpec((None, R, D_MODEL), lambda i: (gmap(i), 0, 0))
    return pl.pallas_call(
        _ada_norm_kernel,
        grid=(M // tm,),
        in_specs=[pl.BlockSpec((tm, D_MODEL), lambda i: (i, 0)),
                  pl.BlockSpec((1, D_MODEL), lambda i: (0, 0)),
                  mspec, mspec],
        out_specs=pl.BlockSpec((tm, D_MODEL), lambda i: (i, 0)),
        out_shape=jax.ShapeDtypeStruct((M, D_MODEL), BF16),
        compiler_params=_cparams(1),
        name="ada_norm",
    )(x, gain.reshape(1, D_MODEL), shift, scale)


def _final_norm_kernel(x_ref, g_ref, o_ref):
    x = x_ref[...]
    y = x * lax.rsqrt(jnp.mean(x * x, axis=-1, keepdims=True) + RMS_EPS)
    o_ref[...] = y * g_ref[...]


def _final_norm(x, gain, tm):
    M = x.shape[0]
    return pl.pallas_call(
        _final_norm_kernel,
        grid=(M // tm,),
        in_specs=[pl.BlockSpec((tm, D_MODEL), lambda i: (i, 0)),
                  pl.BlockSpec((1, D_MODEL), lambda i: (0, 0))],
        out_specs=pl.BlockSpec((tm, D_MODEL), lambda i: (i, 0)),
        out_shape=jax.ShapeDtypeStruct((M, D_MODEL), F32),
        compiler_params=_cparams(1),
        name="final_norm",
    )(x, gain.reshape(1, D_MODEL))


def _gu_kernel(h_ref, wg_ref, wu_ref, o_ref):
    h = h_ref[...]
    g = _bdot(h, wg_ref[...])
    u = _bdot(h, wu_ref[...])
    o_ref[...] = ((g * jax.nn.sigmoid(g)) * u).astype(o_ref.dtype)


def _ffn_gate_up(h, wg, wu, tm, tn=512):
    M, K = h.shape
    N = wg.shape[1]
    return pl.pallas_call(
        _gu_kernel,
        grid=(M // tm, N // tn),
        in_specs=[pl.BlockSpec((tm, K), lambda i, j: (i, 0)),
                  pl.BlockSpec((K, tn), lambda i, j: (0, j)),
                  pl.BlockSpec((K, tn), lambda i, j: (0, j))],
        out_specs=pl.BlockSpec((tm, tn), lambda i, j: (i, j)),
        out_shape=jax.ShapeDtypeStruct((M, N), BF16),
        compiler_params=_cparams(2),
        name="ffn_gate_up",
    )(h, wg, wu)


def _resid_kernel(a_ref, w_ref, x_ref, gate_ref, o_ref, *, coef):
    y = _bdot(a_ref[...], w_ref[...])
    o_ref[...] = x_ref[...] + (coef * gate_ref[...]) * y


def _mm_residual(a, w, x, gate, gmap, coef, tm, tn=512):
    M, K = a.shape
    N = w.shape[1]
    R = gate.shape[1]
    return pl.pallas_call(
        functools.partial(_resid_kernel, coef=coef),
        grid=(M // tm, N // tn),
        in_specs=[pl.BlockSpec((tm, K), lambda i, j: (i, 0)),
                  pl.BlockSpec((K, tn), lambda i, j: (0, j)),
                  pl.BlockSpec((tm, tn), lambda i, j: (i, j)),
                  pl.BlockSpec((None, R, tn), lambda i, j: (gmap(i), 0, j))],
        out_specs=pl.BlockSpec((tm, tn), lambda i, j: (i, j)),
        out_shape=jax.ShapeDtypeStruct((M, N), F32),
        compiler_params=_cparams(2),
        name="mm_residual",
    )(a, w, x, gate)


def _plain_kernel(a_ref, w_ref, o_ref, *, sigmoid):
    y = _bdot(a_ref[...], w_ref[...])
    if sigmoid:
        y = jax.nn.sigmoid(y)
    o_ref[...] = y.astype(o_ref.dtype)


def _mm_plain(a, w, tm, tn=512, sigmoid=False, name="mm_plain"):
    M, K = a.shape
    N = w.shape[1]
    return pl.pallas_call(
        functools.partial(_plain_kernel, sigmoid=sigmoid),
        grid=(M // tm, N // tn),
        in_specs=[pl.BlockSpec((tm, K), lambda i, j: (i, 0)),
                  pl.BlockSpec((K, tn), lambda i, j: (0, j))],
        out_specs=pl.BlockSpec((tm, tn), lambda i, j: (i, j)),
        out_shape=jax.ShapeDtypeStruct((M, N), F32),
        compiler_params=_cparams(2),
        name=name,
    )(a, w)


def _merge_kernel(ya_ref, yb_ref, wa_ref, wb_ref, ga_ref, gb_ref, o_ref):
    pa = _bdot(ya_ref[...], wa_ref[...])
    pb = _bdot(yb_ref[...], wb_ref[...])
    o_ref[...] = (ga_ref[...] * pa + gb_ref[...] * pb).astype(o_ref.dtype)


def _mm_merge(ya, yb, wa, wb, gates, tm, tn=512):
    M, K = ya.shape
    N = wa.shape[1]
    nj = N // tn
    return pl.pallas_call(
        _merge_kernel,
        grid=(M // tm, nj),
        in_specs=[pl.BlockSpec((tm, K), lambda i, j: (i, 0)),
                  pl.BlockSpec((tm, K), lambda i, j: (i, 0)),
                  pl.BlockSpec((K, tn), lambda i, j: (0, j)),
                  pl.BlockSpec((K, tn), lambda i, j: (0, j)),
                  pl.BlockSpec((tm, tn), lambda i, j: (i, j)),
                  pl.BlockSpec((tm, tn), lambda i, j: (i, j + nj))],
        out_specs=pl.BlockSpec((tm, tn), lambda i, j: (i, j)),
        out_shape=jax.ShapeDtypeStruct((M, N), BF16),
        compiler_params=_cparams(2),
        name="mm_merge",
    )(ya, yb, wa, wb, gates, gates)


def _rwkv_prep_kernel(p_ref, sp_ref, mu_ref, w0_ref, a0_ref, kk_ref, ka_ref, rk_ref,
                      w2_ref, a2_ref, g2_ref, bsel_ref,
                      r_out, d_out, k_out, v_out, kk_out, kka_out, g_out, bonus_out,
                      carry_ref, *, seq_len):
    t = pl.program_id(0)
    pa = p_ref[...]
    tm = pa.shape[0]
    row = lax.broadcasted_iota(jnp.int32, pa.shape, 0)
    rolled = pltpu.roll(pa, 1, axis=0)
    if seq_len >= tm:
        @pl.when(t % (seq_len // tm) == 0)
        def _():
            carry_ref[...] = sp_ref[...]

        prev = jnp.where(row == 0, carry_ref[...], rolled)
        carry_ref[...] = pa[tm - 1:tm, :]
    else:
        prev = jnp.where(row % seq_len == 0, sp_ref[...], rolled)
    xm = pa + (prev - pa) * mu_ref[...]
    r = xm[:, 0:A_WIDTH]
    k = xm[:, A_WIDTH:2 * A_WIDTH]
    v = xm[:, 2 * A_WIDTH:3 * A_WIDTH]
    lora = xm[:, 3 * A_WIDTH:A_PAD]
    m_w = _bdot(jnp.tanh(lora).astype(BF16), w2_ref[...])
    m_a = _bdot(lora.astype(BF16), a2_ref[...])
    g = _bdot(jax.nn.sigmoid(lora).astype(BF16), g2_ref[...])
    z = -(w0_ref[...] + m_w)
    softplus = jnp.maximum(z, 0.0) + jnp.log1p(jnp.exp(-jnp.abs(z)))
    w_log = -softplus - 0.5
    a = jax.nn.sigmoid(a0_ref[...] + m_a)
    bsel = bsel_ref[...]
    kkf = k * kk_ref[...]
    norm = jnp.sqrt(_seg_sum(kkf * kkf, bsel))
    kk = kkf / jnp.maximum(norm, 1e-12)
    k2 = k * (1.0 + (a - 1.0) * ka_ref[...])
    r_out[...] = r
    d_out[...] = jnp.exp(-jnp.exp(w_log))
    k_out[...] = k2
    v_out[...] = v
    kk_out[...] = kk
    kka_out[...] = kk * a
    g_out[...] = g
    bonus_out[...] = _seg_sum(r * k2 * rk_ref[...], bsel) * v


def _rwkv_prep(proj, sp, lw, seq_len, tm):
    M = proj.shape[0]
    R = sp.shape[1]
    if seq_len >= tm:
        sp_map = lambda i: (i // (seq_len // tm), 0, 0)
    else:
        sp_map = lambda i: (i, 0, 0)
    row = lambda n: pl.BlockSpec((1, n), lambda i: (0, 0))
    full = lambda a, b: pl.BlockSpec((a, b), lambda i: (0, 0))
    out_spec = pl.BlockSpec((tm, A_WIDTH), lambda i: (i, 0))
    out_shape = jax.ShapeDtypeStruct((M, A_WIDTH), F32)
    return pl.pallas_call(
        functools.partial(_rwkv_prep_kernel, seq_len=seq_len),
        grid=(M // tm,),
        in_specs=[pl.BlockSpec((tm, A_PAD), lambda i: (i, 0)),
                  pl.BlockSpec((None, R, A_PAD), sp_map),
                  row(A_PAD), row(A_WIDTH), row(A_WIDTH), row(A_WIDTH), row(A_WIDTH), row(A_WIDTH),
                  full(LORA_PAD, A_WIDTH), full(LORA_PAD, A_WIDTH), full(LORA_PAD, A_WIDTH),
                  full(SEG, SEG)],
        out_specs=[out_spec] * 8,
        out_shape=[out_shape] * 8,
        scratch_shapes=[pltpu.VMEM((1, A_PAD), F32)],
        compiler_params=_cparams(1),
        name="rwkv_prep",
    )(proj, sp, lw["mu"], lw["w0"], lw["a0"], lw["k_k"], lw["k_a"], lw["r_k"],
      lw["w2"], lw["a2"], lw["g2"], lw["bsel"])


def _rwkv_scan_kernel(r_ref, d_ref, k_ref, v_ref, kk_ref, kka_ref, s0_ref, o_ref, s_out_ref, s_ref, *, steps):
    c = pl.program_id(1)
    n_pairs = A_HEADS // 2
    lane = lax.broadcasted_iota(jnp.int32, (A_HEAD_DIM, LANE), 1)
    left = lane < A_HEAD_DIM

    @pl.when(c == 0)
    def _():
        for p in range(n_pairs):
            s_ref[p] = jnp.concatenate([s0_ref[2 * p], s0_ref[2 * p + 1]], axis=1)

    def half_sums(x):
        sa = jnp.sum(jnp.where(left, x, 0.0), axis=1, keepdims=True)
        sb = jnp.sum(jnp.where(left, 0.0, x), axis=1, keepdims=True)
        return sa, sb

    def run_rows(rows, n_rows):
        for p in range(n_pairs):
            sl = slice(p * LANE, (p + 1) * LANE)
            rb, db, kb, vb, kkb, kkab = (ref[rows, sl] for ref in (r_ref, d_ref, k_ref, v_ref, kk_ref, kka_ref))
            s = s_ref[p]
            o_rows = []
            for u in range(n_rows):
                row = slice(u, u + 1)
                sa, sb = half_sums(s * kkb[row])
                sk = jnp.where(left, sa, sb)
                vt = jnp.broadcast_to(vb[row], (LANE, LANE)).T
                vsel = jnp.where(left, vt[0:A_HEAD_DIM, :], vt[A_HEAD_DIM:LANE, :])
                s = s * db[row] - sk * kkab[row] + vsel * kb[row]
                oa, ob = half_sums(s * rb[row])
                ocol = jnp.concatenate([jnp.broadcast_to(oa, (A_HEAD_DIM, LANE)),
                                        jnp.broadcast_to(ob, (A_HEAD_DIM, LANE))], axis=0)
                o_rows.append(ocol.T[0:1, :])
            s_ref[p] = s
            o_ref[rows, sl] = jnp.concatenate(o_rows, axis=0)

    if steps % SUBLANE == 0:
        def group(i, carry):
            run_rows(pl.ds(pl.multiple_of(i * SUBLANE, SUBLANE), SUBLANE), SUBLANE)
            return carry

        lax.fori_loop(0, steps // SUBLANE, group, 0)
    else:
        run_rows(slice(0, steps), steps)

    @pl.when(c == pl.num_programs(1) - 1)
    def _():
        for p in range(n_pairs):
            s = s_ref[p]
            s_out_ref[2 * p] = s[:, 0:A_HEAD_DIM]
            s_out_ref[2 * p + 1] = s[:, A_HEAD_DIM:LANE]


def _rwkv_scan(ops, s0, B, T):
    tc = min(T, 256)
    ops3 = [a.reshape(B, T, A_WIDTH) for a in ops]
    tspec = pl.BlockSpec((None, tc, A_WIDTH), lambda b, c: (b, c, 0))
    sspec = pl.BlockSpec((None, A_HEADS, A_HEAD_DIM, A_HEAD_DIM), lambda b, c: (b, 0, 0, 0))
    o, s_t = pl.pallas_call(
        functools.partial(_rwkv_scan_kernel, steps=tc),
        grid=(B, T // tc),
        in_specs=[tspec] * 6 + [sspec],
        out_specs=[tspec, sspec],
        out_shape=[jax.ShapeDtypeStruct((B, T, A_WIDTH), F32),
                   jax.ShapeDtypeStruct((B, A_HEADS, A_HEAD_DIM, A_HEAD_DIM), F32)],
        scratch_shapes=[pltpu.VMEM((A_HEADS // 2, A_HEAD_DIM, LANE), F32)],
        compiler_params=_cparams(2),
        name="rwkv_scan",
    )(*ops3, s0)
    return o.reshape(B * T, A_WIDTH), s_t


def _rwkv_post_kernel(o_ref, bonus_ref, g_ref, lnw_ref, lnb_ref, bsel_ref, y_ref):
    o = o_ref[...]
    bsel = bsel_ref[...]
    inv_n = 1.0 / A_HEAD_DIM
    mean = _seg_sum(o, bsel) * inv_n
    c = o - mean
    var = _seg_sum(c * c, bsel) * inv_n
    y = c * lax.rsqrt(var + GN_EPS) * lnw_ref[...] + lnb_ref[...]
    y_ref[...] = ((y + bonus_ref[...]) * g_ref[...]).astype(BF16)


def _rwkv_post(o, bonus, g, lw, tm):
    M = o.shape[0]
    tile = pl.BlockSpec((tm, A_WIDTH), lambda i: (i, 0))
    row = pl.BlockSpec((1, A_WIDTH), lambda i: (0, 0))
    return pl.pallas_call(
        _rwkv_post_kernel,
        grid=(M // tm,),
        in_specs=[tile, tile, tile, row, row, pl.BlockSpec((SEG, SEG), lambda i: (0, 0))],
        out_specs=tile,
        out_shape=jax.ShapeDtypeStruct((M, A_WIDTH), BF16),
        compiler_params=_cparams(1),
        name="rwkv_post",
    )(o, bonus, g, lw["ln_w"], lw["ln_b"], lw["bsel"])


def _rope_kernel(x_ref, c_ref, s1_ref, s2_ref, o_ref):
    x = x_ref[...]
    fwd = pltpu.roll(x, x.shape[1] - ROPE_HALF, axis=1)
    bwd = pltpu.roll(x, ROPE_HALF, axis=1)
    o_ref[...] = x * c_ref[...] + fwd * s1_ref[...] + bwd * s2_ref[...]


def _rope(proj, tables, tm):
    M = proj.shape[0]
    n_tab = tables[0].shape[0] // tm
    tab = pl.BlockSpec((tm, SEG), lambda i, j: (i % n_tab, 0))
    return pl.pallas_call(
        _rope_kernel,
        grid=(M // tm, QK_W // SEG),
        in_specs=[pl.BlockSpec((tm, SEG), lambda i, j: (i, Q_OFF // SEG + j)), tab, tab, tab],
        out_specs=pl.BlockSpec((tm, SEG), lambda i, j: (i, j)),
        out_shape=jax.ShapeDtypeStruct((M, QK_W), F32),
        compiler_params=_cparams(2),
        name="rope",
    )(proj, *tables)


def _swa_prompt_kernel(q_ref, kc_ref, kp_ref, vc_ref, vp_ref, sink_ref, o_ref):
    n = pl.program_id(1)
    rows = GROUP * WINDOW
    i = lax.broadcasted_iota(jnp.int32, (rows, 2 * WINDOW), 0) % WINDOW
    j = lax.broadcasted_iota(jnp.int32, (rows, 2 * WINDOW), 1)
    diff = WINDOW + i - j
    mask = (diff >= 0) & (diff < WINDOW) & ((n > 0) | (j >= WINDOW))
    q = q_ref[...].astype(BF16)
    kc = kc_ref[...].astype(BF16)
    kp = kp_ref[...].astype(BF16)
    vc = vc_ref[...].astype(BF16)
    vp = vp_ref[...].astype(BF16)
    outs = []
    for kh in range(KV_HEADS):
        ksl = slice(kh * HEAD_DIM, (kh + 1) * HEAD_DIM)
        kk = jnp.concatenate([kp[:, ksl], kc[:, ksl]], axis=0)
        vv = jnp.concatenate([vp[:, ksl], vc[:, ksl]], axis=0)
        heads = [kh * GROUP + g for g in range(GROUP)]
        qh = jnp.concatenate([q[:, h * HEAD_DIM:(h + 1) * HEAD_DIM] for h in heads], axis=0)
        s = lax.dot_general(qh, kk, (((1,), (1,)), ((), ())), preferred_element_type=F32)
        s = jnp.where(mask, s * (HEAD_DIM ** -0.5), -jnp.inf)
        sink = sink_ref[kh]
        m = jnp.maximum(jnp.max(s, axis=1, keepdims=True), sink)
        e = jnp.exp(s - m)
        denom = jnp.sum(e, axis=1, keepdims=True) + jnp.exp(sink - m)
        p = (e / denom).astype(BF16)
        o = _bdot(p, vv)
        outs.extend(o[g * WINDOW:(g + 1) * WINDOW, :] for g in range(GROUP))
    o_ref[...] = jnp.concatenate(outs, axis=1).astype(o_ref.dtype)


def _swa_prompt(qk_rot, proj, sinks, B, T):
    nb = T // WINDOW
    rows = GROUP * WINDOW
    kcol = Q_W // KV_W
    vcol = V_OFF // KV_W
    cur = lambda col: (lambda b, n: (b * nb + n, col))
    prev = lambda col: (lambda b, n: (b * nb + jnp.maximum(n - 1, 0), col))
    sink_rows = jnp.repeat(sinks.reshape(KV_HEADS, GROUP), WINDOW, axis=1).reshape(KV_HEADS, rows, 1)
    return pl.pallas_call(
        _swa_prompt_kernel,
        grid=(B, nb),
        in_specs=[pl.BlockSpec((WINDOW, Q_W), cur(0)),
                  pl.BlockSpec((WINDOW, KV_W), cur(kcol)),
                  pl.BlockSpec((WINDOW, KV_W), prev(kcol)),
                  pl.BlockSpec((WINDOW, KV_W), cur(vcol)),
                  pl.BlockSpec((WINDOW, KV_W), prev(vcol)),
                  pl.BlockSpec((KV_HEADS, rows, 1), lambda b, n: (0, 0, 0))],
        out_specs=pl.BlockSpec((WINDOW, Q_W), cur(0)),
        out_shape=jax.ShapeDtypeStruct((B * T, Q_W), BF16),
        compiler_params=_cparams(2),
        name="swa_prompt",
    )(qk_rot, qk_rot, qk_rot, proj, proj, sink_rows)


def _swa_sample_kernel(q_ref, kn_ref, vn_ref, ck_ref, cv_ref, sink_ref, o_ref, *, steps):
    rows = GROUP * steps
    t_row = lax.broadcasted_iota(jnp.int32, (rows, WINDOW), 0) % steps
    c_col = lax.broadcasted_iota(jnp.int32, (rows, WINDOW), 1)
    cache_mask = c_col > t_row
    t_col = lax.broadcasted_iota(jnp.int32, (rows, 1), 0) % steps
    scale = HEAD_DIM ** -0.5
    ck = ck_ref[...].astype(BF16)
    cv = cv_ref[...].astype(BF16)
    kn = kn_ref[...].astype(BF16).astype(F32)
    vn = vn_ref[...].astype(BF16).astype(F32)
    for kh in range(KV_HEADS):
        ksl = slice(kh * HEAD_DIM, (kh + 1) * HEAD_DIM)
        qh = q_ref[kh].astype(BF16)
        sc = lax.dot_general(qh, ck[:, ksl], (((1,), (1,)), ((), ())), preferred_element_type=F32)
        sc = jnp.where(cache_mask, sc * scale, -jnp.inf)
        qf = qh.astype(F32)
        sn = []
        for u in range(steps):
            d = jnp.sum(qf * kn[u:u + 1, ksl], axis=1, keepdims=True) * scale
            sn.append(jnp.where(t_col >= u, d, -jnp.inf))
        sink = sink_ref[kh]
        m = jnp.maximum(jnp.max(sc, axis=1, keepdims=True), sink)
        for d in sn:
            m = jnp.maximum(m, d)
        ec = jnp.exp(sc - m)
        en = [jnp.exp(d - m) for d in sn]
        denom = jnp.sum(ec, axis=1, keepdims=True) + jnp.exp(sink - m)
        for e in en:
            denom = denom + e
        o = _bdot((ec / denom).astype(BF16), cv[:, ksl])
        for u in range(steps):
            o = o + (en[u] / denom).astype(BF16).astype(F32) * vn[u:u + 1, ksl]
        o_ref[kh] = o


def _swa_sample(qk_rot, proj, cache_k, cache_v, sinks, B, T):
    rows = GROUP * T
    qg = qk_rot[:, :Q_W].reshape(B, T, KV_HEADS, GROUP, HEAD_DIM).transpose(0, 2, 3, 1, 4)
    qg = qg.reshape(B, KV_HEADS, rows, HEAD_DIM)
    kn = qk_rot[:, Q_W:].reshape(B, T, KV_W)
    vn = proj[:, V_OFF:V_OFF + KV_W].reshape(B, T, KV_W)
    sink_rows = jnp.repeat(sinks.reshape(KV_HEADS, GROUP), T, axis=1).reshape(KV_HEADS, rows, 1)
    o = pl.pallas_call(
        functools.partial(_swa_sample_kernel, steps=T),
        grid=(B,),
        in_specs=[pl.BlockSpec((None, KV_HEADS, rows, HEAD_DIM), lambda b: (b, 0, 0, 0)),
                  pl.BlockSpec((None, T, KV_W), lambda b: (b, 0, 0)),
                  pl.BlockSpec((None, T, KV_W), lambda b: (b, 0, 0)),
                  pl.BlockSpec((None, WINDOW, KV_W), lambda b: (b, 0, 0)),
                  pl.BlockSpec((None, WINDOW, KV_W), lambda b: (b, 0, 0)),
                  pl.BlockSpec((KV_HEADS, rows, 1), lambda b: (0, 0, 0))],
        out_specs=pl.BlockSpec((None, KV_HEADS, rows, HEAD_DIM), lambda b: (b, 0, 0, 0)),
        out_shape=jax.ShapeDtypeStruct((B, KV_HEADS, rows, HEAD_DIM), F32),
        compiler_params=_cparams(1),
        name="swa_sample",
    )(qg, kn, vn, cache_k.reshape(B, WINDOW, KV_W), cache_v.reshape(B, WINDOW, KV_W), sink_rows)
    o = o.reshape(B, KV_HEADS, GROUP, T, HEAD_DIM).transpose(0, 3, 1, 2, 4).reshape(B * T, Q_W)
    return o.astype(BF16)


def _rope_tables(pos):
    inv = 1.0 / (ROPE_THETA ** (jnp.arange(0, ROPE_DIM, 2, dtype=F32) / ROPE_DIM))
    ang = pos.astype(F32)[:, None] * inv[None, :]
    cos, sin = jnp.cos(ang), jnp.sin(ang)
    n = pos.shape[0]
    ones = jnp.ones((n, HEAD_DIM - ROPE_DIM), F32)
    zeros = jnp.zeros((n, HEAD_DIM - ROPE_DIM), F32)
    zh = jnp.zeros((n, ROPE_HALF), F32)
    c = jnp.concatenate([cos, cos, ones], axis=1)
    s1 = jnp.concatenate([-sin, zh, zeros], axis=1)
    s2 = jnp.concatenate([zh, sin, zeros], axis=1)
    return tuple(jnp.tile(a, (1, SEG // HEAD_DIM)) for a in (c, s1, s2))


def _layer_weights(l, norm_g, w_mod, b_mod, ffn_w_gate, ffn_w_up, ffn_w_down, w_in, rwkv_mu, rwkv_w0, rwkv_w2,
                   rwkv_a0, rwkv_a2, rwkv_g2, rwkv_k_k, rwkv_k_a, rwkv_r_k, rwkv_ln_w, rwkv_ln_b, attn_sinks,
                   w_branch_a, w_branch_b, w_merge_gate, w_out):
    wi = w_in[l]
    zc = lambda n: jnp.zeros((D_MODEL, n), wi.dtype)
    w_proj = jnp.concatenate([wi[:, :A_PROJ], zc(A_PAD - A_PROJ), wi[:, A_PROJ:]], axis=1)
    lora_rows = lambda w, off: jnp.zeros((LORA_PAD, A_WIDTH), F32).at[off:off + w.shape[0]].set(w).astype(BF16)
    seg = jnp.arange(SEG) // A_HEAD_DIM
    row = lambda a: a.reshape(1, -1)
    return {
        "norm_g": norm_g[l], "w_mod": w_mod[l].astype(BF16), "b_mod": b_mod[l],
        "wg": ffn_w_gate[l].astype(BF16), "wu": ffn_w_up[l].astype(BF16), "wd": ffn_w_down[l].astype(BF16),
        "w_proj": w_proj.astype(BF16),
        "mu": jnp.pad(rwkv_mu[l], (0, A_PAD - A_PROJ)).reshape(1, A_PAD),
        "w0": row(rwkv_w0[l]), "a0": row(rwkv_a0[l]), "k_k": row(rwkv_k_k[l]), "k_a": row(rwkv_k_a[l]),
        "r_k": row(rwkv_r_k[l]), "ln_w": row(rwkv_ln_w[l]), "ln_b": row(rwkv_ln_b[l]),
        "w2": lora_rows(rwkv_w2[l], 0), "a2": lora_rows(rwkv_a2[l], DECAY_LORA),
        "g2": lora_rows(rwkv_g2[l], DECAY_LORA + ICLR_LORA),
        "bsel": (seg[:, None] == seg[None, :]).astype(BF16),
        "sinks": attn_sinks[l],
        "w_a": w_branch_a[l].astype(BF16), "w_b": w_branch_b[l].astype(BF16),
        "w_gate": w_merge_gate[l].astype(BF16), "w_out": w_out[l].astype(BF16),
    }


def _decoder_layer(x, mod, lw, B, T, shift_prev, wkv_prev, cache_k, cache_v, tables):
    M = B * T
    tm = min(M, 1024) if T >= 1024 else M
    tm_k = min(tm, 512)
    tm_prep = min(tm, 256)
    per_seq = T >= tm
    if per_seq:
        modv = lambda n: mod[:, n:n + 1, :]
        gmap = lambda tile: (lambda i: i // (T // tile))
        sp = jnp.pad(shift_prev, ((0, 0), (0, A_PAD - A_PROJ))).reshape(B, 1, A_PAD)
    else:
        modv = lambda n: jnp.repeat(mod[:, n, :], T, axis=0).reshape(1, M, D_MODEL)
        gmap = lambda tile: (lambda i: i)
        sp = jnp.repeat(jnp.pad(shift_prev, ((0, 0), (0, A_PAD - A_PROJ))), T, axis=0)
        sp = sp.reshape(M // tm_prep, tm_prep, A_PAD)

    def ffn(x, n, which):
        h = _ada_norm(x, lw["norm_g"][n], modv(3 * n), modv(3 * n + 1), gmap(tm), tm)
        act = _ffn_gate_up(h, lw["wg"][which], lw["wu"][which], tm)
        gate = modv(3 * n + 2)
        if not per_seq:
            gate = gate.reshape(M // tm_k, tm_k, D_MODEL)
        return _mm_residual(act, lw["wd"][which], x, gate, gmap(tm_k), 0.5, tm_k)

    x = ffn(x, 0, 0)

    h = _ada_norm(x, lw["norm_g"][1], modv(3), modv(4), gmap(tm), tm)
    proj = _mm_plain(h, lw["w_proj"], tm, name="in_proj")
    gates = _mm_plain(h, lw["w_gate"], tm, sigmoid=True, name="merge_gates")

    ops = _rwkv_prep(proj, sp, lw, T, tm_prep)
    r, dec, k2, v, kk, kka, g, bonus = ops
    o, new_wkv = _rwkv_scan((r, dec, k2, v, kk, kka), wkv_prev, B, T)
    ya = _rwkv_post(o, bonus, g, lw, min(tm, 512))
    new_shift = proj.reshape(B, T, PROJ_W)[:, -1, :A_PROJ]

    qk_rot = _rope(proj, tables, min(tm, 512))
    v_new = proj[:, V_OFF:V_OFF + KV_W].reshape(B, T, KV_HEADS, HEAD_DIM)
    k_new = qk_rot[:, Q_W:].reshape(B, T, KV_HEADS, HEAD_DIM)
    if cache_k is None:
        yb = _swa_prompt(qk_rot, proj, lw["sinks"], B, T)
        new_k, new_v = k_new[:, -WINDOW:], v_new[:, -WINDOW:]
    else:
        yb = _swa_sample(qk_rot, proj, cache_k, cache_v, lw["sinks"], B, T)
        new_k = jnp.concatenate([cache_k, k_new], axis=1)[:, -WINDOW:]
        new_v = jnp.concatenate([cache_v, v_new], axis=1)[:, -WINDOW:]

    merged = _mm_merge(ya, yb, lw["w_a"], lw["w_b"], gates, tm)
    gate = modv(5)
    x = _mm_residual(merged, lw["w_out"], x, gate, gmap(tm), 1.0, tm)

    x = ffn(x, 2, 1)
    return x, new_shift, new_wkv, new_k, new_v


def kernel(x_prompt, x_sample, c_prompt, c_sample, state_shift, state_wkv, cache_k, cache_v, norm_g, w_mod, b_mod,
           ffn_w_gate, ffn_w_up, ffn_w_down, w_in, rwkv_mu, rwkv_w0, rwkv_w2, rwkv_a0, rwkv_a2, rwkv_g2, rwkv_k_k,
           rwkv_k_a, rwkv_r_k, rwkv_ln_w, rwkv_ln_b, attn_sinks, w_branch_a, w_branch_b, w_merge_gate, w_out,
           final_norm):
    Bp, Tp, _ = x_prompt.shape
    Bs, Ts, _ = x_sample.shape
    tab_p = _rope_tables(jnp.arange(Tp))
    tab_s = _rope_tables(jnp.tile(PAST_LEN + jnp.arange(Ts), Bs))
    zero_shift = jnp.zeros((Bp, A_PROJ), F32)
    zero_wkv = jnp.zeros((Bp, A_HEADS, A_HEAD_DIM, A_HEAD_DIM), F32)
    xp = x_prompt.reshape(Bp * Tp, D_MODEL)
    xs = x_sample.reshape(Bs * Ts, D_MODEL)
    c_all = jnp.concatenate([c_prompt, c_sample], axis=0)
    pad_rows = (-c_all.shape[0]) % 16
    c_all = jnp.pad(c_all, ((0, pad_rows), (0, 0)))
    outs_p, outs_s = [], []
    for l in range(DEPTH):
        lw = _layer_weights(l, norm_g, w_mod, b_mod, ffn_w_gate, ffn_w_up, ffn_w_down, w_in, rwkv_mu, rwkv_w0,
                            rwkv_w2, rwkv_a0, rwkv_a2, rwkv_g2, rwkv_k_k, rwkv_k_a, rwkv_r_k, rwkv_ln_w, rwkv_ln_b,
                            attn_sinks, w_branch_a, w_branch_b, w_merge_gate, w_out)
        mod = _modulation(c_all, lw["w_mod"], lw["b_mod"]).reshape(-1, N_MOD, D_MODEL)
        xp, *st_p = _decoder_layer(xp, mod[:Bp], lw, Bp, Tp, zero_shift, zero_wkv, None, None, tab_p)
        xs, *st_s = _decoder_layer(xs, mod[Bp:Bp + Bs], lw, Bs, Ts, state_shift[l], state_wkv[l],
                                   cache_k[l], cache_v[l], tab_s)
        outs_p.append(st_p)
        outs_s.append(st_s)
    y_prompt = _final_norm(xp, final_norm, 1024).reshape(Bp, Tp, D_MODEL)
    y_sample = _final_norm(xs, final_norm, Bs * Ts).reshape(Bs, Ts, D_MODEL)
    stack = lambda outs, n: jnp.stack([o[n] for o in outs])
    return (y_prompt, y_sample,
            stack(outs_p, 0), stack(outs_p, 1), stack(outs_p, 2), stack(outs_p, 3),
            stack(outs_s, 0), stack(outs_s, 1), stack(outs_s, 2), stack(outs_s, 3))
```

```python
import functools

import jax
import jax.numpy as jnp
from jax import lax
from jax.experimental import pallas as pl
from jax.experimental.pallas import tpu as pltpu

D_MODEL = 2048
DEPTH = 2
A_HEADS = 16
A_HEAD_DIM = 64
A_WIDTH = A_HEADS * A_HEAD_DIM
DECAY_LORA = 64
ICLR_LORA = 64
GATE_LORA = 160
LORA_W = DECAY_LORA + ICLR_LORA + GATE_LORA
A_PROJ = 3 * A_WIDTH + LORA_W
GN_EPS = 64e-5
Q_HEADS = 16
KV_HEADS = 4
GROUP = Q_HEADS // KV_HEADS
HEAD_DIM = 64
Q_W = Q_HEADS * HEAD_DIM
KV_W = KV_HEADS * HEAD_DIM
WINDOW = 128
ROPE_DIM = HEAD_DIM // 4
ROPE_HALF = ROPE_DIM // 2
ROPE_THETA = 500000.0
D_FF = 5632
N_MOD = 9
RMS_EPS = 1e-6

PAST_LEN = 8192
LANE = 128
SUBLANE = 8
CHUNK = 64
SOLVE_BASE = 8
SEG = 256
A_PAD = 3584
LORA_PAD = A_PAD - 3 * A_WIDTH
Q_OFF = A_PAD
K_OFF = Q_OFF + Q_W
V_OFF = K_OFF + KV_W
PROJ_W = V_OFF + KV_W
QK_W = Q_W + KV_W
VMEM_LIMIT = 56 * 1024 * 1024

F32 = jnp.float32
BF16 = jnp.bfloat16


def _cparams(n_axes):
    return pltpu.CompilerParams(dimension_semantics=("arbitrary",) * n_axes,
                                vmem_limit_bytes=VMEM_LIMIT)


def _bdot(a, b):
    return jnp.dot(a, b, preferred_element_type=F32)


def _seg_sum(x, bsel):
    hi = x.astype(BF16)
    lo = (x - hi.astype(F32)).astype(BF16)
    parts = []
    for c in range(x.shape[1] // SEG):
        sl = slice(c * SEG, (c + 1) * SEG)
        parts.append(_bdot(hi[:, sl], bsel) + _bdot(lo[:, sl], bsel))
    return jnp.concatenate(parts, axis=1)


def _mod_kernel(c_ref, w_ref, b_ref, o_ref):
    c = c_ref[...]
    a = (c * jax.nn.sigmoid(c)).astype(BF16)
    o_ref[...] = _bdot(a, w_ref[...]) + b_ref[...]


def _modulation(c, w_mod, b_mod):
    R = c.shape[0]
    N = w_mod.shape[1]
    tn = 1024
    return pl.pallas_call(
        _mod_kernel,
        grid=(N // tn,),
        in_specs=[pl.BlockSpec((R, D_MODEL), lambda j: (0, 0)),
                  pl.BlockSpec((D_MODEL, tn), lambda j: (0, j)),
                  pl.BlockSpec((1, tn), lambda j: (0, j))],
        out_specs=pl.BlockSpec((R, tn), lambda j: (0, j)),
        out_shape=jax.ShapeDtypeStruct((R, N), F32),
        compiler_params=_cparams(1),
        name="modulation",
    )(c, w_mod, b_mod.reshape(1, N))


def _ada_norm_kernel(x_ref, g_ref, sh_ref, sc_ref, o_ref):
    x = x_ref[...]
    y = x * lax.rsqrt(jnp.mean(x * x, axis=-1, keepdims=True) + RMS_EPS)
    y = y * g_ref[...]
    o_ref[...] = (y * (1.0 + sc_ref[...]) + sh_ref[...]).astype(o_ref.dtype)


def _ada_norm(x, gain, shift, scale, gmap, tm):
    M = x.shape[0]
    R = shift.shape[1]
    mspec = pl.BlockSpec((None, R, D_MODEL), lambda i: (gmap(i), 0, 0))
    return pl.pallas_call(
        _ada_norm_kernel,
        grid=(M // tm,),
        in_specs=[pl.BlockSpec((tm, D_MODEL), lambda i: (i, 0)),
                  pl.BlockSpec((1, D_MODEL), lambda i: (0, 0)),
                  mspec, mspec],
        out_specs=pl.BlockSpec((tm, D_MODEL), lambda i: (i, 0)),
        out_shape=jax.ShapeDtypeStruct((M, D_MODEL), BF16),
        compiler_params=_cparams(1),
        name="ada_norm",
    )(x, gain.reshape(1, D_MODEL), shift, scale)


def _final_norm_kernel(x_ref, g_ref, o_ref):
    x = x_ref[...]
    y = x * lax.rsqrt(jnp.mean(x * x, axis=-1, keepdims=True) + RMS_EPS)
    o_ref[...] = y * g_ref[...]


def _final_norm(x, gain, tm):
    M = x.shape[0]
    return pl.pallas_call(
        _final_norm_kernel,
        grid=(M // tm,),
        in_specs=[pl.BlockSpec((tm, D_MODEL), lambda i: (i, 0)),
                  pl.BlockSpec((1, D_MODEL), lambda i: (0, 0))],
        out_specs=pl.BlockSpec((tm, D_MODEL), lambda i: (i, 0)),
        out_shape=jax.ShapeDtypeStruct((M, D_MODEL), F32),
        compiler_params=_cparams(1),
        name="final_norm",
    )(x, gain.reshape(1, D_MODEL))


def _gu_kernel(h_ref, wg_ref, wu_ref, o_ref):
    h = h_ref[...]
    g = _bdot(h, wg_ref[...])
    u = _bdot(h, wu_ref[...])
    o_ref[...] = ((g * jax.nn.sigmoid(g)) * u).astype(o_ref.dtype)


def _ffn_gate_up(h, wg, wu, tm, tn=512):
    M, K = h.shape
    N = wg.shape[1]
    return pl.pallas_call(
        _gu_kernel,
        grid=(M // tm, N // tn),
        in_specs=[pl.BlockSpec((tm, K), lambda i, j: (i, 0)),
                  pl.BlockSpec((K, tn), lambda i, j: (0, j)),
                  pl.BlockSpec((K, tn), lambda i, j: (0, j))],
        out_specs=pl.BlockSpec((tm, tn), lambda i, j: (i, j)),
        out_shape=jax.ShapeDtypeStruct((M, N), BF16),
        compiler_params=_cparams(2),
        name="ffn_gate_up",
    )(h, wg, wu)


def _resid_kernel(a_ref, w_ref, x_ref, gate_ref, o_ref, *, coef):
    y = _bdot(a_ref[...], w_ref[...])
    o_ref[...] = x_ref[...] + (coef * gate_ref[...]) * y


def _mm_residual(a, w, x, gate, gmap, coef, tm, tn=512):
    M, K = a.shape
    N = w.shape[1]
    R = gate.shape[1]
    return pl.pallas_call(
        functools.partial(_resid_kernel, coef=coef),
        grid=(M // tm, N // tn),
        in_specs=[pl.BlockSpec((tm, K), lambda i, j: (i, 0)),
                  pl.BlockSpec((K, tn), lambda i, j: (0, j)),
                  pl.BlockSpec((tm, tn), lambda i, j: (i, j)),
                  pl.BlockSpec((None, R, tn), lambda i, j: (gmap(i), 0, j))],
        out_specs=pl.BlockSpec((tm, tn), lambda i, j: (i, j)),
        out_shape=jax.ShapeDtypeStruct((M, N), F32),
        compiler_params=_cparams(2),
        name="mm_residual",
    )(a, w, x, gate)


def _plain_kernel(a_ref, w_ref, o_ref, *, sigmoid):
    y = _bdot(a_ref[...], w_ref[...])
    if sigmoid:
        y = jax.nn.sigmoid(y)
    o_ref[...] = y.astype(o_ref.dtype)


def _mm_plain(a, w, tm, tn=512, sigmoid=False, name="mm_plain"):
    M, K = a.shape
    N = w.shape[1]
    return pl.pallas_call(
        functools.partial(_plain_kernel, sigmoid=sigmoid),
        grid=(M // tm, N // tn),
        in_specs=[pl.BlockSpec((tm, K), lambda i, j: (i, 0)),
                  pl.BlockSpec((K, tn), lambda i, j: (0, j))],
        out_specs=pl.BlockSpec((tm, tn), lambda i, j: (i, j)),
        out_shape=jax.ShapeDtypeStruct((M, N), F32),
        compiler_params=_cparams(2),
        name=name,
    )(a, w)


def _merge_kernel(ya_ref, yb_ref, wa_ref, wb_ref, ga_ref, gb_ref, o_ref):
    pa = _bdot(ya_ref[...], wa_ref[...])
    pb = _bdot(yb_ref[...], wb_ref[...])
    o_ref[...] = (ga_ref[...] * pa + gb_ref[...] * pb).astype(o_ref.dtype)


def _mm_merge(ya, yb, wa, wb, gates, tm, tn=512):
    M, K = ya.shape
    N = wa.shape[1]
    nj = N // tn
    return pl.pallas_call(
        _merge_kernel,
        grid=(M // tm, nj),
        in_specs=[pl.BlockSpec((tm, K), lambda i, j: (i, 0)),
                  pl.BlockSpec((tm, K), lambda i, j: (i, 0)),
                  pl.BlockSpec((K, tn), lambda i, j: (0, j)),
                  pl.BlockSpec((K, tn), lambda i, j: (0, j)),
                  pl.BlockSpec((tm, tn), lambda i, j: (i, j)),
                  pl.BlockSpec((tm, tn), lambda i, j: (i, j + nj))],
        out_specs=pl.BlockSpec((tm, tn), lambda i, j: (i, j)),
        out_shape=jax.ShapeDtypeStruct((M, N), BF16),
        compiler_params=_cparams(2),
        name="mm_merge",
    )(ya, yb, wa, wb, gates, gates)


def _rwkv_prep_kernel(p_ref, sp_ref, mu_ref, w0_ref, a0_ref, kk_ref, ka_ref, rk_ref,
                      w2_ref, a2_ref, g2_ref, bsel_ref,
                      r_out, d_out, k_out, v_out, kk_out, kka_out, g_out, bonus_out,
                      carry_ref, *, seq_len):
    t = pl.program_id(0)
    pa = p_ref[...]
    tm = pa.shape[0]
    row = lax.broadcasted_iota(jnp.int32, pa.shape, 0)
    rolled = pltpu.roll(pa, 1, axis=0)
    if seq_len >= tm:
        @pl.when(t % (seq_len // tm) == 0)
        def _():
            carry_ref[...] = sp_ref[...]

        prev = jnp.where(row == 0, carry_ref[...], rolled)
        carry_ref[...] = pa[tm - 1:tm, :]
    else:
        prev = jnp.where(row % seq_len == 0, sp_ref[...], rolled)
    xm = pa + (prev - pa) * mu_ref[...]
    r = xm[:, 0:A_WIDTH]
    k = xm[:, A_WIDTH:2 * A_WIDTH]
    v = xm[:, 2 * A_WIDTH:3 * A_WIDTH]
    lora = xm[:, 3 * A_WIDTH:A_PAD]
    m_w = _bdot(jnp.tanh(lora).astype(BF16), w2_ref[...])
    m_a = _bdot(lora.astype(BF16), a2_ref[...])
    g = _bdot(jax.nn.sigmoid(lora).astype(BF16), g2_ref[...])
    z = -(w0_ref[...] + m_w)
    softplus = jnp.maximum(z, 0.0) + jnp.log1p(jnp.exp(-jnp.abs(z)))
    w_log = -softplus - 0.5
    a = jax.nn.sigmoid(a0_ref[...] + m_a)
    bsel = bsel_ref[...]
    kkf = k * kk_ref[...]
    norm = jnp.sqrt(_seg_sum(kkf * kkf, bsel))
    kk = kkf / jnp.maximum(norm, 1e-12)
    k2 = k * (1.0 + (a - 1.0) * ka_ref[...])
    r_out[...] = r
    d_out[...] = -jnp.exp(w_log)
    k_out[...] = k2
    v_out[...] = v
    kk_out[...] = kk
    kka_out[...] = kk * a
    g_out[...] = g
    bonus_out[...] = _seg_sum(r * k2 * rk_ref[...], bsel) * v


def _rwkv_prep(proj, sp, lw, seq_len, tm):
    M = proj.shape[0]
    R = sp.shape[1]
    if seq_len >= tm:
        sp_map = lambda i: (i // (seq_len // tm), 0, 0)
    else:
        sp_map = lambda i: (i, 0, 0)
    row = lambda n: pl.BlockSpec((1, n), lambda i: (0, 0))
    full = lambda a, b: pl.BlockSpec((a, b), lambda i: (0, 0))
    out_spec = pl.BlockSpec((tm, A_WIDTH), lambda i: (i, 0))
    out_shape = jax.ShapeDtypeStruct((M, A_WIDTH), F32)
    return pl.pallas_call(
        functools.partial(_rwkv_prep_kernel, seq_len=seq_len),
        grid=(M // tm,),
        in_specs=[pl.BlockSpec((tm, A_PAD), lambda i: (i, 0)),
                  pl.BlockSpec((None, R, A_PAD), sp_map),
                  row(A_PAD), row(A_WIDTH), row(A_WIDTH), row(A_WIDTH), row(A_WIDTH), row(A_WIDTH),
                  full(LORA_PAD, A_WIDTH), full(LORA_PAD, A_WIDTH), full(LORA_PAD, A_WIDTH),
                  full(SEG, SEG)],
        out_specs=[out_spec] * 8,
        out_shape=[out_shape] * 8,
        scratch_shapes=[pltpu.VMEM((1, A_PAD), F32)],
        compiler_params=_cparams(1),
        name="rwkv_prep",
    )(proj, sp, lw["mu"], lw["w0"], lw["a0"], lw["k_k"], lw["k_a"], lw["r_k"],
      lw["w2"], lw["a2"], lw["g2"], lw["bsel"])


def _rwkv_scan_kernel(r_ref, d_ref, k_ref, v_ref, kk_ref, kka_ref, s0_ref, o_ref, s_out_ref, s_ref, *, steps):
    c = pl.program_id(1)
    n_pairs = A_HEADS // 2
    lane = lax.broadcasted_iota(jnp.int32, (A_HEAD_DIM, LANE), 1)
    left = lane < A_HEAD_DIM

    @pl.when(c == 0)
    def _():
        for p in range(n_pairs):
            s_ref[p] = jnp.concatenate([s0_ref[2 * p], s0_ref[2 * p + 1]], axis=1)

    def half_sums(x):
        sa = jnp.sum(jnp.where(left, x, 0.0), axis=1, keepdims=True)
        sb = jnp.sum(jnp.where(left, 0.0, x), axis=1, keepdims=True)
        return sa, sb

    def run_rows(rows, n_rows):
        for p in range(n_pairs):
            sl = slice(p * LANE, (p + 1) * LANE)
            rb, wb, kb, vb, kkb, kkab = (ref[rows, sl] for ref in (r_ref, d_ref, k_ref, v_ref, kk_ref, kka_ref))
            db = jnp.exp(wb)
            s = s_ref[p]
            o_rows = []
            for u in range(n_rows):
                row = slice(u, u + 1)
                sa, sb = half_sums(s * kkb[row])
                sk = jnp.where(left, sa, sb)
                vt = jnp.broadcast_to(vb[row], (LANE, LANE)).T
                vsel = jnp.where(left, vt[0:A_HEAD_DIM, :], vt[A_HEAD_DIM:LANE, :])
                s = s * db[row] - sk * kkab[row] + vsel * kb[row]
                oa, ob = half_sums(s * rb[row])
                ocol = jnp.concatenate([jnp.broadcast_to(oa, (A_HEAD_DIM, LANE)),
                                        jnp.broadcast_to(ob, (A_HEAD_DIM, LANE))], axis=0)
                o_rows.append(ocol.T[0:1, :])
            s_ref[p] = s
            o_ref[rows, sl] = jnp.concatenate(o_rows, axis=0)

    if steps % SUBLANE == 0:
        def group(i, carry):
            run_rows(pl.ds(pl.multiple_of(i * SUBLANE, SUBLANE), SUBLANE), SUBLANE)
            return carry

        lax.fori_loop(0, steps // SUBLANE, group, 0)
    else:
        run_rows(slice(0, steps), steps)

    @pl.when(c == pl.num_programs(1) - 1)
    def _():
        for p in range(n_pairs):
            s = s_ref[p]
            s_out_ref[2 * p] = s[:, 0:A_HEAD_DIM]
            s_out_ref[2 * p + 1] = s[:, A_HEAD_DIM:LANE]


def _rwkv_chunk_kernel(r_ref, w_ref, k_ref, v_ref, kk_ref, kka_ref, s0_ref, o_ref, s_out_ref, s_ref):
    c = pl.program_id(1)
    L = CHUNK
    P = 2 * L
    n_pairs = A_HEADS // 2
    left = lax.broadcasted_iota(jnp.int32, (L, LANE), 1) < A_HEAD_DIM
    row = lax.broadcasted_iota(jnp.int32, (P, P), 0)
    col = lax.broadcasted_iota(jnp.int32, (P, P), 1)
    strict = row > col
    lower = row >= col
    eye = (row == col).astype(F32)
    same = lambda n: (row // n) == (col // n)
    base = same(SOLVE_BASE)
    levels = []
    n = SOLVE_BASE
    while n < L:
        levels.append(same(2 * n) & jnp.logical_not(same(n)))
        n *= 2
    bf = lambda x: x.astype(BF16)
    nt = lambda a, b: lax.dot_general(a, b, (((1,), (1,)), ((), ())), preferred_element_type=F32)
    tn = lambda a, b: lax.dot_general(a, b, (((0,), (0,)), ((), ())), preferred_element_type=F32)

    @pl.when(c == 0)
    def _():
        z = jnp.zeros((A_HEAD_DIM, A_HEAD_DIM), F32)
        for p in range(n_pairs):
            s_ref[p] = jnp.concatenate([jnp.concatenate([s0_ref[2 * p], z], axis=1),
                                        jnp.concatenate([z, s0_ref[2 * p + 1]], axis=1)], axis=0)

    w = w_ref[...]
    tl = lax.broadcasted_iota(jnp.int32, (L, L), 0) >= lax.broadcasted_iota(jnp.int32, (L, L), 1)
    tri = tl.astype(BF16)
    w_hi = bf(w)
    w_r1 = w - w_hi.astype(F32)
    w_mid = bf(w_r1)
    w_lo = bf(w_r1 - w_mid.astype(F32))
    cum = _bdot(tri, w_hi) + _bdot(tri, w_mid) + _bdot(tri, w_lo)
    g_in = jnp.exp(cum)
    g_inv = jnp.exp(-cum)
    g_last = g_in[L - 1:L, :]
    at = -kk_ref[...] * jnp.exp(cum - w)
    bt = kka_ref[...] * g_inv
    kt = k_ref[...] * g_inv
    rt = r_ref[...] * g_in
    bl = bt * g_last
    kl = kt * g_last
    v = v_ref[...]

    for p in range(n_pairs):
        sl = slice(p * LANE, (p + 1) * LANE)

        def stack(x):
            xp = x[:, sl]
            return jnp.concatenate([jnp.where(left, xp, 0.0), jnp.where(left, 0.0, xp)], axis=0)

        a2, b2, k2, r2, v2 = bf(stack(at)), bf(stack(bt)), bf(stack(kt)), bf(stack(rt)), bf(stack(v))
        sc = nt(jnp.concatenate([a2, r2], axis=0), jnp.concatenate([b2, k2], axis=0))
        a_ab = jnp.where(strict, sc[0:P, 0:P], 0.0)
        a_ak = jnp.where(strict, sc[0:P, P:2 * P], 0.0)
        a_rb = jnp.where(lower, sc[P:2 * P, 0:P], 0.0)
        a_rk = jnp.where(lower, sc[P:2 * P, P:2 * P], 0.0)
        ad = bf(jnp.where(base, a_ab, 0.0))
        t_inv = eye + ad.astype(F32)
        pw = ad
        m = 2
        while m < SOLVE_BASE:
            pw = bf(_bdot(pw, pw))
            t_inv = t_inv + _bdot(bf(t_inv), pw)
            m *= 2
        for lv in levels:
            off = bf(jnp.where(lv, a_ab, 0.0))
            t_inv = t_inv + _bdot(bf(_bdot(bf(t_inv), off)), bf(t_inv))
        s = s_ref[p]
        s_b = bf(s)
        wmat = nt(a2, s_b) + _bdot(bf(a_ak), v2)
        u = bf(_bdot(bf(t_inv), bf(wmat)))
        o2 = nt(r2, s_b) + _bdot(bf(a_rb), u) + _bdot(bf(a_rk), v2)
        o_ref[:, sl] = o2[0:L, :] + o2[L:P, :]
        upd = tn(jnp.concatenate([u, v2], axis=0), jnp.concatenate([bf(stack(bl)), bf(stack(kl))], axis=0))
        s_ref[p] = s * g_last[:, sl] + upd

    @pl.when(c == pl.num_programs(1) - 1)
    def _():
        for p in range(n_pairs):
            s = s_ref[p]
            s_out_ref[2 * p] = s[0:A_HEAD_DIM, 0:A_HEAD_DIM]
            s_out_ref[2 * p + 1] = s[A_HEAD_DIM:LANE, A_HEAD_DIM:LANE]


def _rwkv_scan_chunked(ops, s0, B, T):
    ops3 = [a.reshape(B, T, A_WIDTH) for a in ops]
    tspec = pl.BlockSpec((None, CHUNK, A_WIDTH), lambda b, c: (b, c, 0))
    sspec = pl.BlockSpec((None, A_HEADS, A_HEAD_DIM, A_HEAD_DIM), lambda b, c: (b, 0, 0, 0))
    o, s_t = pl.pallas_call(
        _rwkv_chunk_kernel,
        grid=(B, T // CHUNK),
        in_specs=[tspec] * 6 + [sspec],
        out_specs=[tspec, sspec],
        out_shape=[jax.ShapeDtypeStruct((B, T, A_WIDTH), F32),
                   jax.ShapeDtypeStruct((B, A_HEADS, A_HEAD_DIM, A_HEAD_DIM), F32)],
        scratch_shapes=[pltpu.VMEM((A_HEADS // 2, LANE, LANE), F32)],
        compiler_params=_cparams(2),
        name="rwkv_chunk",
    )(*ops3, s0)
    return o.reshape(B * T, A_WIDTH), s_t


def _rwkv_scan(ops, s0, B, T):
    if T % CHUNK == 0:
        return _rwkv_scan_chunked(ops, s0, B, T)
    tc = min(T, 256)
    ops3 = [a.reshape(B, T, A_WIDTH) for a in ops]
    tspec = pl.BlockSpec((None, tc, A_WIDTH), lambda b, c: (b, c, 0))
    sspec = pl.BlockSpec((None, A_HEADS, A_HEAD_DIM, A_HEAD_DIM), lambda b, c: (b, 0, 0, 0))
    o, s_t = pl.pallas_call(
        functools.partial(_rwkv_scan_kernel, steps=tc),
        grid=(B, T // tc),
        in_specs=[tspec] * 6 + [sspec],
        out_specs=[tspec, sspec],
        out_shape=[jax.ShapeDtypeStruct((B, T, A_WIDTH), F32),
                   jax.ShapeDtypeStruct((B, A_HEADS, A_HEAD_DIM, A_HEAD_DIM), F32)],
        scratch_shapes=[pltpu.VMEM((A_HEADS // 2, A_HEAD_DIM, LANE), F32)],
        compiler_params=_cparams(2),
        name="rwkv_scan",
    )(*ops3, s0)
    return o.reshape(B * T, A_WIDTH), s_t


def _rwkv_post_kernel(o_ref, bonus_ref, g_ref, lnw_ref, lnb_ref, bsel_ref, y_ref):
    o = o_ref[...]
    bsel = bsel_ref[...]
    inv_n = 1.0 / A_HEAD_DIM
    mean = _seg_sum(o, bsel) * inv_n
    c = o - mean
    var = _seg_sum(c * c, bsel) * inv_n
    y = c * lax.rsqrt(var + GN_EPS) * lnw_ref[...] + lnb_ref[...]
    y_ref[...] = ((y + bonus_ref[...]) * g_ref[...]).astype(BF16)


def _rwkv_post(o, bonus, g, lw, tm):
    M = o.shape[0]
    tile = pl.BlockSpec((tm, A_WIDTH), lambda i: (i, 0))
    row = pl.BlockSpec((1, A_WIDTH), lambda i: (0, 0))
    return pl.pallas_call(
        _rwkv_post_kernel,
        grid=(M // tm,),
        in_specs=[tile, tile, tile, row, row, pl.BlockSpec((SEG, SEG), lambda i: (0, 0))],
        out_specs=tile,
        out_shape=jax.ShapeDtypeStruct((M, A_WIDTH), BF16),
        compiler_params=_cparams(1),
        name="rwkv_post",
    )(o, bonus, g, lw["ln_w"], lw["ln_b"], lw["bsel"])


def _rope_kernel(x_ref, c_ref, s1_ref, s2_ref, o_ref):
    x = x_ref[...]
    fwd = pltpu.roll(x, x.shape[1] - ROPE_HALF, axis=1)
    bwd = pltpu.roll(x, ROPE_HALF, axis=1)
    o_ref[...] = x * c_ref[...] + fwd * s1_ref[...] + bwd * s2_ref[...]


def _rope(proj, tables, tm):
    M = proj.shape[0]
    n_tab = tables[0].shape[0] // tm
    tab = pl.BlockSpec((tm, SEG), lambda i, j: (i % n_tab, 0))
    return pl.pallas_call(
        _rope_kernel,
        grid=(M // tm, QK_W // SEG),
        in_specs=[pl.BlockSpec((tm, SEG), lambda i, j: (i, Q_OFF // SEG + j)), tab, tab, tab],
        out_specs=pl.BlockSpec((tm, SEG), lambda i, j: (i, j)),
        out_shape=jax.ShapeDtypeStruct((M, QK_W), F32),
        compiler_params=_cparams(2),
        name="rope",
    )(proj, *tables)


def _swa_prompt_kernel(q_ref, kc_ref, kp_ref, vc_ref, vp_ref, sink_ref, o_ref):
    n = pl.program_id(1)
    rows = GROUP * WINDOW
    i = lax.broadcasted_iota(jnp.int32, (rows, 2 * WINDOW), 0) % WINDOW
    j = lax.broadcasted_iota(jnp.int32, (rows, 2 * WINDOW), 1)
    diff = WINDOW + i - j
    mask = (diff >= 0) & (diff < WINDOW) & ((n > 0) | (j >= WINDOW))
    q = q_ref[...].astype(BF16)
    kc = kc_ref[...].astype(BF16)
    kp = kp_ref[...].astype(BF16)
    vc = vc_ref[...].astype(BF16)
    vp = vp_ref[...].astype(BF16)
    outs = []
    for kh in range(KV_HEADS):
        ksl = slice(kh * HEAD_DIM, (kh + 1) * HEAD_DIM)
        kk = jnp.concatenate([kp[:, ksl], kc[:, ksl]], axis=0)
        vv = jnp.concatenate([vp[:, ksl], vc[:, ksl]], axis=0)
        heads = [kh * GROUP + g for g in range(GROUP)]
        qh = jnp.concatenate([q[:, h * HEAD_DIM:(h + 1) * HEAD_DIM] for h in heads], axis=0)
        s = lax.dot_general(qh, kk, (((1,), (1,)), ((), ())), preferred_element_type=F32)
        s = jnp.where(mask, s * (HEAD_DIM ** -0.5), -jnp.inf)
        sink = sink_ref[kh]
        m = jnp.maximum(jnp.max(s, axis=1, keepdims=True), sink)
        e = jnp.exp(s - m)
        denom = jnp.sum(e, axis=1, keepdims=True) + jnp.exp(sink - m)
        p = (e / denom).astype(BF16)
        o = _bdot(p, vv)
        outs.extend(o[g * WINDOW:(g + 1) * WINDOW, :] for g in range(GROUP))
    o_ref[...] = jnp.concatenate(outs, axis=1).astype(o_ref.dtype)


def _swa_prompt(qk_rot, proj, sinks, B, T):
    nb = T // WINDOW
    rows = GROUP * WINDOW
    kcol = Q_W // KV_W
    vcol = V_OFF // KV_W
    cur = lambda col: (lambda b, n: (b * nb + n, col))
    prev = lambda col: (lambda b, n: (b * nb + jnp.maximum(n - 1, 0), col))
    sink_rows = jnp.repeat(sinks.reshape(KV_HEADS, GROUP), WINDOW, axis=1).reshape(KV_HEADS, rows, 1)
    return pl.pallas_call(
        _swa_prompt_kernel,
        grid=(B, nb),
        in_specs=[pl.BlockSpec((WINDOW, Q_W), cur(0)),
                  pl.BlockSpec((WINDOW, KV_W), cur(kcol)),
                  pl.BlockSpec((WINDOW, KV_W), prev(kcol)),
                  pl.BlockSpec((WINDOW, KV_W), cur(vcol)),
                  pl.BlockSpec((WINDOW, KV_W), prev(vcol)),
                  pl.BlockSpec((KV_HEADS, rows, 1), lambda b, n: (0, 0, 0))],
        out_specs=pl.BlockSpec((WINDOW, Q_W), cur(0)),
        out_shape=jax.ShapeDtypeStruct((B * T, Q_W), BF16),
        compiler_params=_cparams(2),
        name="swa_prompt",
    )(qk_rot, qk_rot, qk_rot, proj, proj, sink_rows)


def _swa_sample_kernel(q_ref, kn_ref, vn_ref, ck_ref, cv_ref, sink_ref, o_ref, *, steps):
    rows = GROUP * steps
    t_row = lax.broadcasted_iota(jnp.int32, (rows, WINDOW), 0) % steps
    c_col = lax.broadcasted_iota(jnp.int32, (rows, WINDOW), 1)
    cache_mask = c_col > t_row
    t_col = lax.broadcasted_iota(jnp.int32, (rows, 1), 0) % steps
    scale = HEAD_DIM ** -0.5
    ck = ck_ref[...].astype(BF16)
    cv = cv_ref[...].astype(BF16)
    kn = kn_ref[...].astype(BF16).astype(F32)
    vn = vn_ref[...].astype(BF16).astype(F32)
    for kh in range(KV_HEADS):
        ksl = slice(kh * HEAD_DIM, (kh + 1) * HEAD_DIM)
        qh = q_ref[kh].astype(BF16)
        sc = lax.dot_general(qh, ck[:, ksl], (((1,), (1,)), ((), ())), preferred_element_type=F32)
        sc = jnp.where(cache_mask, sc * scale, -jnp.inf)
        qf = qh.astype(F32)
        sn = []
        for u in range(steps):
            d = jnp.sum(qf * kn[u:u + 1, ksl], axis=1, keepdims=True) * scale
            sn.append(jnp.where(t_col >= u, d, -jnp.inf))
        sink = sink_ref[kh]
        m = jnp.maximum(jnp.max(sc, axis=1, keepdims=True), sink)
        for d in sn:
            m = jnp.maximum(m, d)
        ec = jnp.exp(sc - m)
        en = [jnp.exp(d - m) for d in sn]
        denom = jnp.sum(ec, axis=1, keepdims=True) + jnp.exp(sink - m)
        for e in en:
            denom = denom + e
        o = _bdot((ec / denom).astype(BF16), cv[:, ksl])
        for u in range(steps):
            o = o + (en[u] / denom).astype(BF16).astype(F32) * vn[u:u + 1, ksl]
        o_ref[kh] = o


def _swa_sample(qk_rot, proj, cache_k, cache_v, sinks, B, T):
    rows = GROUP * T
    qg = qk_rot[:, :Q_W].reshape(B, T, KV_HEADS, GROUP, HEAD_DIM).transpose(0, 2, 3, 1, 4)
    qg = qg.reshape(B, KV_HEADS, rows, HEAD_DIM)
    kn = qk_rot[:, Q_W:].reshape(B, T, KV_W)
    vn = proj[:, V_OFF:V_OFF + KV_W].reshape(B, T, KV_W)
    sink_rows = jnp.repeat(sinks.reshape(KV_HEADS, GROUP), T, axis=1).reshape(KV_HEADS, rows, 1)
    o = pl.pallas_call(
        functools.partial(_swa_sample_kernel, steps=T),
        grid=(B,),
        in_specs=[pl.BlockSpec((None, KV_HEADS, rows, HEAD_DIM), lambda b: (b, 0, 0, 0)),
                  pl.BlockSpec((None, T, KV_W), lambda b: (b, 0, 0)),
                  pl.BlockSpec((None, T, KV_W), lambda b: (b, 0, 0)),
                  pl.BlockSpec((None, WINDOW, KV_W), lambda b: (b, 0, 0)),
                  pl.BlockSpec((None, WINDOW, KV_W), lambda b: (b, 0, 0)),
                  pl.BlockSpec((KV_HEADS, rows, 1), lambda b: (0, 0, 0))],
        out_specs=pl.BlockSpec((None, KV_HEADS, rows, HEAD_DIM), lambda b: (b, 0, 0, 0)),
        out_shape=jax.ShapeDtypeStruct((B, KV_HEADS, rows, HEAD_DIM), F32),
        compiler_params=_cparams(1),
        name="swa_sample",
    )(qg, kn, vn, cache_k.reshape(B, WINDOW, KV_W), cache_v.reshape(B, WINDOW, KV_W), sink_rows)
    o = o.reshape(B, KV_HEADS, GROUP, T, HEAD_DIM).transpose(0, 3, 1, 2, 4).reshape(B * T, Q_W)
    return o.astype(BF16)


def _rope_tables(pos):
    inv = 1.0 / (ROPE_THETA ** (jnp.arange(0, ROPE_DIM, 2, dtype=F32) / ROPE_DIM))
    ang = pos.astype(F32)[:, None] * inv[None, :]
    cos, sin = jnp.cos(ang), jnp.sin(ang)
    n = pos.shape[0]
    ones = jnp.ones((n, HEAD_DIM - ROPE_DIM), F32)
    zeros = jnp.zeros((n, HEAD_DIM - ROPE_DIM), F32)
    zh = jnp.zeros((n, ROPE_HALF), F32)
    c = jnp.concatenate([cos, cos, ones], axis=1)
    s1 = jnp.concatenate([-sin, zh, zeros], axis=1)
    s2 = jnp.concatenate([zh, sin, zeros], axis=1)
    return tuple(jnp.tile(a, (1, SEG // HEAD_DIM)) for a in (c, s1, s2))


def _layer_weights(l, norm_g, w_mod, b_mod, ffn_w_gate, ffn_w_up, ffn_w_down, w_in, rwkv_mu, rwkv_w0, rwkv_w2,
                   rwkv_a0, rwkv_a2, rwkv_g2, rwkv_k_k, rwkv_k_a, rwkv_r_k, rwkv_ln_w, rwkv_ln_b, attn_sinks,
                   w_branch_a, w_branch_b, w_merge_gate, w_out):
    wi = w_in[l]
    zc = lambda n: jnp.zeros((D_MODEL, n), wi.dtype)
    w_proj = jnp.concatenate([wi[:, :A_PROJ], zc(A_PAD - A_PROJ), wi[:, A_PROJ:]], axis=1)
    lora_rows = lambda w, off: jnp.zeros((LORA_PAD, A_WIDTH), F32).at[off:off + w.shape[0]].set(w).astype(BF16)
    seg = jnp.arange(SEG) // A_HEAD_DIM
    row = lambda a: a.reshape(1, -1)
    return {
        "norm_g": norm_g[l], "w_mod": w_mod[l].astype(BF16), "b_mod": b_mod[l],
        "wg": ffn_w_gate[l].astype(BF16), "wu": ffn_w_up[l].astype(BF16), "wd": ffn_w_down[l].astype(BF16),
        "w_proj": w_proj.astype(BF16),
        "mu": jnp.pad(rwkv_mu[l], (0, A_PAD - A_PROJ)).reshape(1, A_PAD),
        "w0": row(rwkv_w0[l]), "a0": row(rwkv_a0[l]), "k_k": row(rwkv_k_k[l]), "k_a": row(rwkv_k_a[l]),
        "r_k": row(rwkv_r_k[l]), "ln_w": row(rwkv_ln_w[l]), "ln_b": row(rwkv_ln_b[l]),
        "w2": lora_rows(rwkv_w2[l], 0), "a2": lora_rows(rwkv_a2[l], DECAY_LORA),
        "g2": lora_rows(rwkv_g2[l], DECAY_LORA + ICLR_LORA),
        "bsel": (seg[:, None] == seg[None, :]).astype(BF16),
        "sinks": attn_sinks[l],
        "w_a": w_branch_a[l].astype(BF16), "w_b": w_branch_b[l].astype(BF16),
        "w_gate": w_merge_gate[l].astype(BF16), "w_out": w_out[l].astype(BF16),
    }


def _decoder_layer(x, mod, lw, B, T, shift_prev, wkv_prev, cache_k, cache_v, tables):
    M = B * T
    tm = min(M, 1024) if T >= 1024 else M
    tm_k = min(tm, 512)
    tm_prep = min(tm, 256)
    per_seq = T >= tm
    if per_seq:
        modv = lambda n: mod[:, n:n + 1, :]
        gmap = lambda tile: (lambda i: i // (T // tile))
        sp = jnp.pad(shift_prev, ((0, 0), (0, A_PAD - A_PROJ))).reshape(B, 1, A_PAD)
    else:
        modv = lambda n: jnp.repeat(mod[:, n, :], T, axis=0).reshape(1, M, D_MODEL)
        gmap = lambda tile: (lambda i: i)
        sp = jnp.repeat(jnp.pad(shift_prev, ((0, 0), (0, A_PAD - A_PROJ))), T, axis=0)
        sp = sp.reshape(M // tm_prep, tm_prep, A_PAD)

    def ffn(x, n, which):
        h = _ada_norm(x, lw["norm_g"][n], modv(3 * n), modv(3 * n + 1), gmap(tm), tm)
        act = _ffn_gate_up(h, lw["wg"][which], lw["wu"][which], tm)
        gate = modv(3 * n + 2)
        if not per_seq:
            gate = gate.reshape(M // tm_k, tm_k, D_MODEL)
        return _mm_residual(act, lw["wd"][which], x, gate, gmap(tm_k), 0.5, tm_k)

    x = ffn(x, 0, 0)

    h = _ada_norm(x, lw["norm_g"][1], modv(3), modv(4), gmap(tm), tm)
    proj = _mm_plain(h, lw["w_proj"], tm, name="in_proj")
    gates = _mm_plain(h, lw["w_gate"], tm, sigmoid=True, name="merge_gates")

    ops = _rwkv_prep(proj, sp, lw, T, tm_prep)
    r, dec, k2, v, kk, kka, g, bonus = ops
    o, new_wkv = _rwkv_scan((r, dec, k2, v, kk, kka), wkv_prev, B, T)
    ya = _rwkv_post(o, bonus, g, lw, min(tm, 512))
    new_shift = proj.reshape(B, T, PROJ_W)[:, -1, :A_PROJ]

    qk_rot = _rope(proj, tables, min(tm, 512))
    v_new = proj[:, V_OFF:V_OFF + KV_W].reshape(B, T, KV_HEADS, HEAD_DIM)
    k_new = qk_rot[:, Q_W:].reshape(B, T, KV_HEADS, HEAD_DIM)
    if cache_k is None:
        yb = _swa_prompt(qk_rot, proj, lw["sinks"], B, T)
        new_k, new_v = k_new[:, -WINDOW:], v_new[:, -WINDOW:]
    else:
        yb = _swa_sample(qk_rot, proj, cache_k, cache_v, lw["sinks"], B, T)
        new_k = jnp.concatenate([cache_k, k_new], axis=1)[:, -WINDOW:]
        new_v = jnp.concatenate([cache_v, v_new], axis=1)[:, -WINDOW:]

    merged = _mm_merge(ya, yb, lw["w_a"], lw["w_b"], gates, tm)
    gate = modv(5)
    x = _mm_residual(merged, lw["w_out"], x, gate, gmap(tm), 1.0, tm)

    x = ffn(x, 2, 1)
    return x, new_shift, new_wkv, new_k, new_v


def kernel(x_prompt, x_sample, c_prompt, c_sample, state_shift, state_wkv, cache_k, cache_v, norm_g, w_mod, b_mod,
           ffn_w_gate, ffn_w_up, ffn_w_down, w_in, rwkv_mu, rwkv_w0, rwkv_w2, rwkv_a0, rwkv_a2, rwkv_g2, rwkv_k_k,
           rwkv_k_a, rwkv_r_k, rwkv_ln_w, rwkv_ln_b, attn_sinks, w_branch_a, w_branch_b, w_merge_gate, w_out,
           final_norm):
    Bp, Tp, _ = x_prompt.shape
    Bs, Ts, _ = x_sample.shape
    tab_p = _rope_tables(jnp.arange(Tp))
    tab_s = _rope_tables(jnp.tile(PAST_LEN + jnp.arange(Ts), Bs))
    zero_shift = jnp.zeros((Bp, A_PROJ), F32)
    zero_wkv = jnp.zeros((Bp, A_HEADS, A_HEAD_DIM, A_HEAD_DIM), F32)
    xp = x_prompt.reshape(Bp * Tp, D_MODEL)
    xs = x_sample.reshape(Bs * Ts, D_MODEL)
    c_all = jnp.concatenate([c_prompt, c_sample], axis=0)
    pad_rows = (-c_all.shape[0]) % 16
    c_all = jnp.pad(c_all, ((0, pad_rows), (0, 0)))
    outs_p, outs_s = [], []
    for l in range(DEPTH):
        lw = _layer_weights(l, norm_g, w_mod, b_mod, ffn_w_gate, ffn_w_up, ffn_w_down, w_in, rwkv_mu, rwkv_w0,
                            rwkv_w2, rwkv_a0, rwkv_a2, rwkv_g2, rwkv_k_k, rwkv_k_a, rwkv_r_k, rwkv_ln_w, rwkv_ln_b,
                            attn_sinks, w_branch_a, w_branch_b, w_merge_gate, w_out)
        mod = _modulation(c_all, lw["w_mod"], lw["b_mod"]).reshape(-1, N_MOD, D_MODEL)
        xp, *st_p = _decoder_layer(xp, mod[:Bp], lw, Bp, Tp, zero_shift, zero_wkv, None, None, tab_p)
        xs, *st_s = _decoder_layer(xs, mod[Bp:Bp + Bs], lw, Bs, Ts, state_shift[l], state_wkv[l],
                                   cache_k[l], cache_v[l], tab_s)
        outs_p.append(st_p)
        outs_s.append(st_s)
    y_prompt = _final_norm(xp, final_norm, 1024).reshape(Bp, Tp, D_MODEL)
    y_sample = _final_norm(xs, final_norm, Bs * Ts).reshape(Bs, Ts, D_MODEL)
    stack = lambda outs, n: jnp.stack([o[n] for o in outs])
    return (y_prompt, y_sample,
            stack(outs_p, 0), stack(outs_p, 1), stack(outs_p, 2), stack(outs_p, 3),
            stack(outs_s, 0), stack(outs_s, 1), stack(outs_s, 2), stack(outs_s, 3))
```

```python
import functools

import jax
import jax.numpy as jnp
from jax import lax
from jax.experimental import pallas as pl
from jax.experimental.pallas import tpu as pltpu

D_MODEL = 2048
DEPTH = 2
A_HEADS = 16
A_HEAD_DIM = 64
A_WIDTH = A_HEADS * A_HEAD_DIM
DECAY_LORA = 64
ICLR_LORA = 64
GATE_LORA = 160
LORA_W = DECAY_LORA + ICLR_LORA + GATE_LORA
A_PROJ = 3 * A_WIDTH + LORA_W
GN_EPS = 64e-5
Q_HEADS = 16
KV_HEADS = 4
GROUP = Q_HEADS // KV_HEADS
HEAD_DIM = 64
Q_W = Q_HEADS * HEAD_DIM
KV_W = KV_HEADS * HEAD_DIM
WINDOW = 128
ROPE_DIM = HEAD_DIM // 4
ROPE_HALF = ROPE_DIM // 2
ROPE_THETA = 500000.0
D_FF = 5632
N_MOD = 9
RMS_EPS = 1e-6

PAST_LEN = 8192
LANE = 128
SUBLANE = 8
CHUNK = 64
SOLVE_BASE = 8
SEG = 256
A_PAD = 3584
LORA_PAD = A_PAD - 3 * A_WIDTH
Q_OFF = A_PAD
K_OFF = Q_OFF + Q_W
V_OFF = K_OFF + KV_W
PROJ_W = V_OFF + KV_W
QK_W = Q_W + KV_W
VMEM_LIMIT = 56 * 1024 * 1024

F32 = jnp.float32
BF16 = jnp.bfloat16


def _cparams(n_axes):
    return pltpu.CompilerParams(dimension_semantics=("arbitrary",) * n_axes,
                                vmem_limit_bytes=VMEM_LIMIT)


def _bdot(a, b):
    return jnp.dot(a, b, preferred_element_type=F32)


def _seg_sum(x, bsel):
    hi = x.astype(BF16)
    lo = (x - hi.astype(F32)).astype(BF16)
    parts = []
    for c in range(x.shape[1] // SEG):
        sl = slice(c * SEG, (c + 1) * SEG)
        parts.append(_bdot(hi[:, sl], bsel) + _bdot(lo[:, sl], bsel))
    return jnp.concatenate(parts, axis=1)


def _mod_kernel(c_ref, w_ref, b_ref, o_ref):
    c = c_ref[...]
    a = (c * jax.nn.sigmoid(c)).astype(BF16)
    o_ref[...] = _bdot(a, w_ref[...].astype(BF16)) + b_ref[...]


def _modulation(c, w_mod, b_mod, l):
    R = c.shape[0]
    N = w_mod.shape[-1]
    tn = 1024
    return pl.pallas_call(
        _mod_kernel,
        grid=(N // tn,),
        in_specs=[pl.BlockSpec((R, D_MODEL), lambda j: (0, 0)),
                  pl.BlockSpec((None, D_MODEL, tn), lambda j: (l, 0, j)),
                  pl.BlockSpec((None, 1, tn), lambda j: (l, 0, j))],
        out_specs=pl.BlockSpec((R, tn), lambda j: (0, j)),
        out_shape=jax.ShapeDtypeStruct((R, N), F32),
        compiler_params=_cparams(1),
        name="modulation",
    )(c, w_mod, b_mod.reshape(b_mod.shape[0], 1, N))


def _ada_norm_kernel(x_ref, g_ref, sh_ref, sc_ref, o_ref):
    x = x_ref[...]
    y = x * lax.rsqrt(jnp.mean(x * x, axis=-1, keepdims=True) + RMS_EPS)
    y = y * g_ref[...]
    o_ref[...] = (y * (1.0 + sc_ref[...]) + sh_ref[...]).astype(o_ref.dtype)


def _ada_norm(x, gain, shift, scale, gmap, tm):
    M = x.shape[0]
    R = shift.shape[1]
    mspec = pl.BlockSpec((None, R, D_MODEL), lambda i: (gmap(i), 0, 0))
    return pl.pallas_call(
        _ada_norm_kernel,
        grid=(M // tm,),
        in_specs=[pl.BlockSpec((tm, D_MODEL), lambda i: (i, 0)),
                  pl.BlockSpec((1, D_MODEL), lambda i: (0, 0)),
                  mspec, mspec],
        out_specs=pl.BlockSpec((tm, D_MODEL), lambda i: (i, 0)),
        out_shape=jax.ShapeDtypeStruct((M, D_MODEL), BF16),
        compiler_params=_cparams(1),
        name="ada_norm",
    )(x, gain.reshape(1, D_MODEL), shift, scale)


def _final_norm_kernel(x_ref, g_ref, o_ref):
    x = x_ref[...]
    y = x * lax.rsqrt(jnp.mean(x * x, axis=-1, keepdims=True) + RMS_EPS)
    o_ref[...] = y * g_ref[...]


def _final_norm(x, gain, tm):
    M = x.shape[0]
    return pl.pallas_call(
        _final_norm_kernel,
        grid=(M // tm,),
        in_specs=[pl.BlockSpec((tm, D_MODEL), lambda i: (i, 0)),
                  pl.BlockSpec((1, D_MODEL), lambda i: (0, 0))],
        out_specs=pl.BlockSpec((tm, D_MODEL), lambda i: (i, 0)),
        out_shape=jax.ShapeDtypeStruct((M, D_MODEL), F32),
        compiler_params=_cparams(1),
        name="final_norm",
    )(x, gain.reshape(1, D_MODEL))


def _wspec(w, lead, tn):
    return pl.BlockSpec((None,) * len(lead) + (w.shape[-2], tn), lambda j, i: lead + (0, j))


def _wdot(a, w_ref):
    return _bdot(a, w_ref[...].astype(BF16))


def _gu_kernel(h_ref, wg_ref, wu_ref, o_ref):
    h = h_ref[...]
    g = _wdot(h, wg_ref)
    u = _wdot(h, wu_ref)
    o_ref[...] = ((g * jax.nn.sigmoid(g)) * u).astype(o_ref.dtype)


def _ffn_gate_up(h, wg, wu, lead, tm, tn=512):
    M, K = h.shape
    N = wg.shape[-1]
    return pl.pallas_call(
        _gu_kernel,
        grid=(N // tn, M // tm),
        in_specs=[pl.BlockSpec((tm, K), lambda j, i: (i, 0)), _wspec(wg, lead, tn), _wspec(wu, lead, tn)],
        out_specs=pl.BlockSpec((tm, tn), lambda j, i: (i, j)),
        out_shape=jax.ShapeDtypeStruct((M, N), BF16),
        compiler_params=_cparams(2),
        name="ffn_gate_up",
    )(h, wg, wu)


def _resid_kernel(a_ref, w_ref, x_ref, gate_ref, o_ref, *, coef):
    y = _wdot(a_ref[...], w_ref)
    o_ref[...] = x_ref[...] + (coef * gate_ref[...]) * y


def _mm_residual(a, w, lead, x, gate, gmap, coef, tm, tn=512):
    M, K = a.shape
    N = w.shape[-1]
    R = gate.shape[1]
    return pl.pallas_call(
        functools.partial(_resid_kernel, coef=coef),
        grid=(N // tn, M // tm),
        in_specs=[pl.BlockSpec((tm, K), lambda j, i: (i, 0)),
                  _wspec(w, lead, tn),
                  pl.BlockSpec((tm, tn), lambda j, i: (i, j)),
                  pl.BlockSpec((None, R, tn), lambda j, i: (gmap(i), 0, j))],
        out_specs=pl.BlockSpec((tm, tn), lambda j, i: (i, j)),
        out_shape=jax.ShapeDtypeStruct((M, N), F32),
        compiler_params=_cparams(2),
        name="mm_residual",
    )(a, w, x, gate)


def _plain_kernel(a_ref, w_ref, o_ref, *, sigmoid):
    y = _wdot(a_ref[...], w_ref)
    if sigmoid:
        y = jax.nn.sigmoid(y)
    o_ref[...] = y.astype(o_ref.dtype)


def _mm_plain(a, w, lead, tm, tn=512, sigmoid=False, name="mm_plain"):
    M, K = a.shape
    N = w.shape[-1]
    return pl.pallas_call(
        functools.partial(_plain_kernel, sigmoid=sigmoid),
        grid=(N // tn, M // tm),
        in_specs=[pl.BlockSpec((tm, K), lambda j, i: (i, 0)), _wspec(w, lead, tn)],
        out_specs=pl.BlockSpec((tm, tn), lambda j, i: (i, j)),
        out_shape=jax.ShapeDtypeStruct((M, N), F32),
        compiler_params=_cparams(2),
        name=name,
    )(a, w)


def _merge_kernel(ya_ref, yb_ref, wa_ref, wb_ref, ga_ref, gb_ref, o_ref):
    pa = _wdot(ya_ref[...], wa_ref)
    pb = _wdot(yb_ref[...], wb_ref)
    o_ref[...] = (ga_ref[...] * pa + gb_ref[...] * pb).astype(o_ref.dtype)


def _mm_merge(ya, yb, wa, wb, lead, gates, tm, tn=512):
    M, K = ya.shape
    N = wa.shape[-1]
    nj = N // tn
    return pl.pallas_call(
        _merge_kernel,
        grid=(nj, M // tm),
        in_specs=[pl.BlockSpec((tm, K), lambda j, i: (i, 0)),
                  pl.BlockSpec((tm, K), lambda j, i: (i, 0)),
                  _wspec(wa, lead, tn), _wspec(wb, lead, tn),
                  pl.BlockSpec((tm, tn), lambda j, i: (i, j)),
                  pl.BlockSpec((tm, tn), lambda j, i: (i, j + nj))],
        out_specs=pl.BlockSpec((tm, tn), lambda j, i: (i, j)),
        out_shape=jax.ShapeDtypeStruct((M, N), BF16),
        compiler_params=_cparams(2),
        name="mm_merge",
    )(ya, yb, wa, wb, gates, gates)


def _rwkv_prep_kernel(p_ref, sp_ref, mu_ref, w0_ref, a0_ref, kk_ref, ka_ref, rk_ref,
                      w2_ref, a2_ref, g2_ref, bsel_ref,
                      r_out, d_out, k_out, v_out, kk_out, kka_out, g_out, bonus_out,
                      carry_ref, *, seq_len):
    t = pl.program_id(0)
    pa = p_ref[...]
    tm = pa.shape[0]
    row = lax.broadcasted_iota(jnp.int32, pa.shape, 0)
    rolled = pltpu.roll(pa, 1, axis=0)
    if seq_len >= tm:
        @pl.when(t % (seq_len // tm) == 0)
        def _():
            carry_ref[...] = sp_ref[...]

        prev = jnp.where(row == 0, carry_ref[...], rolled)
        carry_ref[...] = pa[tm - 1:tm, :]
    else:
        prev = jnp.where(row % seq_len == 0, sp_ref[...], rolled)
    xm = pa + (prev - pa) * mu_ref[...]
    r = xm[:, 0:A_WIDTH]
    k = xm[:, A_WIDTH:2 * A_WIDTH]
    v = xm[:, 2 * A_WIDTH:3 * A_WIDTH]
    lora = xm[:, 3 * A_WIDTH:A_PAD]
    m_w = _bdot(jnp.tanh(lora).astype(BF16), w2_ref[...])
    m_a = _bdot(lora.astype(BF16), a2_ref[...])
    g = _bdot(jax.nn.sigmoid(lora).astype(BF16), g2_ref[...])
    z = -(w0_ref[...] + m_w)
    softplus = jnp.maximum(z, 0.0) + jnp.log1p(jnp.exp(-jnp.abs(z)))
    w_log = -softplus - 0.5
    a = jax.nn.sigmoid(a0_ref[...] + m_a)
    bsel = bsel_ref[...]
    kkf = k * kk_ref[...]
    norm = jnp.sqrt(_seg_sum(kkf * kkf, bsel))
    kk = kkf / jnp.maximum(norm, 1e-12)
    k2 = k * (1.0 + (a - 1.0) * ka_ref[...])
    r_out[...] = r
    d_out[...] = -jnp.exp(w_log)
    k_out[...] = k2
    v_out[...] = v
    kk_out[...] = kk
    kka_out[...] = kk * a
    g_out[...] = g
    bonus_out[...] = _seg_sum(r * k2 * rk_ref[...], bsel) * v


def _rwkv_prep(proj, sp, lw, seq_len, tm):
    M = proj.shape[0]
    R = sp.shape[1]
    if seq_len >= tm:
        sp_map = lambda i: (i // (seq_len // tm), 0, 0)
    else:
        sp_map = lambda i: (i, 0, 0)
    row = lambda n: pl.BlockSpec((1, n), lambda i: (0, 0))
    full = lambda a, b: pl.BlockSpec((a, b), lambda i: (0, 0))
    out_spec = pl.BlockSpec((tm, A_WIDTH), lambda i: (i, 0))
    out_shape = jax.ShapeDtypeStruct((M, A_WIDTH), F32)
    return pl.pallas_call(
        functools.partial(_rwkv_prep_kernel, seq_len=seq_len),
        grid=(M // tm,),
        in_specs=[pl.BlockSpec((tm, A_PAD), lambda i: (i, 0)),
                  pl.BlockSpec((None, R, A_PAD), sp_map),
                  row(A_PAD), row(A_WIDTH), row(A_WIDTH), row(A_WIDTH), row(A_WIDTH), row(A_WIDTH),
                  full(LORA_PAD, A_WIDTH), full(LORA_PAD, A_WIDTH), full(LORA_PAD, A_WIDTH),
                  full(SEG, SEG)],
        out_specs=[out_spec] * 8,
        out_shape=[out_shape] * 8,
        scratch_shapes=[pltpu.VMEM((1, A_PAD), F32)],
        compiler_params=_cparams(1),
        name="rwkv_prep",
    )(proj, sp, lw["mu"], lw["w0"], lw["a0"], lw["k_k"], lw["k_a"], lw["r_k"],
      lw["w2"], lw["a2"], lw["g2"], lw["bsel"])


def _rwkv_scan_kernel(r_ref, d_ref, k_ref, v_ref, kk_ref, kka_ref, s0_ref, o_ref, s_out_ref, s_ref, *, steps):
    c = pl.program_id(1)
    n_pairs = A_HEADS // 2
    lane = lax.broadcasted_iota(jnp.int32, (A_HEAD_DIM, LANE), 1)
    left = lane < A_HEAD_DIM

    @pl.when(c == 0)
    def _():
        for p in range(n_pairs):
            s_ref[p] = jnp.concatenate([s0_ref[2 * p], s0_ref[2 * p + 1]], axis=1)

    def half_sums(x):
        sa = jnp.sum(jnp.where(left, x, 0.0), axis=1, keepdims=True)
        sb = jnp.sum(jnp.where(left, 0.0, x), axis=1, keepdims=True)
        return sa, sb

    def run_rows(rows, n_rows):
        for p in range(n_pairs):
            sl = slice(p * LANE, (p + 1) * LANE)
            rb, wb, kb, vb, kkb, kkab = (ref[rows, sl] for ref in (r_ref, d_ref, k_ref, v_ref, kk_ref, kka_ref))
            db = jnp.exp(wb)
            s = s_ref[p]
            o_rows = []
            for u in range(n_rows):
                row = slice(u, u + 1)
                sa, sb = half_sums(s * kkb[row])
                sk = jnp.where(left, sa, sb)
                vt = jnp.broadcast_to(vb[row], (LANE, LANE)).T
                vsel = jnp.where(left, vt[0:A_HEAD_DIM, :], vt[A_HEAD_DIM:LANE, :])
                s = s * db[row] - sk * kkab[row] + vsel * kb[row]
                oa, ob = half_sums(s * rb[row])
                ocol = jnp.concatenate([jnp.broadcast_to(oa, (A_HEAD_DIM, LANE)),
                                        jnp.broadcast_to(ob, (A_HEAD_DIM, LANE))], axis=0)
                o_rows.append(ocol.T[0:1, :])
            s_ref[p] = s
            o_ref[rows, sl] = jnp.concatenate(o_rows, axis=0)

    if steps % SUBLANE == 0:
        def group(i, carry):
            run_rows(pl.ds(pl.multiple_of(i * SUBLANE, SUBLANE), SUBLANE), SUBLANE)
            return carry

        lax.fori_loop(0, steps // SUBLANE, group, 0)
    else:
        run_rows(slice(0, steps), steps)

    @pl.when(c == pl.num_programs(1) - 1)
    def _():
        for p in range(n_pairs):
            s = s_ref[p]
            s_out_ref[2 * p] = s[:, 0:A_HEAD_DIM]
            s_out_ref[2 * p + 1] = s[:, A_HEAD_DIM:LANE]


def _rwkv_chunk_kernel(r_ref, w_ref, k_ref, v_ref, kk_ref, kka_ref, s0_ref, o_ref, s_out_ref, s_ref):
    c = pl.program_id(1)
    L = CHUNK
    P = 2 * L
    n_pairs = A_HEADS // 2
    left = lax.broadcasted_iota(jnp.int32, (L, LANE), 1) < A_HEAD_DIM
    row = lax.broadcasted_iota(jnp.int32, (P, P), 0)
    col = lax.broadcasted_iota(jnp.int32, (P, P), 1)
    strict = row > col
    lower = row >= col
    eye = (row == col).astype(F32)
    same = lambda n: (row // n) == (col // n)
    base = same(SOLVE_BASE)
    levels = []
    n = SOLVE_BASE
    while n < L:
        levels.append(same(2 * n) & jnp.logical_not(same(n)))
        n *= 2
    bf = lambda x: x.astype(BF16)
    nt = lambda a, b: lax.dot_general(a, b, (((1,), (1,)), ((), ())), preferred_element_type=F32)
    tn = lambda a, b: lax.dot_general(a, b, (((0,), (0,)), ((), ())), preferred_element_type=F32)

    @pl.when(c == 0)
    def _():
        z = jnp.zeros((A_HEAD_DIM, A_HEAD_DIM), F32)
        for p in range(n_pairs):
            s_ref[p] = jnp.concatenate([jnp.concatenate([s0_ref[2 * p], z], axis=1),
                                        jnp.concatenate([z, s0_ref[2 * p + 1]], axis=1)], axis=0)

    w = w_ref[...]
    tl = lax.broadcasted_iota(jnp.int32, (L, L), 0) >= lax.broadcasted_iota(jnp.int32, (L, L), 1)
    tri = tl.astype(BF16)
    w_hi = bf(w)
    w_r1 = w - w_hi.astype(F32)
    w_mid = bf(w_r1)
    w_lo = bf(w_r1 - w_mid.astype(F32))
    cum = _bdot(tri, w_hi) + _bdot(tri, w_mid) + _bdot(tri, w_lo)
    g_in = jnp.exp(cum)
    g_inv = jnp.exp(-cum)
    g_last = g_in[L - 1:L, :]
    at = -kk_ref[...] * jnp.exp(cum - w)
    bt = kka_ref[...] * g_inv
    kt = k_ref[...] * g_inv
    rt = r_ref[...] * g_in
    bl = bt * g_last
    kl = kt * g_last
    v = v_ref[...]

    pairs = range(n_pairs)
    lanes = [slice(p * LANE, (p + 1) * LANE) for p in pairs]

    def stack(x, p):
        xp = x[:, lanes[p]]
        return bf(jnp.concatenate([jnp.where(left, xp, 0.0), jnp.where(left, 0.0, xp)], axis=0))

    a2 = [stack(at, p) for p in pairs]
    r2 = [stack(rt, p) for p in pairs]
    b2 = [stack(bt, p) for p in pairs]
    k2 = [stack(kt, p) for p in pairs]
    v2 = [stack(v, p) for p in pairs]
    sc = [nt(jnp.concatenate([a2[p], r2[p]], axis=0), jnp.concatenate([b2[p], k2[p]], axis=0)) for p in pairs]
    a_ab = [jnp.where(strict, sc[p][0:P, 0:P], 0.0) for p in pairs]
    a_ak = [bf(jnp.where(strict, sc[p][0:P, P:2 * P], 0.0)) for p in pairs]
    a_rb = [bf(jnp.where(lower, sc[p][P:2 * P, 0:P], 0.0)) for p in pairs]
    a_rk = [bf(jnp.where(lower, sc[p][P:2 * P, P:2 * P], 0.0)) for p in pairs]
    pw = [bf(jnp.where(base, a_ab[p], 0.0)) for p in pairs]
    t_inv = [eye + pw[p].astype(F32) for p in pairs]
    m = 2
    while m < SOLVE_BASE:
        pw = [bf(_bdot(pw[p], pw[p])) for p in pairs]
        t_inv = [t_inv[p] + _bdot(bf(t_inv[p]), pw[p]) for p in pairs]
        m *= 2
    for lv in levels:
        half = [bf(_bdot(bf(t_inv[p]), bf(jnp.where(lv, a_ab[p], 0.0)))) for p in pairs]
        t_inv = [t_inv[p] + _bdot(half[p], bf(t_inv[p])) for p in pairs]
    s = [s_ref[p] for p in pairs]
    s_b = [bf(s[p]) for p in pairs]
    wmat = [bf(nt(a2[p], s_b[p]) + _bdot(a_ak[p], v2[p])) for p in pairs]
    u = [bf(_bdot(bf(t_inv[p]), wmat[p])) for p in pairs]
    for p in pairs:
        o2 = nt(r2[p], s_b[p]) + _bdot(a_rb[p], u[p]) + _bdot(a_rk[p], v2[p])
        o_ref[:, lanes[p]] = o2[0:L, :] + o2[L:P, :]
    for p in pairs:
        upd = tn(jnp.concatenate([u[p], v2[p]], axis=0),
                 jnp.concatenate([stack(bl, p), stack(kl, p)], axis=0))
        s_ref[p] = s[p] * g_last[:, lanes[p]] + upd

    @pl.when(c == pl.num_programs(1) - 1)
    def _():
        for p in range(n_pairs):
            s = s_ref[p]
            s_out_ref[2 * p] = s[0:A_HEAD_DIM, 0:A_HEAD_DIM]
            s_out_ref[2 * p + 1] = s[A_HEAD_DIM:LANE, A_HEAD_DIM:LANE]


def _rwkv_scan_chunked(ops, s0, B, T):
    ops3 = [a.reshape(B, T, A_WIDTH) for a in ops]
    tspec = pl.BlockSpec((None, CHUNK, A_WIDTH), lambda b, c: (b, c, 0))
    sspec = pl.BlockSpec((None, A_HEADS, A_HEAD_DIM, A_HEAD_DIM), lambda b, c: (b, 0, 0, 0))
    o, s_t = pl.pallas_call(
        _rwkv_chunk_kernel,
        grid=(B, T // CHUNK),
        in_specs=[tspec] * 6 + [sspec],
        out_specs=[tspec, sspec],
        out_shape=[jax.ShapeDtypeStruct((B, T, A_WIDTH), F32),
                   jax.ShapeDtypeStruct((B, A_HEADS, A_HEAD_DIM, A_HEAD_DIM), F32)],
        scratch_shapes=[pltpu.VMEM((A_HEADS // 2, LANE, LANE), F32)],
        compiler_params=_cparams(2),
        name="rwkv_chunk",
    )(*ops3, s0)
    return o.reshape(B * T, A_WIDTH), s_t


def _rwkv_scan(ops, s0, B, T):
    if T % CHUNK == 0:
        return _rwkv_scan_chunked(ops, s0, B, T)
    tc = min(T, 256)
    ops3 = [a.reshape(B, T, A_WIDTH) for a in ops]
    tspec = pl.BlockSpec((None, tc, A_WIDTH), lambda b, c: (b, c, 0))
    sspec = pl.BlockSpec((None, A_HEADS, A_HEAD_DIM, A_HEAD_DIM), lambda b, c: (b, 0, 0, 0))
    o, s_t = pl.pallas_call(
        functools.partial(_rwkv_scan_kernel, steps=tc),
        grid=(B, T // tc),
        in_specs=[tspec] * 6 + [sspec],
        out_specs=[tspec, sspec],
        out_shape=[jax.ShapeDtypeStruct((B, T, A_WIDTH), F32),
                   jax.ShapeDtypeStruct((B, A_HEADS, A_HEAD_DIM, A_HEAD_DIM), F32)],
        scratch_shapes=[pltpu.VMEM((A_HEADS // 2, A_HEAD_DIM, LANE), F32)],
        compiler_params=_cparams(2),
        name="rwkv_scan",
    )(*ops3, s0)
    return o.reshape(B * T, A_WIDTH), s_t


def _rwkv_post_kernel(o_ref, bonus_ref, g_ref, lnw_ref, lnb_ref, bsel_ref, y_ref):
    o = o_ref[...]
    bsel = bsel_ref[...]
    inv_n = 1.0 / A_HEAD_DIM
    mean = _seg_sum(o, bsel) * inv_n
    c = o - mean
    var = _seg_sum(c * c, bsel) * inv_n
    y = c * lax.rsqrt(var + GN_EPS) * lnw_ref[...] + lnb_ref[...]
    y_ref[...] = ((y + bonus_ref[...]) * g_ref[...]).astype(BF16)


def _rwkv_post(o, bonus, g, lw, tm):
    M = o.shape[0]
    tile = pl.BlockSpec((tm, A_WIDTH), lambda i: (i, 0))
    row = pl.BlockSpec((1, A_WIDTH), lambda i: (0, 0))
    return pl.pallas_call(
        _rwkv_post_kernel,
        grid=(M // tm,),
        in_specs=[tile, tile, tile, row, row, pl.BlockSpec((SEG, SEG), lambda i: (0, 0))],
        out_specs=tile,
        out_shape=jax.ShapeDtypeStruct((M, A_WIDTH), BF16),
        compiler_params=_cparams(1),
        name="rwkv_post",
    )(o, bonus, g, lw["ln_w"], lw["ln_b"], lw["bsel"])


def _rope_kernel(x_ref, c_ref, s1_ref, s2_ref, o_ref):
    x = x_ref[...]
    fwd = pltpu.roll(x, x.shape[1] - ROPE_HALF, axis=1)
    bwd = pltpu.roll(x, ROPE_HALF, axis=1)
    o_ref[...] = x * c_ref[...] + fwd * s1_ref[...] + bwd * s2_ref[...]


def _rope(proj, tables, tm):
    M = proj.shape[0]
    n_tab = tables[0].shape[0] // tm
    tab = pl.BlockSpec((tm, SEG), lambda i, j: (i % n_tab, 0))
    return pl.pallas_call(
        _rope_kernel,
        grid=(M // tm, QK_W // SEG),
        in_specs=[pl.BlockSpec((tm, SEG), lambda i, j: (i, Q_OFF // SEG + j)), tab, tab, tab],
        out_specs=pl.BlockSpec((tm, SEG), lambda i, j: (i, j)),
        out_shape=jax.ShapeDtypeStruct((M, QK_W), F32),
        compiler_params=_cparams(2),
        name="rope",
    )(proj, *tables)


def _swa_prompt_kernel(q_ref, kc_ref, kp_ref, vc_ref, vp_ref, sink_ref, o_ref):
    n = pl.program_id(1)
    rows = GROUP * WINDOW
    i = lax.broadcasted_iota(jnp.int32, (rows, 2 * WINDOW), 0) % WINDOW
    j = lax.broadcasted_iota(jnp.int32, (rows, 2 * WINDOW), 1)
    diff = WINDOW + i - j
    mask = (diff >= 0) & (diff < WINDOW) & ((n > 0) | (j >= WINDOW))
    q = q_ref[...].astype(BF16)
    kc = kc_ref[...].astype(BF16)
    kp = kp_ref[...].astype(BF16)
    vc = vc_ref[...].astype(BF16)
    vp = vp_ref[...].astype(BF16)
    outs = []
    for kh in range(KV_HEADS):
        ksl = slice(kh * HEAD_DIM, (kh + 1) * HEAD_DIM)
        kk = jnp.concatenate([kp[:, ksl], kc[:, ksl]], axis=0)
        vv = jnp.concatenate([vp[:, ksl], vc[:, ksl]], axis=0)
        heads = [kh * GROUP + g for g in range(GROUP)]
        qh = jnp.concatenate([q[:, h * HEAD_DIM:(h + 1) * HEAD_DIM] for h in heads], axis=0)
        s = lax.dot_general(qh, kk, (((1,), (1,)), ((), ())), preferred_element_type=F32)
        s = jnp.where(mask, s * (HEAD_DIM ** -0.5), -jnp.inf)
        sink = sink_ref[kh]
        m = jnp.maximum(jnp.max(s, axis=1, keepdims=True), sink)
        e = jnp.exp(s - m)
        denom = jnp.sum(e, axis=1, keepdims=True) + jnp.exp(sink - m)
        p = (e / denom).astype(BF16)
        o = _bdot(p, vv)
        outs.extend(o[g * WINDOW:(g + 1) * WINDOW, :] for g in range(GROUP))
    o_ref[...] = jnp.concatenate(outs, axis=1).astype(o_ref.dtype)


def _swa_prompt(qk_rot, proj, sinks, B, T):
    nb = T // WINDOW
    rows = GROUP * WINDOW
    kcol = Q_W // KV_W
    vcol = V_OFF // KV_W
    cur = lambda col: (lambda b, n: (b * nb + n, col))
    prev = lambda col: (lambda b, n: (b * nb + jnp.maximum(n - 1, 0), col))
    sink_rows = jnp.repeat(sinks.reshape(KV_HEADS, GROUP), WINDOW, axis=1).reshape(KV_HEADS, rows, 1)
    return pl.pallas_call(
        _swa_prompt_kernel,
        grid=(B, nb),
        in_specs=[pl.BlockSpec((WINDOW, Q_W), cur(0)),
                  pl.BlockSpec((WINDOW, KV_W), cur(kcol)),
                  pl.BlockSpec((WINDOW, KV_W), prev(kcol)),
                  pl.BlockSpec((WINDOW, KV_W), cur(vcol)),
                  pl.BlockSpec((WINDOW, KV_W), prev(vcol)),
                  pl.BlockSpec((KV_HEADS, rows, 1), lambda b, n: (0, 0, 0))],
        out_specs=pl.BlockSpec((WINDOW, Q_W), cur(0)),
        out_shape=jax.ShapeDtypeStruct((B * T, Q_W), BF16),
        compiler_params=_cparams(2),
        name="swa_prompt",
    )(qk_rot, qk_rot, qk_rot, proj, proj, sink_rows)


def _swa_sample_kernel(q_ref, kn_ref, vn_ref, ck_ref, cv_ref, sink_ref, o_ref, *, steps):
    rows = GROUP * steps
    t_row = lax.broadcasted_iota(jnp.int32, (rows, WINDOW), 0) % steps
    c_col = lax.broadcasted_iota(jnp.int32, (rows, WINDOW), 1)
    cache_mask = c_col > t_row
    t_col = lax.broadcasted_iota(jnp.int32, (rows, 1), 0) % steps
    scale = HEAD_DIM ** -0.5
    probs = []
    units = [(b, kh) for b in range(q_ref.shape[0]) for kh in range(KV_HEADS)]
    ksl = [slice(kh * HEAD_DIM, (kh + 1) * HEAD_DIM) for kh in range(KV_HEADS)]
    qh = {u: q_ref[u[0], u[1]].astype(BF16) for u in units}
    sc = {u: lax.dot_general(qh[u], ck_ref[u[0], :, ksl[u[1]]].astype(BF16), (((1,), (1,)), ((), ())),
                             preferred_element_type=F32) for u in units}
    s_c, s_n, m_c = {}, {}, {}
    for u in units:
        b, kh = u
        kn = kn_ref[b].astype(BF16).astype(F32)
        s_c[u] = jnp.where(cache_mask, sc[u] * scale, -jnp.inf)
        qf = qh[u].astype(F32)
        s_n[u] = [jnp.where(t_col >= t, jnp.sum(qf * kn[t:t + 1, ksl[kh]], axis=1, keepdims=True) * scale, -jnp.inf)
                  for t in range(steps)]
        m_c[u] = jnp.max(s_c[u], axis=1, keepdims=True)
    for u in units:
        sink = sink_ref[u[1]]
        m = jnp.maximum(m_c[u], sink)
        for d in s_n[u]:
            m = jnp.maximum(m, d)
        ec = jnp.exp(s_c[u] - m)
        en = [jnp.exp(d - m) for d in s_n[u]]
        denom = jnp.sum(ec, axis=1, keepdims=True) + jnp.exp(sink - m)
        for e in en:
            denom = denom + e
        probs.append(((ec / denom).astype(BF16), [(e / denom).astype(BF16).astype(F32) for e in en]))
    for u, (pc, pn) in zip(units, probs):
        b, kh = u
        vn = vn_ref[b].astype(BF16).astype(F32)
        o = _bdot(pc, cv_ref[b, :, ksl[kh]].astype(BF16))
        for t in range(steps):
            o = o + pn[t] * vn[t:t + 1, ksl[kh]]
        o_ref[b, kh] = o


def _swa_sample(qk_rot, proj, cache_k, cache_v, sinks, B, T):
    rows = GROUP * T
    qg = qk_rot[:, :Q_W].reshape(B, T, KV_HEADS, GROUP, HEAD_DIM).transpose(0, 2, 3, 1, 4)
    qg = qg.reshape(B, KV_HEADS, rows, HEAD_DIM)
    kn = qk_rot[:, Q_W:].reshape(B, T, KV_W)
    vn = proj[:, V_OFF:V_OFF + KV_W].reshape(B, T, KV_W)
    sink_rows = jnp.repeat(sinks.reshape(KV_HEADS, GROUP), T, axis=1).reshape(KV_HEADS, rows, 1)
    nb = SUBLANE if B % SUBLANE == 0 else 1
    o = pl.pallas_call(
        functools.partial(_swa_sample_kernel, steps=T),
        grid=(B // nb,),
        in_specs=[pl.BlockSpec((nb, KV_HEADS, rows, HEAD_DIM), lambda b: (b, 0, 0, 0)),
                  pl.BlockSpec((nb, T, KV_W), lambda b: (b, 0, 0)),
                  pl.BlockSpec((nb, T, KV_W), lambda b: (b, 0, 0)),
                  pl.BlockSpec((nb, WINDOW, KV_W), lambda b: (b, 0, 0)),
                  pl.BlockSpec((nb, WINDOW, KV_W), lambda b: (b, 0, 0)),
                  pl.BlockSpec((KV_HEADS, rows, 1), lambda b: (0, 0, 0))],
        out_specs=pl.BlockSpec((nb, KV_HEADS, rows, HEAD_DIM), lambda b: (b, 0, 0, 0)),
        out_shape=jax.ShapeDtypeStruct((B, KV_HEADS, rows, HEAD_DIM), F32),
        compiler_params=_cparams(1),
        name="swa_sample",
    )(qg, kn, vn, cache_k.reshape(B, WINDOW, KV_W), cache_v.reshape(B, WINDOW, KV_W), sink_rows)
    o = o.reshape(B, KV_HEADS, GROUP, T, HEAD_DIM).transpose(0, 3, 1, 2, 4).reshape(B * T, Q_W)
    return o.astype(BF16)


def _rope_tables(pos):
    inv = 1.0 / (ROPE_THETA ** (jnp.arange(0, ROPE_DIM, 2, dtype=F32) / ROPE_DIM))
    ang = pos.astype(F32)[:, None] * inv[None, :]
    cos, sin = jnp.cos(ang), jnp.sin(ang)
    n = pos.shape[0]
    ones = jnp.ones((n, HEAD_DIM - ROPE_DIM), F32)
    zeros = jnp.zeros((n, HEAD_DIM - ROPE_DIM), F32)
    zh = jnp.zeros((n, ROPE_HALF), F32)
    c = jnp.concatenate([cos, cos, ones], axis=1)
    s1 = jnp.concatenate([-sin, zh, zeros], axis=1)
    s2 = jnp.concatenate([zh, sin, zeros], axis=1)
    return tuple(jnp.tile(a, (1, SEG // HEAD_DIM)) for a in (c, s1, s2))


def _layer_weights(l, norm_g, w_mod, b_mod, ffn_w_gate, ffn_w_up, ffn_w_down, w_in, rwkv_mu, rwkv_w0, rwkv_w2,
                   rwkv_a0, rwkv_a2, rwkv_g2, rwkv_k_k, rwkv_k_a, rwkv_r_k, rwkv_ln_w, rwkv_ln_b, attn_sinks,
                   w_branch_a, w_branch_b, w_merge_gate, w_out):
    wi = w_in[l]
    zc = lambda n: jnp.zeros((D_MODEL, n), wi.dtype)
    w_proj = jnp.concatenate([wi[:, :A_PROJ], zc(A_PAD - A_PROJ), wi[:, A_PROJ:]], axis=1)
    lora_rows = lambda w, off: jnp.zeros((LORA_PAD, A_WIDTH), F32).at[off:off + w.shape[0]].set(w).astype(BF16)
    seg = jnp.arange(SEG) // A_HEAD_DIM
    row = lambda a: a.reshape(1, -1)
    return {
        "l": l, "norm_g": norm_g[l], "w_mod": w_mod, "b_mod": b_mod,
        "wg": ffn_w_gate, "wu": ffn_w_up, "wd": ffn_w_down,
        "w_proj": w_proj,
        "mu": jnp.pad(rwkv_mu[l], (0, A_PAD - A_PROJ)).reshape(1, A_PAD),
        "w0": row(rwkv_w0[l]), "a0": row(rwkv_a0[l]), "k_k": row(rwkv_k_k[l]), "k_a": row(rwkv_k_a[l]),
        "r_k": row(rwkv_r_k[l]), "ln_w": row(rwkv_ln_w[l]), "ln_b": row(rwkv_ln_b[l]),
        "w2": lora_rows(rwkv_w2[l], 0), "a2": lora_rows(rwkv_a2[l], DECAY_LORA),
        "g2": lora_rows(rwkv_g2[l], DECAY_LORA + ICLR_LORA),
        "bsel": (seg[:, None] == seg[None, :]).astype(BF16),
        "sinks": attn_sinks[l],
        "w_a": w_branch_a, "w_b": w_branch_b, "w_gate": w_merge_gate, "w_out": w_out,
    }


def _decoder_layer(x, mod, lw, B, T, shift_prev, wkv_prev, cache_k, cache_v, tables):
    M = B * T
    tm = min(M, 1024) if T >= 1024 else M
    tm_k = min(tm, 512)
    tm_prep = min(tm, 256)
    per_seq = T >= tm
    if per_seq:
        modv = lambda n: mod[:, n:n + 1, :]
        gmap = lambda tile: (lambda i: i // (T // tile))
        sp = jnp.pad(shift_prev, ((0, 0), (0, A_PAD - A_PROJ))).reshape(B, 1, A_PAD)
    else:
        modv = lambda n: jnp.repeat(mod[:, n, :], T, axis=0).reshape(1, M, D_MODEL)
        gmap = lambda tile: (lambda i: i)
        sp = jnp.repeat(jnp.pad(shift_prev, ((0, 0), (0, A_PAD - A_PROJ))), T, axis=0)
        sp = sp.reshape(M // tm_prep, tm_prep, A_PAD)

    l = lw["l"]

    def ffn(x, n, which):
        h = _ada_norm(x, lw["norm_g"][n], modv(3 * n), modv(3 * n + 1), gmap(tm), tm)
        act = _ffn_gate_up(h, lw["wg"], lw["wu"], (l, which), tm)
        gate = modv(3 * n + 2)
        if not per_seq:
            gate = gate.reshape(M // tm_k, tm_k, D_MODEL)
        return _mm_residual(act, lw["wd"], (l, which), x, gate, gmap(tm_k), 0.5, tm_k)

    x = ffn(x, 0, 0)

    h = _ada_norm(x, lw["norm_g"][1], modv(3), modv(4), gmap(tm), tm)
    proj = _mm_plain(h, lw["w_proj"], (), tm, name="in_proj")
    gates = _mm_plain(h, lw["w_gate"], (l,), tm, sigmoid=True, name="merge_gates")

    ops = _rwkv_prep(proj, sp, lw, T, tm_prep)
    r, dec, k2, v, kk, kka, g, bonus = ops
    o, new_wkv = _rwkv_scan((r, dec, k2, v, kk, kka), wkv_prev, B, T)
    ya = _rwkv_post(o, bonus, g, lw, min(tm, 512))
    new_shift = proj.reshape(B, T, PROJ_W)[:, -1, :A_PROJ]

    qk_rot = _rope(proj, tables, min(tm, 512))
    v_new = proj[:, V_OFF:V_OFF + KV_W].reshape(B, T, KV_HEADS, HEAD_DIM)
    k_new = qk_rot[:, Q_W:].reshape(B, T, KV_HEADS, HEAD_DIM)
    if cache_k is None:
        yb = _swa_prompt(qk_rot, proj, lw["sinks"], B, T)
        new_k, new_v = k_new[:, -WINDOW:], v_new[:, -WINDOW:]
    else:
        yb = _swa_sample(qk_rot, proj, cache_k, cache_v, lw["sinks"], B, T)
        new_k = jnp.concatenate([cache_k, k_new], axis=1)[:, -WINDOW:]
        new_v = jnp.concatenate([cache_v, v_new], axis=1)[:, -WINDOW:]

    merged = _mm_merge(ya, yb, lw["w_a"], lw["w_b"], (l,), gates, tm)
    gate = modv(5)
    x = _mm_residual(merged, lw["w_out"], (l,), x, gate, gmap(tm), 1.0, tm)

    x = ffn(x, 2, 1)
    return x, new_shift, new_wkv, new_k, new_v


def kernel(x_prompt, x_sample, c_prompt, c_sample, state_shift, state_wkv, cache_k, cache_v, norm_g, w_mod, b_mod,
           ffn_w_gate, ffn_w_up, ffn_w_down, w_in, rwkv_mu, rwkv_w0, rwkv_w2, rwkv_a0, rwkv_a2, rwkv_g2, rwkv_k_k,
           rwkv_k_a, rwkv_r_k, rwkv_ln_w, rwkv_ln_b, attn_sinks, w_branch_a, w_branch_b, w_merge_gate, w_out,
           final_norm):
    Bp, Tp, _ = x_prompt.shape
    Bs, Ts, _ = x_sample.shape
    tab_p = _rope_tables(jnp.arange(Tp))
    tab_s = _rope_tables(jnp.tile(PAST_LEN + jnp.arange(Ts), Bs))
    zero_shift = jnp.zeros((Bp, A_PROJ), F32)
    zero_wkv = jnp.zeros((Bp, A_HEADS, A_HEAD_DIM, A_HEAD_DIM), F32)
    xp = x_prompt.reshape(Bp * Tp, D_MODEL)
    xs = x_sample.reshape(Bs * Ts, D_MODEL)
    c_all = jnp.concatenate([c_prompt, c_sample], axis=0)
    pad_rows = (-c_all.shape[0]) % 16
    c_all = jnp.pad(c_all, ((0, pad_rows), (0, 0)))
    outs_p, outs_s = [], []
    for l in range(DEPTH):
        lw = _layer_weights(l, norm_g, w_mod, b_mod, ffn_w_gate, ffn_w_up, ffn_w_down, w_in, rwkv_mu, rwkv_w0,
                            rwkv_w2, rwkv_a0, rwkv_a2, rwkv_g2, rwkv_k_k, rwkv_k_a, rwkv_r_k, rwkv_ln_w, rwkv_ln_b,
                            attn_sinks, w_branch_a, w_branch_b, w_merge_gate, w_out)
        mod = _modulation(c_all, w_mod, b_mod, l).reshape(-1, N_MOD, D_MODEL)
        xp, *st_p = _decoder_layer(xp, mod[:Bp], lw, Bp, Tp, zero_shift, zero_wkv, None, None, tab_p)
        xs, *st_s = _decoder_layer(xs, mod[Bp:Bp + Bs], lw, Bs, Ts, state_shift[l], state_wkv[l],
                                   cache_k[l], cache_v[l], tab_s)
        outs_p.append(st_p)
        outs_s.append(st_s)
    y_prompt = _final_norm(xp, final_norm, 1024).reshape(Bp, Tp, D_MODEL)
    y_sample = _final_norm(xs, final_norm, Bs * Ts).reshape(Bs, Ts, D_MODEL)
    stack = lambda outs, n: jnp.stack([o[n] for o in outs])
    return (y_prompt, y_sample,
            stack(outs_p, 0), stack(outs_p, 1), stack(outs_p, 2), stack(outs_p, 3),
            stack(outs_s, 0), stack(outs_s, 1), stack(outs_s, 2), stack(outs_s, 3))
```

```python
import functools

import jax
import jax.numpy as jnp
from jax import lax
from jax.experimental import pallas as pl
from jax.experimental.pallas import tpu as pltpu

D_MODEL = 2048
DEPTH = 2
A_HEADS = 16
A_HEAD_DIM = 64
A_WIDTH = A_HEADS * A_HEAD_DIM
DECAY_LORA = 64
ICLR_LORA = 64
GATE_LORA = 160
LORA_W = DECAY_LORA + ICLR_LORA + GATE_LORA
A_PROJ = 3 * A_WIDTH + LORA_W
GN_EPS = 64e-5
Q_HEADS = 16
KV_HEADS = 4
GROUP = Q_HEADS // KV_HEADS
HEAD_DIM = 64
Q_W = Q_HEADS * HEAD_DIM
KV_W = KV_HEADS * HEAD_DIM
WINDOW = 128
ROPE_DIM = HEAD_DIM // 4
ROPE_HALF = ROPE_DIM // 2
ROPE_THETA = 500000.0
D_FF = 5632
N_MOD = 9
RMS_EPS = 1e-6

PAST_LEN = 8192
LANE = 128
SUBLANE = 8
CHUNK = 64
SOLVE_BASE = 8
SEG = 256
A_PAD = 3584
LORA_PAD = A_PAD - 3 * A_WIDTH
Q_OFF = A_PAD
K_OFF = Q_OFF + Q_W
V_OFF = K_OFF + KV_W
PROJ_W = V_OFF + KV_W
QK_W = Q_W + KV_W
VMEM_LIMIT = 56 * 1024 * 1024

F32 = jnp.float32
BF16 = jnp.bfloat16


def _cparams(n_axes):
    return pltpu.CompilerParams(dimension_semantics=("arbitrary",) * n_axes,
                                vmem_limit_bytes=VMEM_LIMIT)


def _bdot(a, b):
    return jnp.dot(a, b, preferred_element_type=F32)


def _seg_sum(x, bsel):
    hi = x.astype(BF16)
    lo = (x - hi.astype(F32)).astype(BF16)
    parts = []
    for c in range(x.shape[1] // SEG):
        sl = slice(c * SEG, (c + 1) * SEG)
        parts.append(_bdot(hi[:, sl], bsel) + _bdot(lo[:, sl], bsel))
    return jnp.concatenate(parts, axis=1)


def _mod_kernel(c_ref, w_ref, b_ref, o_ref):
    c = c_ref[...]
    a = (c * jax.nn.sigmoid(c)).astype(BF16)
    o_ref[...] = _bdot(a, w_ref[...].astype(BF16)) + b_ref[...]


def _modulation(c, w_mod, b_mod, l):
    R = c.shape[0]
    N = w_mod.shape[-1]
    tn = 1024
    return pl.pallas_call(
        _mod_kernel,
        grid=(N // tn,),
        in_specs=[pl.BlockSpec((R, D_MODEL), lambda j: (0, 0)),
                  pl.BlockSpec((None, D_MODEL, tn), lambda j: (l, 0, j)),
                  pl.BlockSpec((None, 1, tn), lambda j: (l, 0, j))],
        out_specs=pl.BlockSpec((R, tn), lambda j: (0, j)),
        out_shape=jax.ShapeDtypeStruct((R, N), F32),
        compiler_params=_cparams(1),
        name="modulation",
    )(c, w_mod, b_mod.reshape(b_mod.shape[0], 1, N))


def _ada_norm_kernel(x_ref, g_ref, sh_ref, sc_ref, o_ref):
    x = x_ref[...]
    y = x * lax.rsqrt(jnp.mean(x * x, axis=-1, keepdims=True) + RMS_EPS)
    y = y * g_ref[...]
    o_ref[...] = (y * (1.0 + sc_ref[...]) + sh_ref[...]).astype(o_ref.dtype)


def _ada_norm(x, gain, shift, scale, gmap, tm):
    M = x.shape[0]
    R = shift.shape[1]
    mspec = pl.BlockSpec((None, R, D_MODEL), lambda i: (gmap(i), 0, 0))
    return pl.pallas_call(
        _ada_norm_kernel,
        grid=(M // tm,),
        in_specs=[pl.BlockSpec((tm, D_MODEL), lambda i: (i, 0)),
                  pl.BlockSpec((1, D_MODEL), lambda i: (0, 0)),
                  mspec, mspec],
        out_specs=pl.BlockSpec((tm, D_MODEL), lambda i: (i, 0)),
        out_shape=jax.ShapeDtypeStruct((M, D_MODEL), BF16),
        compiler_params=_cparams(1),
        name="ada_norm",
    )(x, gain.reshape(1, D_MODEL), shift, scale)


def _final_norm_kernel(x_ref, g_ref, o_ref):
    x = x_ref[...]
    y = x * lax.rsqrt(jnp.mean(x * x, axis=-1, keepdims=True) + RMS_EPS)
    o_ref[...] = y * g_ref[...]


def _final_norm(x, gain, tm):
    M = x.shape[0]
    return pl.pallas_call(
        _final_norm_kernel,
        grid=(M // tm,),
        in_specs=[pl.BlockSpec((tm, D_MODEL), lambda i: (i, 0)),
                  pl.BlockSpec((1, D_MODEL), lambda i: (0, 0))],
        out_specs=pl.BlockSpec((tm, D_MODEL), lambda i: (i, 0)),
        out_shape=jax.ShapeDtypeStruct((M, D_MODEL), F32),
        compiler_params=_cparams(1),
        name="final_norm",
    )(x, gain.reshape(1, D_MODEL))


def _wspec(w, lead, tn):
    return pl.BlockSpec((None,) * len(lead) + (w.shape[-2], tn), lambda j, i: lead + (0, j))


def _wdot(a, w_ref):
    return _bdot(a, w_ref[...].astype(BF16))


def _gu_kernel(h_ref, wg_ref, wu_ref, o_ref):
    h = h_ref[...]
    g = _wdot(h, wg_ref)
    u = _wdot(h, wu_ref)
    o_ref[...] = ((g * jax.nn.sigmoid(g)) * u).astype(o_ref.dtype)


def _ffn_gate_up(h, wg, wu, lead, tm, tn=512):
    M, K = h.shape
    N = wg.shape[-1]
    return pl.pallas_call(
        _gu_kernel,
        grid=(N // tn, M // tm),
        in_specs=[pl.BlockSpec((tm, K), lambda j, i: (i, 0)), _wspec(wg, lead, tn), _wspec(wu, lead, tn)],
        out_specs=pl.BlockSpec((tm, tn), lambda j, i: (i, j)),
        out_shape=jax.ShapeDtypeStruct((M, N), BF16),
        compiler_params=_cparams(2),
        name="ffn_gate_up",
    )(h, wg, wu)


def _resid_kernel(a_ref, w_ref, x_ref, gate_ref, o_ref, *, coef):
    y = _wdot(a_ref[...], w_ref)
    o_ref[...] = x_ref[...] + (coef * gate_ref[...]) * y


def _mm_residual(a, w, lead, x, gate, gmap, coef, tm, tn=512):
    M, K = a.shape
    N = w.shape[-1]
    R = gate.shape[1]
    return pl.pallas_call(
        functools.partial(_resid_kernel, coef=coef),
        grid=(N // tn, M // tm),
        in_specs=[pl.BlockSpec((tm, K), lambda j, i: (i, 0)),
                  _wspec(w, lead, tn),
                  pl.BlockSpec((tm, tn), lambda j, i: (i, j)),
                  pl.BlockSpec((None, R, tn), lambda j, i: (gmap(i), 0, j))],
        out_specs=pl.BlockSpec((tm, tn), lambda j, i: (i, j)),
        out_shape=jax.ShapeDtypeStruct((M, N), F32),
        compiler_params=_cparams(2),
        name="mm_residual",
    )(a, w, x, gate)


def _plain_kernel(a_ref, w_ref, o_ref, *, sigmoid):
    y = _wdot(a_ref[...], w_ref)
    if sigmoid:
        y = jax.nn.sigmoid(y)
    o_ref[...] = y.astype(o_ref.dtype)


def _mm_plain(a, w, lead, tm, tn=512, sigmoid=False, name="mm_plain"):
    M, K = a.shape
    N = w.shape[-1]
    return pl.pallas_call(
        functools.partial(_plain_kernel, sigmoid=sigmoid),
        grid=(N // tn, M // tm),
        in_specs=[pl.BlockSpec((tm, K), lambda j, i: (i, 0)), _wspec(w, lead, tn)],
        out_specs=pl.BlockSpec((tm, tn), lambda j, i: (i, j)),
        out_shape=jax.ShapeDtypeStruct((M, N), F32),
        compiler_params=_cparams(2),
        name=name,
    )(a, w)


def _merge_kernel(ya_ref, yb_ref, wa_ref, wb_ref, ga_ref, gb_ref, o_ref):
    pa = _wdot(ya_ref[...], wa_ref)
    pb = _wdot(yb_ref[...], wb_ref)
    o_ref[...] = (ga_ref[...] * pa + gb_ref[...] * pb).astype(o_ref.dtype)


def _mm_merge(ya, yb, wa, wb, lead, gates, tm, tn=512):
    M, K = ya.shape
    N = wa.shape[-1]
    nj = N // tn
    return pl.pallas_call(
        _merge_kernel,
        grid=(nj, M // tm),
        in_specs=[pl.BlockSpec((tm, K), lambda j, i: (i, 0)),
                  pl.BlockSpec((tm, K), lambda j, i: (i, 0)),
                  _wspec(wa, lead, tn), _wspec(wb, lead, tn),
                  pl.BlockSpec((tm, tn), lambda j, i: (i, j)),
                  pl.BlockSpec((tm, tn), lambda j, i: (i, j + nj))],
        out_specs=pl.BlockSpec((tm, tn), lambda j, i: (i, j)),
        out_shape=jax.ShapeDtypeStruct((M, N), BF16),
        compiler_params=_cparams(2),
        name="mm_merge",
    )(ya, yb, wa, wb, gates, gates)


def _rwkv_prep_kernel(p_ref, sp_ref, mu_ref, w0_ref, a0_ref, kk_ref, ka_ref, rk_ref,
                      w2_ref, a2_ref, g2_ref, bsel_ref,
                      r_out, d_out, k_out, v_out, kk_out, kka_out, g_out, bonus_out,
                      carry_ref, *, tiles_per_seq, time_major):
    t = pl.program_id(0)
    pa = p_ref[...]
    tm = pa.shape[0]
    if time_major:
        @pl.when(t == 0)
        def _():
            carry_ref[...] = sp_ref[...]

        prev = carry_ref[...]
        carry_ref[...] = pa
    else:
        @pl.when(t % tiles_per_seq == 0)
        def _():
            carry_ref[...] = sp_ref[...]

        row = lax.broadcasted_iota(jnp.int32, pa.shape, 0)
        prev = jnp.where(row == 0, carry_ref[...], pltpu.roll(pa, 1, axis=0))
        carry_ref[...] = pa[tm - 1:tm, :]

    def put(ref, val):
        ref[...] = val.T if time_major else val

    xm = pa + (prev - pa) * mu_ref[...]
    r = xm[:, 0:A_WIDTH]
    k = xm[:, A_WIDTH:2 * A_WIDTH]
    v = xm[:, 2 * A_WIDTH:3 * A_WIDTH]
    lora = xm[:, 3 * A_WIDTH:A_PAD]
    m_w = _bdot(jnp.tanh(lora).astype(BF16), w2_ref[...])
    m_a = _bdot(lora.astype(BF16), a2_ref[...])
    g = _bdot(jax.nn.sigmoid(lora).astype(BF16), g2_ref[...])
    z = -(w0_ref[...] + m_w)
    softplus = jnp.maximum(z, 0.0) + jnp.log1p(jnp.exp(-jnp.abs(z)))
    w_log = -softplus - 0.5
    a = jax.nn.sigmoid(a0_ref[...] + m_a)
    bsel = bsel_ref[...]
    kkf = k * kk_ref[...]
    norm = jnp.sqrt(_seg_sum(kkf * kkf, bsel))
    kk = kkf / jnp.maximum(norm, 1e-12)
    k2 = k * (1.0 + (a - 1.0) * ka_ref[...])
    put(r_out, r)
    put(d_out, -jnp.exp(w_log))
    put(k_out, k2)
    put(v_out, v)
    put(kk_out, kk)
    put(kka_out, kk * a)
    g_out[...] = g
    bonus_out[...] = _seg_sum(r * k2 * rk_ref[...], bsel) * v


def _rwkv_prep(proj, sp, lw, seq_len, tm, time_major):
    M = proj.shape[0]
    R = sp.shape[1]
    tiles_per_seq = 1 if time_major else seq_len // tm
    sp_map = (lambda i: (0, 0, 0)) if time_major else (lambda i: (i // tiles_per_seq, 0, 0))
    row = lambda n: pl.BlockSpec((1, n), lambda i: (0, 0))
    full = lambda a, b: pl.BlockSpec((a, b), lambda i: (0, 0))
    tok_spec = pl.BlockSpec((tm, A_WIDTH), lambda i: (i, 0))
    tok_shape = jax.ShapeDtypeStruct((M, A_WIDTH), F32)
    if time_major:
        op_spec = pl.BlockSpec((A_WIDTH, tm), lambda i: (0, i))
        op_shape = jax.ShapeDtypeStruct((A_WIDTH, M), F32)
    else:
        op_spec, op_shape = tok_spec, tok_shape
    return pl.pallas_call(
        functools.partial(_rwkv_prep_kernel, tiles_per_seq=tiles_per_seq, time_major=time_major),
        grid=(M // tm,),
        in_specs=[pl.BlockSpec((tm, A_PAD), lambda i: (i, 0)),
                  pl.BlockSpec((None, R, A_PAD), sp_map),
                  row(A_PAD), row(A_WIDTH), row(A_WIDTH), row(A_WIDTH), row(A_WIDTH), row(A_WIDTH),
                  full(LORA_PAD, A_WIDTH), full(LORA_PAD, A_WIDTH), full(LORA_PAD, A_WIDTH),
                  full(SEG, SEG)],
        out_specs=[op_spec] * 6 + [tok_spec] * 2,
        out_shape=[op_shape] * 6 + [tok_shape] * 2,
        scratch_shapes=[pltpu.VMEM((R, A_PAD), F32)],
        compiler_params=_cparams(1),
        name="rwkv_prep",
    )(proj, sp, lw["mu"], lw["w0"], lw["a0"], lw["k_k"], lw["k_a"], lw["r_k"],
      lw["w2"], lw["a2"], lw["g2"], lw["bsel"])


def _rwkv_lane_scan_kernel(r_ref, w_ref, k_ref, v_ref, kk_ref, kka_ref, s0_ref, o_ref, s_out_ref, st_ref, *, steps):
    n = A_HEAD_DIM
    nb = s0_ref.shape[0]
    st_ref[...] = s0_ref[...].T
    tsl = [slice(t * nb, (t + 1) * nb) for t in range(steps)]

    def group(g, carry):
        rows8 = pl.ds(pl.multiple_of(g * SUBLANE, SUBLANE), SUBLANE)
        v8 = [v_ref[rows8, tsl[t]] for t in range(steps)]
        o8 = [[] for _ in range(steps)]
        for u in range(SUBLANE):
            srows = pl.ds(pl.multiple_of((g * SUBLANE + u) * n, n), n)
            s = st_ref[srows, :]
            for t in range(steps):
                sk = jnp.sum(s * kk_ref[:, tsl[t]], axis=0, keepdims=True)
                s = s * jnp.exp(w_ref[:, tsl[t]]) - sk * kka_ref[:, tsl[t]] + v8[t][u:u + 1, :] * k_ref[:, tsl[t]]
                o8[t].append(jnp.sum(s * r_ref[:, tsl[t]], axis=0, keepdims=True))
            st_ref[srows, :] = s
        for t in range(steps):
            o_ref[rows8, tsl[t]] = jnp.concatenate(o8[t], axis=0)
        return carry

    lax.fori_loop(0, n // SUBLANE, group, 0)
    s_out_ref[...] = st_ref[...].T


def _rwkv_lane_scan(ops_t, s0, B, T):
    n2 = A_HEAD_DIM * A_HEAD_DIM
    ospec = pl.BlockSpec((A_HEAD_DIM, T * B), lambda h: (h, 0))
    sspec = pl.BlockSpec((B, n2), lambda h: (0, h))
    o, s_t = pl.pallas_call(
        functools.partial(_rwkv_lane_scan_kernel, steps=T),
        grid=(A_HEADS,),
        in_specs=[ospec] * 6 + [sspec],
        out_specs=[ospec, sspec],
        out_shape=[jax.ShapeDtypeStruct((A_WIDTH, T * B), F32),
                   jax.ShapeDtypeStruct((B, A_HEADS * n2), F32)],
        scratch_shapes=[pltpu.VMEM((n2, B), F32)],
        compiler_params=_cparams(1),
        name="rwkv_lane_scan",
    )(*ops_t, s0.reshape(B, A_HEADS * n2))
    return o, s_t.reshape(B, A_HEADS, A_HEAD_DIM, A_HEAD_DIM)


def _rwkv_chunk_kernel(r_ref, w_ref, k_ref, v_ref, kk_ref, kka_ref, s0_ref, o_ref, s_out_ref, s_ref):
    c = pl.program_id(1)
    L = CHUNK
    P = 2 * L
    n_pairs = A_HEADS // 2
    left = lax.broadcasted_iota(jnp.int32, (L, LANE), 1) < A_HEAD_DIM
    row = lax.broadcasted_iota(jnp.int32, (P, P), 0)
    col = lax.broadcasted_iota(jnp.int32, (P, P), 1)
    strict = row > col
    lower = row >= col
    eye = (row == col).astype(F32)
    same = lambda n: (row // n) == (col // n)
    base = same(SOLVE_BASE)
    levels = []
    n = SOLVE_BASE
    while n < L:
        levels.append(same(2 * n) & jnp.logical_not(same(n)))
        n *= 2
    bf = lambda x: x.astype(BF16)
    nt = lambda a, b: lax.dot_general(a, b, (((1,), (1,)), ((), ())), preferred_element_type=F32)
    tn = lambda a, b: lax.dot_general(a, b, (((0,), (0,)), ((), ())), preferred_element_type=F32)

    @pl.when(c == 0)
    def _():
        z = jnp.zeros((A_HEAD_DIM, A_HEAD_DIM), F32)
        for p in range(n_pairs):
            s_ref[p] = jnp.concatenate([jnp.concatenate([s0_ref[2 * p], z], axis=1),
                                        jnp.concatenate([z, s0_ref[2 * p + 1]], axis=1)], axis=0)

    w = w_ref[...]
    tl = lax.broadcasted_iota(jnp.int32, (L, L), 0) >= lax.broadcasted_iota(jnp.int32, (L, L), 1)
    tri = tl.astype(BF16)
    w_hi = bf(w)
    w_r1 = w - w_hi.astype(F32)
    w_mid = bf(w_r1)
    w_lo = bf(w_r1 - w_mid.astype(F32))
    cum = _bdot(tri, w_hi) + _bdot(tri, w_mid) + _bdot(tri, w_lo)
    g_in = jnp.exp(cum)
    g_inv = jnp.exp(-cum)
    g_last = g_in[L - 1:L, :]
    at = -kk_ref[...] * jnp.exp(cum - w)
    bt = kka_ref[...] * g_inv
    kt = k_ref[...] * g_inv
    rt = r_ref[...] * g_in
    bl = bt * g_last
    kl = kt * g_last
    v = v_ref[...]

    pairs = range(n_pairs)
    lanes = [slice(p * LANE, (p + 1) * LANE) for p in pairs]

    def stack(x, p):
        xp = x[:, lanes[p]]
        return bf(jnp.concatenate([jnp.where(left, xp, 0.0), jnp.where(left, 0.0, xp)], axis=0))

    a2 = [stack(at, p) for p in pairs]
    r2 = [stack(rt, p) for p in pairs]
    b2 = [stack(bt, p) for p in pairs]
    k2 = [stack(kt, p) for p in pairs]
    v2 = [stack(v, p) for p in pairs]
    sc = [nt(jnp.concatenate([a2[p], r2[p]], axis=0), jnp.concatenate([b2[p], k2[p]], axis=0)) for p in pairs]
    a_ab = [jnp.where(strict, sc[p][0:P, 0:P], 0.0) for p in pairs]
    a_ak = [bf(jnp.where(strict, sc[p][0:P, P:2 * P], 0.0)) for p in pairs]
    a_rb = [bf(jnp.where(lower, sc[p][P:2 * P, 0:P], 0.0)) for p in pairs]
    a_rk = [bf(jnp.where(lower, sc[p][P:2 * P, P:2 * P], 0.0)) for p in pairs]
    pw = [bf(jnp.where(base, a_ab[p], 0.0)) for p in pairs]
    t_inv = [eye + pw[p].astype(F32) for p in pairs]
    m = 2
    while m < SOLVE_BASE:
        pw = [bf(_bdot(pw[p], pw[p])) for p in pairs]
        t_inv = [t_inv[p] + _bdot(bf(t_inv[p]), pw[p]) for p in pairs]
        m *= 2
    for lv in levels:
        half = [bf(_bdot(bf(t_inv[p]), bf(jnp.where(lv, a_ab[p], 0.0)))) for p in pairs]
        t_inv = [t_inv[p] + _bdot(half[p], bf(t_inv[p])) for p in pairs]
    s = [s_ref[p] for p in pairs]
    s_b = [bf(s[p]) for p in pairs]
    wmat = [bf(nt(a2[p], s_b[p]) + _bdot(a_ak[p], v2[p])) for p in pairs]
    u = [bf(_bdot(bf(t_inv[p]), wmat[p])) for p in pairs]
    for p in pairs:
        o2 = nt(r2[p], s_b[p]) + _bdot(a_rb[p], u[p]) + _bdot(a_rk[p], v2[p])
        o_ref[:, lanes[p]] = o2[0:L, :] + o2[L:P, :]
    for p in pairs:
        upd = tn(jnp.concatenate([u[p], v2[p]], axis=0),
                 jnp.concatenate([stack(bl, p), stack(kl, p)], axis=0))
        s_ref[p] = s[p] * g_last[:, lanes[p]] + upd

    @pl.when(c == pl.num_programs(1) - 1)
    def _():
        for p in range(n_pairs):
            s = s_ref[p]
            s_out_ref[2 * p] = s[0:A_HEAD_DIM, 0:A_HEAD_DIM]
            s_out_ref[2 * p + 1] = s[A_HEAD_DIM:LANE, A_HEAD_DIM:LANE]


def _rwkv_scan_chunked(ops, s0, B, T):
    ops3 = [a.reshape(B, T, A_WIDTH) for a in ops]
    tspec = pl.BlockSpec((None, CHUNK, A_WIDTH), lambda b, c: (b, c, 0))
    sspec = pl.BlockSpec((None, A_HEADS, A_HEAD_DIM, A_HEAD_DIM), lambda b, c: (b, 0, 0, 0))
    o, s_t = pl.pallas_call(
        _rwkv_chunk_kernel,
        grid=(B, T // CHUNK),
        in_specs=[tspec] * 6 + [sspec],
        out_specs=[tspec, sspec],
        out_shape=[jax.ShapeDtypeStruct((B, T, A_WIDTH), F32),
                   jax.ShapeDtypeStruct((B, A_HEADS, A_HEAD_DIM, A_HEAD_DIM), F32)],
        scratch_shapes=[pltpu.VMEM((A_HEADS // 2, LANE, LANE), F32)],
        compiler_params=_cparams(2),
        name="rwkv_chunk",
    )(*ops3, s0)
    return o.reshape(B * T, A_WIDTH), s_t


def _rwkv_post_kernel(o_ref, bonus_ref, g_ref, lnw_ref, lnb_ref, bsel_ref, y_ref, *, o_feature_major):
    o = o_ref[...].T if o_feature_major else o_ref[...]
    bsel = bsel_ref[...]
    inv_n = 1.0 / A_HEAD_DIM
    mean = _seg_sum(o, bsel) * inv_n
    c = o - mean
    var = _seg_sum(c * c, bsel) * inv_n
    y = c * lax.rsqrt(var + GN_EPS) * lnw_ref[...] + lnb_ref[...]
    y_ref[...] = ((y + bonus_ref[...]) * g_ref[...]).astype(BF16)


def _rwkv_post(o, bonus, g, lw, tm, o_feature_major=False):
    M = bonus.shape[0]
    tile = pl.BlockSpec((tm, A_WIDTH), lambda i: (i, 0))
    o_tile = pl.BlockSpec((A_WIDTH, tm), lambda i: (0, i)) if o_feature_major else tile
    row = pl.BlockSpec((1, A_WIDTH), lambda i: (0, 0))
    return pl.pallas_call(
        functools.partial(_rwkv_post_kernel, o_feature_major=o_feature_major),
        grid=(M // tm,),
        in_specs=[o_tile, tile, tile, row, row, pl.BlockSpec((SEG, SEG), lambda i: (0, 0))],
        out_specs=tile,
        out_shape=jax.ShapeDtypeStruct((M, A_WIDTH), BF16),
        compiler_params=_cparams(1),
        name="rwkv_post",
    )(o, bonus, g, lw["ln_w"], lw["ln_b"], lw["bsel"])


def _rope_kernel(x_ref, c_ref, s1_ref, s2_ref, o_ref):
    x = x_ref[...]
    fwd = pltpu.roll(x, x.shape[1] - ROPE_HALF, axis=1)
    bwd = pltpu.roll(x, ROPE_HALF, axis=1)
    o_ref[...] = x * c_ref[...] + fwd * s1_ref[...] + bwd * s2_ref[...]


def _rope(proj, tables, tm):
    M = proj.shape[0]
    n_tab = tables[0].shape[0] // tm
    tab = pl.BlockSpec((tm, SEG), lambda i, j: (i % n_tab, 0))
    return pl.pallas_call(
        _rope_kernel,
        grid=(M // tm, QK_W // SEG),
        in_specs=[pl.BlockSpec((tm, SEG), lambda i, j: (i, Q_OFF // SEG + j)), tab, tab, tab],
        out_specs=pl.BlockSpec((tm, SEG), lambda i, j: (i, j)),
        out_shape=jax.ShapeDtypeStruct((M, QK_W), F32),
        compiler_params=_cparams(2),
        name="rope",
    )(proj, *tables)


def _swa_prompt_kernel(q_ref, kc_ref, kp_ref, vc_ref, vp_ref, sink_ref, o_ref):
    n = pl.program_id(1)
    rows = GROUP * WINDOW
    i = lax.broadcasted_iota(jnp.int32, (rows, 2 * WINDOW), 0) % WINDOW
    j = lax.broadcasted_iota(jnp.int32, (rows, 2 * WINDOW), 1)
    diff = WINDOW + i - j
    mask = (diff >= 0) & (diff < WINDOW) & ((n > 0) | (j >= WINDOW))
    q = q_ref[...].astype(BF16)
    kc = kc_ref[...].astype(BF16)
    kp = kp_ref[...].astype(BF16)
    vc = vc_ref[...].astype(BF16)
    vp = vp_ref[...].astype(BF16)
    khs = range(KV_HEADS)
    ksl = [slice(kh * HEAD_DIM, (kh + 1) * HEAD_DIM) for kh in khs]
    kk = [jnp.concatenate([kp[:, ksl[kh]], kc[:, ksl[kh]]], axis=0) for kh in khs]
    vv = [jnp.concatenate([vp[:, ksl[kh]], vc[:, ksl[kh]]], axis=0) for kh in khs]
    qh = [jnp.concatenate([q[:, (kh * GROUP + g) * HEAD_DIM:(kh * GROUP + g + 1) * HEAD_DIM] for g in range(GROUP)],
                          axis=0) for kh in khs]
    s = [lax.dot_general(qh[kh], kk[kh], (((1,), (1,)), ((), ())), preferred_element_type=F32) for kh in khs]
    s = [jnp.where(mask, s[kh] * (HEAD_DIM ** -0.5), -jnp.inf) for kh in khs]
    m = [jnp.maximum(jnp.max(s[kh], axis=1, keepdims=True), sink_ref[kh]) for kh in khs]
    e = [jnp.exp(s[kh] - m[kh]) for kh in khs]
    denom = [jnp.sum(e[kh], axis=1, keepdims=True) + jnp.exp(sink_ref[kh] - m[kh]) for kh in khs]
    o = [_bdot((e[kh] / denom[kh]).astype(BF16), vv[kh]) for kh in khs]
    outs = [o[kh][g * WINDOW:(g + 1) * WINDOW, :] for kh in khs for g in range(GROUP)]
    o_ref[...] = jnp.concatenate(outs, axis=1).astype(o_ref.dtype)


def _swa_prompt(qk_rot, proj, sinks, B, T):
    nb = T // WINDOW
    rows = GROUP * WINDOW
    kcol = Q_W // KV_W
    vcol = V_OFF // KV_W
    cur = lambda col: (lambda b, n: (b * nb + n, col))
    prev = lambda col: (lambda b, n: (b * nb + jnp.maximum(n - 1, 0), col))
    sink_rows = jnp.repeat(sinks.reshape(KV_HEADS, GROUP), WINDOW, axis=1).reshape(KV_HEADS, rows, 1)
    return pl.pallas_call(
        _swa_prompt_kernel,
        grid=(B, nb),
        in_specs=[pl.BlockSpec((WINDOW, Q_W), cur(0)),
                  pl.BlockSpec((WINDOW, KV_W), cur(kcol)),
                  pl.BlockSpec((WINDOW, KV_W), prev(kcol)),
                  pl.BlockSpec((WINDOW, KV_W), cur(vcol)),
                  pl.BlockSpec((WINDOW, KV_W), prev(vcol)),
                  pl.BlockSpec((KV_HEADS, rows, 1), lambda b, n: (0, 0, 0))],
        out_specs=pl.BlockSpec((WINDOW, Q_W), cur(0)),
        out_shape=jax.ShapeDtypeStruct((B * T, Q_W), BF16),
        compiler_params=_cparams(2),
        name="swa_prompt",
    )(qk_rot, qk_rot, qk_rot, proj, proj, sink_rows)


def _swa_sample_kernel(q_ref, kn_ref, vn_ref, ck_ref, cv_ref, sink_ref, o_ref, *, steps):
    rows = GROUP * steps
    t_row = lax.broadcasted_iota(jnp.int32, (rows, WINDOW), 0) % steps
    c_col = lax.broadcasted_iota(jnp.int32, (rows, WINDOW), 1)
    cache_mask = c_col > t_row
    t_col = lax.broadcasted_iota(jnp.int32, (rows, 1), 0) % steps
    scale = HEAD_DIM ** -0.5
    probs = []
    units = [(b, kh) for b in range(q_ref.shape[0]) for kh in range(KV_HEADS)]
    ksl = [slice(kh * HEAD_DIM, (kh + 1) * HEAD_DIM) for kh in range(KV_HEADS)]
    qh = {u: q_ref[u[0], u[1]].astype(BF16) for u in units}
    sc = {u: lax.dot_general(qh[u], ck_ref[u[0], :, ksl[u[1]]].astype(BF16), (((1,), (1,)), ((), ())),
                             preferred_element_type=F32) for u in units}
    s_c, s_n, m_c = {}, {}, {}
    for u in units:
        b, kh = u
        kn = kn_ref[b].astype(BF16).astype(F32)
        s_c[u] = jnp.where(cache_mask, sc[u] * scale, -jnp.inf)
        qf = qh[u].astype(F32)
        s_n[u] = [jnp.where(t_col >= t, jnp.sum(qf * kn[t:t + 1, ksl[kh]], axis=1, keepdims=True) * scale, -jnp.inf)
                  for t in range(steps)]
        m_c[u] = jnp.max(s_c[u], axis=1, keepdims=True)
    for u in units:
        sink = sink_ref[u[1]]
        m = jnp.maximum(m_c[u], sink)
        for d in s_n[u]:
            m = jnp.maximum(m, d)
        ec = jnp.exp(s_c[u] - m)
        en = [jnp.exp(d - m) for d in s_n[u]]
        denom = jnp.sum(ec, axis=1, keepdims=True) + jnp.exp(sink - m)
        for e in en:
            denom = denom + e
        probs.append(((ec / denom).astype(BF16), [(e / denom).astype(BF16).astype(F32) for e in en]))
    for u, (pc, pn) in zip(units, probs):
        b, kh = u
        vn = vn_ref[b].astype(BF16).astype(F32)
        o = _bdot(pc, cv_ref[b, :, ksl[kh]].astype(BF16))
        for t in range(steps):
            o = o + pn[t] * vn[t:t + 1, ksl[kh]]
        o_ref[b, kh] = o


def _swa_sample(qk_rot, proj, cache_k, cache_v, sinks, B, T):
    rows = GROUP * T
    qg = qk_rot[:, :Q_W].reshape(T, B, KV_HEADS, GROUP, HEAD_DIM).transpose(1, 2, 3, 0, 4)
    qg = qg.reshape(B, KV_HEADS, rows, HEAD_DIM)
    kn = qk_rot[:, Q_W:].reshape(T, B, KV_W).transpose(1, 0, 2)
    vn = proj[:, V_OFF:V_OFF + KV_W].reshape(T, B, KV_W).transpose(1, 0, 2)
    sink_rows = jnp.repeat(sinks.reshape(KV_HEADS, GROUP), T, axis=1).reshape(KV_HEADS, rows, 1)
    nb = SUBLANE if B % SUBLANE == 0 else 1
    o = pl.pallas_call(
        functools.partial(_swa_sample_kernel, steps=T),
        grid=(B // nb,),
        in_specs=[pl.BlockSpec((nb, KV_HEADS, rows, HEAD_DIM), lambda b: (b, 0, 0, 0)),
                  pl.BlockSpec((nb, T, KV_W), lambda b: (b, 0, 0)),
                  pl.BlockSpec((nb, T, KV_W), lambda b: (b, 0, 0)),
                  pl.BlockSpec((nb, WINDOW, KV_W), lambda b: (b, 0, 0)),
                  pl.BlockSpec((nb, WINDOW, KV_W), lambda b: (b, 0, 0)),
                  pl.BlockSpec((KV_HEADS, rows, 1), lambda b: (0, 0, 0))],
        out_specs=pl.BlockSpec((nb, KV_HEADS, rows, HEAD_DIM), lambda b: (b, 0, 0, 0)),
        out_shape=jax.ShapeDtypeStruct((B, KV_HEADS, rows, HEAD_DIM), F32),
        compiler_params=_cparams(1),
        name="swa_sample",
    )(qg, kn, vn, cache_k.reshape(B, WINDOW, KV_W), cache_v.reshape(B, WINDOW, KV_W), sink_rows)
    o = o.reshape(B, KV_HEADS, GROUP, T, HEAD_DIM).transpose(3, 0, 1, 2, 4).reshape(T * B, Q_W)
    return o.astype(BF16)


def _rope_tables(pos):
    inv = 1.0 / (ROPE_THETA ** (jnp.arange(0, ROPE_DIM, 2, dtype=F32) / ROPE_DIM))
    ang = pos.astype(F32)[:, None] * inv[None, :]
    cos, sin = jnp.cos(ang), jnp.sin(ang)
    n = pos.shape[0]
    ones = jnp.ones((n, HEAD_DIM - ROPE_DIM), F32)
    zeros = jnp.zeros((n, HEAD_DIM - ROPE_DIM), F32)
    zh = jnp.zeros((n, ROPE_HALF), F32)
    c = jnp.concatenate([cos, cos, ones], axis=1)
    s1 = jnp.concatenate([-sin, zh, zeros], axis=1)
    s2 = jnp.concatenate([zh, sin, zeros], axis=1)
    return tuple(jnp.tile(a, (1, SEG // HEAD_DIM)) for a in (c, s1, s2))


def _layer_weights(l, norm_g, w_mod, b_mod, ffn_w_gate, ffn_w_up, ffn_w_down, w_in, rwkv_mu, rwkv_w0, rwkv_w2,
                   rwkv_a0, rwkv_a2, rwkv_g2, rwkv_k_k, rwkv_k_a, rwkv_r_k, rwkv_ln_w, rwkv_ln_b, attn_sinks,
                   w_branch_a, w_branch_b, w_merge_gate, w_out):
    wi = w_in[l]
    zc = lambda n: jnp.zeros((D_MODEL, n), wi.dtype)
    w_proj = jnp.concatenate([wi[:, :A_PROJ], zc(A_PAD - A_PROJ), wi[:, A_PROJ:]], axis=1)
    lora_rows = lambda w, off: jnp.zeros((LORA_PAD, A_WIDTH), F32).at[off:off + w.shape[0]].set(w).astype(BF16)
    seg = jnp.arange(SEG) // A_HEAD_DIM
    row = lambda a: a.reshape(1, -1)
    return {
        "l": l, "norm_g": norm_g[l], "w_mod": w_mod, "b_mod": b_mod,
        "wg": ffn_w_gate, "wu": ffn_w_up, "wd": ffn_w_down,
        "w_proj": w_proj,
        "mu": jnp.pad(rwkv_mu[l], (0, A_PAD - A_PROJ)).reshape(1, A_PAD),
        "w0": row(rwkv_w0[l]), "a0": row(rwkv_a0[l]), "k_k": row(rwkv_k_k[l]), "k_a": row(rwkv_k_a[l]),
        "r_k": row(rwkv_r_k[l]), "ln_w": row(rwkv_ln_w[l]), "ln_b": row(rwkv_ln_b[l]),
        "w2": lora_rows(rwkv_w2[l], 0), "a2": lora_rows(rwkv_a2[l], DECAY_LORA),
        "g2": lora_rows(rwkv_g2[l], DECAY_LORA + ICLR_LORA),
        "bsel": (seg[:, None] == seg[None, :]).astype(BF16),
        "sinks": attn_sinks[l],
        "w_a": w_branch_a, "w_b": w_branch_b, "w_gate": w_merge_gate, "w_out": w_out,
    }


def _decoder_layer(x, mod, lw, B, T, shift_prev, wkv_prev, cache_k, cache_v, tables):
    M = B * T
    time_major = cache_k is not None
    sp = jnp.pad(shift_prev, ((0, 0), (0, A_PAD - A_PROJ)))
    if time_major:
        tm = M
        tm_prep = B
        modv = lambda n: jnp.tile(mod[:, n, :], (T, 1)).reshape(1, M, D_MODEL)
        gmap = lambda tile: (lambda i: i)
        sp = sp.reshape(1, B, A_PAD)
    else:
        tm = min(T, 1024)
        tm_prep = min(tm, 256)
        modv = lambda n: mod[:, n:n + 1, :]
        gmap = lambda tile: (lambda i: i // (T // tile))
        sp = sp.reshape(B, 1, A_PAD)
    tm_k = min(tm, 512)

    l = lw["l"]

    def ffn(x, n, which):
        h = _ada_norm(x, lw["norm_g"][n], modv(3 * n), modv(3 * n + 1), gmap(tm), tm)
        act = _ffn_gate_up(h, lw["wg"], lw["wu"], (l, which), tm)
        gate = modv(3 * n + 2)
        if time_major:
            gate = gate.reshape(M // tm_k, tm_k, D_MODEL)
        return _mm_residual(act, lw["wd"], (l, which), x, gate, gmap(tm_k), 0.5, tm_k)

    x = ffn(x, 0, 0)

    h = _ada_norm(x, lw["norm_g"][1], modv(3), modv(4), gmap(tm), tm)
    proj = _mm_plain(h, lw["w_proj"], (), tm, name="in_proj")
    gates = _mm_plain(h, lw["w_gate"], (l,), tm, sigmoid=True, name="merge_gates")

    r, w, k2, v, kk, kka, g, bonus = _rwkv_prep(proj, sp, lw, T, tm_prep, time_major)
    if time_major:
        o, new_wkv = _rwkv_lane_scan((r, w, k2, v, kk, kka), wkv_prev, B, T)
        ya = _rwkv_post(o, bonus, g, lw, B, o_feature_major=True)
        new_shift = proj[(T - 1) * B:, :A_PROJ]
    else:
        o, new_wkv = _rwkv_scan_chunked((r, w, k2, v, kk, kka), wkv_prev, B, T)
        ya = _rwkv_post(o, bonus, g, lw, min(tm, 512))
        new_shift = proj.reshape(B, T, PROJ_W)[:, -1, :A_PROJ]

    qk_rot = _rope(proj, tables, min(tm, 512))
    if time_major:
        yb = _swa_sample(qk_rot, proj, cache_k, cache_v, lw["sinks"], B, T)
        bt = lambda a: a.reshape(T, B, KV_HEADS, HEAD_DIM).transpose(1, 0, 2, 3)
        new_k = jnp.concatenate([cache_k, bt(qk_rot[:, Q_W:])], axis=1)[:, -WINDOW:]
        new_v = jnp.concatenate([cache_v, bt(proj[:, V_OFF:V_OFF + KV_W])], axis=1)[:, -WINDOW:]
    else:
        yb = _swa_prompt(qk_rot, proj, lw["sinks"], B, T)
        new_k = qk_rot[:, Q_W:].reshape(B, T, KV_HEADS, HEAD_DIM)[:, -WINDOW:]
        new_v = proj[:, V_OFF:V_OFF + KV_W].reshape(B, T, KV_HEADS, HEAD_DIM)[:, -WINDOW:]

    merged = _mm_merge(ya, yb, lw["w_a"], lw["w_b"], (l,), gates, tm)
    gate = modv(5)
    x = _mm_residual(merged, lw["w_out"], (l,), x, gate, gmap(tm), 1.0, tm)

    x = ffn(x, 2, 1)
    return x, new_shift, new_wkv, new_k, new_v


def kernel(x_prompt, x_sample, c_prompt, c_sample, state_shift, state_wkv, cache_k, cache_v, norm_g, w_mod, b_mod,
           ffn_w_gate, ffn_w_up, ffn_w_down, w_in, rwkv_mu, rwkv_w0, rwkv_w2, rwkv_a0, rwkv_a2, rwkv_g2, rwkv_k_k,
           rwkv_k_a, rwkv_r_k, rwkv_ln_w, rwkv_ln_b, attn_sinks, w_branch_a, w_branch_b, w_merge_gate, w_out,
           final_norm):
    Bp, Tp, _ = x_prompt.shape
    Bs, Ts, _ = x_sample.shape
    tab_p = _rope_tables(jnp.arange(Tp))
    tab_s = _rope_tables(jnp.repeat(PAST_LEN + jnp.arange(Ts), Bs))
    zero_shift = jnp.zeros((Bp, A_PROJ), F32)
    zero_wkv = jnp.zeros((Bp, A_HEADS, A_HEAD_DIM, A_HEAD_DIM), F32)
    xp = x_prompt.reshape(Bp * Tp, D_MODEL)
    xs = x_sample.transpose(1, 0, 2).reshape(Ts * Bs, D_MODEL)
    c_all = jnp.concatenate([c_prompt, c_sample], axis=0)
    pad_rows = (-c_all.shape[0]) % 16
    c_all = jnp.pad(c_all, ((0, pad_rows), (0, 0)))
    outs_p, outs_s = [], []
    for l in range(DEPTH):
        lw = _layer_weights(l, norm_g, w_mod, b_mod, ffn_w_gate, ffn_w_up, ffn_w_down, w_in, rwkv_mu, rwkv_w0,
                            rwkv_w2, rwkv_a0, rwkv_a2, rwkv_g2, rwkv_k_k, rwkv_k_a, rwkv_r_k, rwkv_ln_w, rwkv_ln_b,
                            attn_sinks, w_branch_a, w_branch_b, w_merge_gate, w_out)
        mod = _modulation(c_all, w_mod, b_mod, l).reshape(-1, N_MOD, D_MODEL)
        xp, *st_p = _decoder_layer(xp, mod[:Bp], lw, Bp, Tp, zero_shift, zero_wkv, None, None, tab_p)
        xs, *st_s = _decoder_layer(xs, mod[Bp:Bp + Bs], lw, Bs, Ts, state_shift[l], state_wkv[l],
                                   cache_k[l], cache_v[l], tab_s)
        outs_p.append(st_p)
        outs_s.append(st_s)
    y_prompt = _final_norm(xp, final_norm, 1024).reshape(Bp, Tp, D_MODEL)
    y_sample = _final_norm(xs, final_norm, Bs * Ts).reshape(Ts, Bs, D_MODEL).transpose(1, 0, 2)
    stack = lambda outs, n: jnp.stack([o[n] for o in outs])
    return (y_prompt, y_sample,
            stack(outs_p, 0), stack(outs_p, 1), stack(outs_p, 2), stack(outs_p, 3),
            stack(outs_s, 0), stack(outs_s, 1), stack(outs_s, 2), stack(outs_s, 3))
```

```python
import functools

import jax
import jax.numpy as jnp
from jax import lax
from jax.experimental import pallas as pl
from jax.experimental.pallas import tpu as pltpu

D_MODEL = 2048
DEPTH = 2
A_HEADS = 16
A_HEAD_DIM = 64
A_WIDTH = A_HEADS * A_HEAD_DIM
DECAY_LORA = 64
ICLR_LORA = 64
GATE_LORA = 160
LORA_W = DECAY_LORA + ICLR_LORA + GATE_LORA
A_PROJ = 3 * A_WIDTH + LORA_W
GN_EPS = 64e-5
Q_HEADS = 16
KV_HEADS = 4
GROUP = Q_HEADS // KV_HEADS
HEAD_DIM = 64
Q_W = Q_HEADS * HEAD_DIM
KV_W = KV_HEADS * HEAD_DIM
WINDOW = 128
ROPE_DIM = HEAD_DIM // 4
ROPE_HALF = ROPE_DIM // 2
ROPE_THETA = 500000.0
D_FF = 5632
N_MOD = 9
RMS_EPS = 1e-6

PAST_LEN = 8192
LANE = 128
SUBLANE = 8
CHUNK = 64
SOLVE_BASE = 8
SEG = 256
A_PAD = 3584
LORA_PAD = A_PAD - 3 * A_WIDTH
Q_OFF = A_PAD
K_OFF = Q_OFF + Q_W
V_OFF = K_OFF + KV_W
PROJ_W = V_OFF + KV_W
QK_W = Q_W + KV_W
VMEM_LIMIT = 56 * 1024 * 1024

F32 = jnp.float32
BF16 = jnp.bfloat16


def _cparams(n_axes):
    return pltpu.CompilerParams(dimension_semantics=("arbitrary",) * n_axes,
                                vmem_limit_bytes=VMEM_LIMIT)


def _bdot(a, b):
    return jnp.dot(a, b, preferred_element_type=F32)


def _seg_sum(x, bsel):
    hi = x.astype(BF16)
    lo = (x - hi.astype(F32)).astype(BF16)
    parts = []
    for c in range(x.shape[1] // SEG):
        sl = slice(c * SEG, (c + 1) * SEG)
        parts.append(_bdot(hi[:, sl], bsel) + _bdot(lo[:, sl], bsel))
    return jnp.concatenate(parts, axis=1)


def _mod_kernel(c_ref, w_ref, b_ref, o_ref):
    c = c_ref[...]
    a = (c * jax.nn.sigmoid(c)).astype(BF16)
    o_ref[...] = _bdot(a, w_ref[...].astype(BF16)) + b_ref[...]


def _modulation(c, w_mod, b_mod, l):
    R = c.shape[0]
    N = w_mod.shape[-1]
    tn = 1024
    return pl.pallas_call(
        _mod_kernel,
        grid=(N // tn,),
        in_specs=[pl.BlockSpec((R, D_MODEL), lambda j: (0, 0)),
                  pl.BlockSpec((None, D_MODEL, tn), lambda j: (l, 0, j)),
                  pl.BlockSpec((None, 1, tn), lambda j: (l, 0, j))],
        out_specs=pl.BlockSpec((R, tn), lambda j: (0, j)),
        out_shape=jax.ShapeDtypeStruct((R, N), F32),
        compiler_params=_cparams(1),
        name="modulation",
    )(c, w_mod, b_mod.reshape(b_mod.shape[0], 1, N))


def _mod_rows(ref, tm):
    val = ref[...]
    reps = tm // val.shape[0]
    return val if val.shape[0] == 1 or reps == 1 else jnp.concatenate([val] * reps, axis=0)


def _ada_norm_kernel(x_ref, g_ref, sh_ref, sc_ref, o_ref):
    x = x_ref[...]
    tm = x.shape[0]
    y = x * lax.rsqrt(jnp.mean(x * x, axis=-1, keepdims=True) + RMS_EPS)
    y = y * g_ref[...]
    o_ref[...] = (y * (1.0 + _mod_rows(sc_ref, tm)) + _mod_rows(sh_ref, tm)).astype(o_ref.dtype)


def _ada_norm(x, norm_g, l, n, mod, gmap, tm):
    M = x.shape[0]
    R = mod.shape[1]
    mspec = lambda col: pl.BlockSpec((None, R, D_MODEL), lambda i: (gmap(i), 0, col))
    return pl.pallas_call(
        _ada_norm_kernel,
        grid=(M // tm,),
        in_specs=[pl.BlockSpec((tm, D_MODEL), lambda i: (i, 0)),
                  pl.BlockSpec((None, None, 1, D_MODEL), lambda i: (l, n, 0, 0)),
                  mspec(3 * n), mspec(3 * n + 1)],
        out_specs=pl.BlockSpec((tm, D_MODEL), lambda i: (i, 0)),
        out_shape=jax.ShapeDtypeStruct((M, D_MODEL), BF16),
        compiler_params=_cparams(1),
        name="ada_norm",
    )(x, norm_g.reshape(norm_g.shape[0], norm_g.shape[1], 1, D_MODEL), mod, mod)


def _final_norm_kernel(x_ref, g_ref, o_ref):
    x = x_ref[...]
    y = x * lax.rsqrt(jnp.mean(x * x, axis=-1, keepdims=True) + RMS_EPS)
    o_ref[...] = y * g_ref[...]


def _final_norm(x, gain, tm):
    M = x.shape[0]
    return pl.pallas_call(
        _final_norm_kernel,
        grid=(M // tm,),
        in_specs=[pl.BlockSpec((tm, D_MODEL), lambda i: (i, 0)),
                  pl.BlockSpec((1, D_MODEL), lambda i: (0, 0))],
        out_specs=pl.BlockSpec((tm, D_MODEL), lambda i: (i, 0)),
        out_shape=jax.ShapeDtypeStruct((M, D_MODEL), F32),
        compiler_params=_cparams(1),
        name="final_norm",
    )(x, gain.reshape(1, D_MODEL))


def _wspec(w, lead, tn):
    return pl.BlockSpec((None,) * len(lead) + (w.shape[-2], tn), lambda j, i: lead + (0, j))


def _wdot(a, w_ref):
    return _bdot(a, w_ref[...].astype(BF16))


def _gu_kernel(h_ref, wg_ref, wu_ref, o_ref):
    h = h_ref[...]
    g = _wdot(h, wg_ref)
    u = _wdot(h, wu_ref)
    o_ref[...] = ((g * jax.nn.sigmoid(g)) * u).astype(o_ref.dtype)


def _ffn_gate_up(h, wg, wu, lead, tm, tn=512):
    M, K = h.shape
    N = wg.shape[-1]
    return pl.pallas_call(
        _gu_kernel,
        grid=(N // tn, M // tm),
        in_specs=[pl.BlockSpec((tm, K), lambda j, i: (i, 0)), _wspec(wg, lead, tn), _wspec(wu, lead, tn)],
        out_specs=pl.BlockSpec((tm, tn), lambda j, i: (i, j)),
        out_shape=jax.ShapeDtypeStruct((M, N), BF16),
        compiler_params=_cparams(2),
        name="ffn_gate_up",
    )(h, wg, wu)


def _resid_kernel(a_ref, w_ref, x_ref, gate_ref, o_ref, *, coef):
    y = _wdot(a_ref[...], w_ref)
    o_ref[...] = x_ref[...] + (coef * _mod_rows(gate_ref, y.shape[0])) * y


def _mm_residual(a, w, lead, x, mod, n_gate, gmap, coef, tm, tn=512):
    M, K = a.shape
    N = w.shape[-1]
    R = mod.shape[1]
    nj = N // tn
    return pl.pallas_call(
        functools.partial(_resid_kernel, coef=coef),
        grid=(nj, M // tm),
        in_specs=[pl.BlockSpec((tm, K), lambda j, i: (i, 0)),
                  _wspec(w, lead, tn),
                  pl.BlockSpec((tm, tn), lambda j, i: (i, j)),
                  pl.BlockSpec((None, R, tn), lambda j, i: (gmap(i), 0, n_gate * nj + j))],
        out_specs=pl.BlockSpec((tm, tn), lambda j, i: (i, j)),
        out_shape=jax.ShapeDtypeStruct((M, N), F32),
        compiler_params=_cparams(2),
        name="mm_residual",
    )(a, w, x, mod)


def _plain_kernel(a_ref, w_ref, o_ref, *, sigmoid):
    y = _wdot(a_ref[...], w_ref)
    if sigmoid:
        y = jax.nn.sigmoid(y)
    o_ref[...] = y.astype(o_ref.dtype)


def _mm_plain(a, w, lead, tm, tn=512, sigmoid=False, name="mm_plain"):
    M, K = a.shape
    N = w.shape[-1]
    return pl.pallas_call(
        functools.partial(_plain_kernel, sigmoid=sigmoid),
        grid=(N // tn, M // tm),
        in_specs=[pl.BlockSpec((tm, K), lambda j, i: (i, 0)), _wspec(w, lead, tn)],
        out_specs=pl.BlockSpec((tm, tn), lambda j, i: (i, j)),
        out_shape=jax.ShapeDtypeStruct((M, N), F32),
        compiler_params=_cparams(2),
        name=name,
    )(a, w)


def _merge_kernel(ya_ref, yb_ref, wa_ref, wb_ref, ga_ref, gb_ref, o_ref):
    pa = _wdot(ya_ref[...], wa_ref)
    pb = _wdot(yb_ref[...], wb_ref)
    o_ref[...] = (ga_ref[...] * pa + gb_ref[...] * pb).astype(o_ref.dtype)


def _mm_merge(ya, yb, wa, wb, lead, gates, tm, tn=512):
    M, K = ya.shape
    N = wa.shape[-1]
    nj = N // tn
    return pl.pallas_call(
        _merge_kernel,
        grid=(nj, M // tm),
        in_specs=[pl.BlockSpec((tm, K), lambda j, i: (i, 0)),
                  pl.BlockSpec((tm, K), lambda j, i: (i, 0)),
                  _wspec(wa, lead, tn), _wspec(wb, lead, tn),
                  pl.BlockSpec((tm, tn), lambda j, i: (i, j)),
                  pl.BlockSpec((tm, tn), lambda j, i: (i, j + nj))],
        out_specs=pl.BlockSpec((tm, tn), lambda j, i: (i, j)),
        out_shape=jax.ShapeDtypeStruct((M, N), BF16),
        compiler_params=_cparams(2),
        name="mm_merge",
    )(ya, yb, wa, wb, gates, gates)


def _rwkv_prep_kernel(p_ref, sp_ref, mu_ref, w0_ref, a0_ref, kk_ref, ka_ref, rk_ref,
                      w2_ref, a2_ref, g2_ref, bsel_ref,
                      r_out, d_out, k_out, v_out, kk_out, kka_out, g_out, bonus_out,
                      carry_ref, *, tiles_per_seq, time_major):
    t = pl.program_id(0)
    pa = p_ref[...]
    tm = pa.shape[0]
    if time_major:
        @pl.when(t == 0)
        def _():
            carry_ref[...] = sp_ref[...]

        prev = carry_ref[...]
        carry_ref[...] = pa
    else:
        @pl.when(t % tiles_per_seq == 0)
        def _():
            carry_ref[...] = sp_ref[...]

        row = lax.broadcasted_iota(jnp.int32, pa.shape, 0)
        prev = jnp.where(row == 0, carry_ref[...], pltpu.roll(pa, 1, axis=0))
        carry_ref[...] = pa[tm - 1:tm, :]

    def put(ref, val):
        ref[...] = val.T if time_major else val

    xm = pa + (prev - pa) * mu_ref[...]
    r = xm[:, 0:A_WIDTH]
    k = xm[:, A_WIDTH:2 * A_WIDTH]
    v = xm[:, 2 * A_WIDTH:3 * A_WIDTH]
    lora = xm[:, 3 * A_WIDTH:A_PAD]
    m_w = _bdot(jnp.tanh(lora).astype(BF16), w2_ref[...])
    m_a = _bdot(lora.astype(BF16), a2_ref[...])
    g = _bdot(jax.nn.sigmoid(lora).astype(BF16), g2_ref[...])
    z = -(w0_ref[...] + m_w)
    softplus = jnp.maximum(z, 0.0) + jnp.log1p(jnp.exp(-jnp.abs(z)))
    w_log = -softplus - 0.5
    a = jax.nn.sigmoid(a0_ref[...] + m_a)
    bsel = bsel_ref[...]
    kkf = k * kk_ref[...]
    norm = jnp.sqrt(_seg_sum(kkf * kkf, bsel))
    kk = kkf / jnp.maximum(norm, 1e-12)
    k2 = k * (1.0 + (a - 1.0) * ka_ref[...])
    put(r_out, r)
    put(d_out, -jnp.exp(w_log))
    put(k_out, k2)
    put(v_out, v)
    put(kk_out, kk)
    put(kka_out, kk * a)
    g_out[...] = g
    bonus_out[...] = _seg_sum(r * k2 * rk_ref[...], bsel) * v


def _rwkv_prep(proj, sp, lw, seq_len, tm, time_major):
    M = proj.shape[0]
    R = sp.shape[1]
    l = lw["l"]
    tiles_per_seq = 1 if time_major else seq_len // tm
    sp_map = (lambda i: (0, 0, 0)) if time_major else (lambda i: (i // tiles_per_seq, 0, 0))
    row = lambda n: pl.BlockSpec((None, 1, n), lambda i: (l, 0, 0))
    full = lambda a, b: pl.BlockSpec((None, a, b), lambda i: (l, 0, 0))
    tok_spec = pl.BlockSpec((tm, A_WIDTH), lambda i: (i, 0))
    tok_shape = jax.ShapeDtypeStruct((M, A_WIDTH), F32)
    if time_major:
        op_spec = pl.BlockSpec((A_WIDTH, tm), lambda i: (0, i))
        op_shape = jax.ShapeDtypeStruct((A_WIDTH, M), F32)
    else:
        op_spec, op_shape = tok_spec, tok_shape
    return pl.pallas_call(
        functools.partial(_rwkv_prep_kernel, tiles_per_seq=tiles_per_seq, time_major=time_major),
        grid=(M // tm,),
        in_specs=[pl.BlockSpec((tm, A_PAD), lambda i: (i, 0)),
                  pl.BlockSpec((None, R, A_PAD), sp_map),
                  row(A_PAD), row(A_WIDTH), row(A_WIDTH), row(A_WIDTH), row(A_WIDTH), row(A_WIDTH),
                  full(LORA_PAD, A_WIDTH), full(LORA_PAD, A_WIDTH), full(LORA_PAD, A_WIDTH),
                  pl.BlockSpec((SEG, SEG), lambda i: (0, 0))],
        out_specs=[op_spec] * 6 + [tok_spec] * 2,
        out_shape=[op_shape] * 6 + [tok_shape] * 2,
        scratch_shapes=[pltpu.VMEM((R, A_PAD), F32)],
        compiler_params=_cparams(1),
        name="rwkv_prep",
    )(proj, sp, lw["mu"], lw["w0"], lw["a0"], lw["k_k"], lw["k_a"], lw["r_k"],
      lw["w2"], lw["a2"], lw["g2"], lw["bsel"])


def _rwkv_lane_scan_kernel(r_ref, w_ref, k_ref, v_ref, kk_ref, kka_ref, s0_ref, *rest, steps, n_prev):
    if n_prev:
        prev_ref, o_ref, s_out_ref, st_ref = rest
        for li in range(n_prev):
            s_out_ref[li] = prev_ref[li]
    else:
        o_ref, s_out_ref, st_ref = rest
    n = A_HEAD_DIM
    nb = s0_ref.shape[0]
    st_ref[...] = s0_ref[...].T
    tsl = [slice(t * nb, (t + 1) * nb) for t in range(steps)]

    def group(g, carry):
        rows8 = pl.ds(pl.multiple_of(g * SUBLANE, SUBLANE), SUBLANE)
        v8 = [v_ref[rows8, tsl[t]] for t in range(steps)]
        o8 = [[] for _ in range(steps)]
        for u in range(SUBLANE):
            srows = pl.ds(pl.multiple_of((g * SUBLANE + u) * n, n), n)
            s = st_ref[srows, :]
            for t in range(steps):
                sk = jnp.sum(s * kk_ref[:, tsl[t]], axis=0, keepdims=True)
                s = s * jnp.exp(w_ref[:, tsl[t]]) - sk * kka_ref[:, tsl[t]] + v8[t][u:u + 1, :] * k_ref[:, tsl[t]]
                o8[t].append(jnp.sum(s * r_ref[:, tsl[t]], axis=0, keepdims=True))
            st_ref[srows, :] = s
        for t in range(steps):
            o_ref[rows8, tsl[t]] = jnp.concatenate(o8[t], axis=0)
        return carry

    lax.fori_loop(0, n // SUBLANE, group, 0)
    s_out_ref[n_prev] = st_ref[...].T


def _rwkv_lane_scan(ops_t, state, prev, l, B, T):
    n2 = A_HEAD_DIM * A_HEAD_DIM
    ospec = pl.BlockSpec((A_HEAD_DIM, T * B), lambda h: (h, 0))
    in_specs = [ospec] * 6 + [pl.BlockSpec((None, B, n2), lambda h: (l, 0, h))]
    args = list(ops_t) + [state]
    if l:
        in_specs.append(pl.BlockSpec((l, B, n2), lambda h: (0, 0, h)))
        args.append(prev)
    return pl.pallas_call(
        functools.partial(_rwkv_lane_scan_kernel, steps=T, n_prev=l),
        grid=(A_HEADS,),
        in_specs=in_specs,
        out_specs=[ospec, pl.BlockSpec((l + 1, B, n2), lambda h: (0, 0, h))],
        out_shape=[jax.ShapeDtypeStruct((A_WIDTH, T * B), F32),
                   jax.ShapeDtypeStruct((l + 1, B, A_HEADS * n2), F32)],
        scratch_shapes=[pltpu.VMEM((n2, B), F32)],
        compiler_params=_cparams(1),
        name="rwkv_lane_scan",
    )(*args)


def _rwkv_chunk_kernel(r_ref, w_ref, k_ref, v_ref, kk_ref, kka_ref, s0_ref, o_ref, s_out_ref, s_ref):
    c = pl.program_id(1)
    L = CHUNK
    P = 2 * L
    n_pairs = A_HEADS // 2
    left = lax.broadcasted_iota(jnp.int32, (L, LANE), 1) < A_HEAD_DIM
    row = lax.broadcasted_iota(jnp.int32, (P, P), 0)
    col = lax.broadcasted_iota(jnp.int32, (P, P), 1)
    strict = row > col
    lower = row >= col
    eye = (row == col).astype(F32)
    same = lambda n: (row // n) == (col // n)
    base = same(SOLVE_BASE)
    levels = []
    n = SOLVE_BASE
    while n < L:
        levels.append(same(2 * n) & jnp.logical_not(same(n)))
        n *= 2
    bf = lambda x: x.astype(BF16)
    nt = lambda a, b: lax.dot_general(a, b, (((1,), (1,)), ((), ())), preferred_element_type=F32)
    tn = lambda a, b: lax.dot_general(a, b, (((0,), (0,)), ((), ())), preferred_element_type=F32)

    @pl.when(c == 0)
    def _():
        z = jnp.zeros((A_HEAD_DIM, A_HEAD_DIM), F32)
        for p in range(n_pairs):
            s_ref[p] = jnp.concatenate([jnp.concatenate([s0_ref[2 * p], z], axis=1),
                                        jnp.concatenate([z, s0_ref[2 * p + 1]], axis=1)], axis=0)

    w = w_ref[...]
    tl = lax.broadcasted_iota(jnp.int32, (L, L), 0) >= lax.broadcasted_iota(jnp.int32, (L, L), 1)
    tri = tl.astype(BF16)
    w_hi = bf(w)
    w_r1 = w - w_hi.astype(F32)
    w_mid = bf(w_r1)
    w_lo = bf(w_r1 - w_mid.astype(F32))
    cum = _bdot(tri, w_hi) + _bdot(tri, w_mid) + _bdot(tri, w_lo)
    g_in = jnp.exp(cum)
    g_inv = jnp.exp(-cum)
    g_last = g_in[L - 1:L, :]
    at = -kk_ref[...] * jnp.exp(cum - w)
    bt = kka_ref[...] * g_inv
    kt = k_ref[...] * g_inv
    rt = r_ref[...] * g_in
    bl = bt * g_last
    kl = kt * g_last
    v = v_ref[...]

    pairs = range(n_pairs)
    lanes = [slice(p * LANE, (p + 1) * LANE) for p in pairs]

    def stack(x, p):
        xp = x[:, lanes[p]]
        return bf(jnp.concatenate([jnp.where(left, xp, 0.0), jnp.where(left, 0.0, xp)], axis=0))

    a2 = [stack(at, p) for p in pairs]
    r2 = [stack(rt, p) for p in pairs]
    b2 = [stack(bt, p) for p in pairs]
    k2 = [stack(kt, p) for p in pairs]
    v2 = [stack(v, p) for p in pairs]
    sc = [nt(jnp.concatenate([a2[p], r2[p]], axis=0), jnp.concatenate([b2[p], k2[p]], axis=0)) for p in pairs]
    a_ab = [jnp.where(strict, sc[p][0:P, 0:P], 0.0) for p in pairs]
    a_ak = [bf(jnp.where(strict, sc[p][0:P, P:2 * P], 0.0)) for p in pairs]
    a_rb = [bf(jnp.where(lower, sc[p][P:2 * P, 0:P], 0.0)) for p in pairs]
    a_rk = [bf(jnp.where(lower, sc[p][P:2 * P, P:2 * P], 0.0)) for p in pairs]
    pw = [bf(jnp.where(base, a_ab[p], 0.0)) for p in pairs]
    t_inv = [eye + pw[p].astype(F32) for p in pairs]
    m = 2
    while m < SOLVE_BASE:
        pw = [bf(_bdot(pw[p], pw[p])) for p in pairs]
        t_inv = [t_inv[p] + _bdot(bf(t_inv[p]), pw[p]) for p in pairs]
        m *= 2
    for lv in levels:
        half = [bf(_bdot(bf(t_inv[p]), bf(jnp.where(lv, a_ab[p], 0.0)))) for p in pairs]
        t_inv = [t_inv[p] + _bdot(half[p], bf(t_inv[p])) for p in pairs]
    s = [s_ref[p] for p in pairs]
    s_b = [bf(s[p]) for p in pairs]
    wmat = [bf(nt(a2[p], s_b[p]) + _bdot(a_ak[p], v2[p])) for p in pairs]
    u = [bf(_bdot(bf(t_inv[p]), wmat[p])) for p in pairs]
    for p in pairs:
        o2 = nt(r2[p], s_b[p]) + _bdot(a_rb[p], u[p]) + _bdot(a_rk[p], v2[p])
        o_ref[:, lanes[p]] = o2[0:L, :] + o2[L:P, :]
    for p in pairs:
        upd = tn(jnp.concatenate([u[p], v2[p]], axis=0),
                 jnp.concatenate([stack(bl, p), stack(kl, p)], axis=0))
        s_ref[p] = s[p] * g_last[:, lanes[p]] + upd

    @pl.when(c == pl.num_programs(1) - 1)
    def _():
        for p in range(n_pairs):
            s = s_ref[p]
            s_out_ref[2 * p] = s[0:A_HEAD_DIM, 0:A_HEAD_DIM]
            s_out_ref[2 * p + 1] = s[A_HEAD_DIM:LANE, A_HEAD_DIM:LANE]


def _rwkv_scan_chunked(ops, s0, B, T):
    ops3 = [a.reshape(B, T, A_WIDTH) for a in ops]
    tspec = pl.BlockSpec((None, CHUNK, A_WIDTH), lambda b, c: (b, c, 0))
    sspec = pl.BlockSpec((None, A_HEADS, A_HEAD_DIM, A_HEAD_DIM), lambda b, c: (b, 0, 0, 0))
    o, s_t = pl.pallas_call(
        _rwkv_chunk_kernel,
        grid=(B, T // CHUNK),
        in_specs=[tspec] * 6 + [sspec],
        out_specs=[tspec, sspec],
        out_shape=[jax.ShapeDtypeStruct((B, T, A_WIDTH), F32),
                   jax.ShapeDtypeStruct((B, A_HEADS, A_HEAD_DIM, A_HEAD_DIM), F32)],
        scratch_shapes=[pltpu.VMEM((A_HEADS // 2, LANE, LANE), F32)],
        compiler_params=_cparams(2),
        name="rwkv_chunk",
    )(*ops3, s0)
    return o.reshape(B * T, A_WIDTH), s_t


def _rwkv_post_kernel(o_ref, bonus_ref, g_ref, lnw_ref, lnb_ref, bsel_ref, y_ref, *, o_feature_major):
    o = o_ref[...].T if o_feature_major else o_ref[...]
    bsel = bsel_ref[...]
    inv_n = 1.0 / A_HEAD_DIM
    mean = _seg_sum(o, bsel) * inv_n
    c = o - mean
    var = _seg_sum(c * c, bsel) * inv_n
    y = c * lax.rsqrt(var + GN_EPS) * lnw_ref[...] + lnb_ref[...]
    y_ref[...] = ((y + bonus_ref[...]) * g_ref[...]).astype(BF16)


def _rwkv_post(o, bonus, g, lw, tm, o_feature_major=False):
    M = bonus.shape[0]
    tile = pl.BlockSpec((tm, A_WIDTH), lambda i: (i, 0))
    o_tile = pl.BlockSpec((A_WIDTH, tm), lambda i: (0, i)) if o_feature_major else tile
    l = lw["l"]
    row = pl.BlockSpec((None, 1, A_WIDTH), lambda i: (l, 0, 0))
    return pl.pallas_call(
        functools.partial(_rwkv_post_kernel, o_feature_major=o_feature_major),
        grid=(M // tm,),
        in_specs=[o_tile, tile, tile, row, row, pl.BlockSpec((SEG, SEG), lambda i: (0, 0))],
        out_specs=tile,
        out_shape=jax.ShapeDtypeStruct((M, A_WIDTH), BF16),
        compiler_params=_cparams(1),
        name="rwkv_post",
    )(o, bonus, g, lw["ln_w"], lw["ln_b"], lw["bsel"])


def _rope_kernel(x_ref, c_ref, s1_ref, s2_ref, o_ref):
    x = x_ref[...]
    fwd = pltpu.roll(x, x.shape[1] - ROPE_HALF, axis=1)
    bwd = pltpu.roll(x, ROPE_HALF, axis=1)
    o_ref[...] = x * c_ref[...] + fwd * s1_ref[...] + bwd * s2_ref[...]


def _rope(proj, tables, tm):
    M = proj.shape[0]
    n_tab = tables[0].shape[0] // tm
    tab = pl.BlockSpec((tm, SEG), lambda i, j: (i % n_tab, 0))
    return pl.pallas_call(
        _rope_kernel,
        grid=(M // tm, QK_W // SEG),
        in_specs=[pl.BlockSpec((tm, SEG), lambda i, j: (i, Q_OFF // SEG + j)), tab, tab, tab],
        out_specs=pl.BlockSpec((tm, SEG), lambda i, j: (i, j)),
        out_shape=jax.ShapeDtypeStruct((M, QK_W), F32),
        compiler_params=_cparams(2),
        name="rope",
    )(proj, *tables)


def _swa_prompt_kernel(q_ref, kc_ref, kp_ref, vc_ref, vp_ref, sink_ref, o_ref):
    n = pl.program_id(1)
    rows = GROUP * WINDOW
    i = lax.broadcasted_iota(jnp.int32, (rows, 2 * WINDOW), 0) % WINDOW
    j = lax.broadcasted_iota(jnp.int32, (rows, 2 * WINDOW), 1)
    diff = WINDOW + i - j
    mask = (diff >= 0) & (diff < WINDOW) & ((n > 0) | (j >= WINDOW))
    q = q_ref[...].astype(BF16)
    kc = kc_ref[...].astype(BF16)
    kp = kp_ref[...].astype(BF16)
    vc = vc_ref[...].astype(BF16)
    vp = vp_ref[...].astype(BF16)
    khs = range(KV_HEADS)
    ksl = [slice(kh * HEAD_DIM, (kh + 1) * HEAD_DIM) for kh in khs]
    kk = [jnp.concatenate([kp[:, ksl[kh]], kc[:, ksl[kh]]], axis=0) for kh in khs]
    vv = [jnp.concatenate([vp[:, ksl[kh]], vc[:, ksl[kh]]], axis=0) for kh in khs]
    qh = [jnp.concatenate([q[:, (kh * GROUP + g) * HEAD_DIM:(kh * GROUP + g + 1) * HEAD_DIM] for g in range(GROUP)],
                          axis=0) for kh in khs]
    s = [lax.dot_general(qh[kh], kk[kh], (((1,), (1,)), ((), ())), preferred_element_type=F32) for kh in khs]
    s = [jnp.where(mask, s[kh] * (HEAD_DIM ** -0.5), -jnp.inf) for kh in khs]
    m = [jnp.maximum(jnp.max(s[kh], axis=1, keepdims=True), sink_ref[kh]) for kh in khs]
    e = [jnp.exp(s[kh] - m[kh]) for kh in khs]
    denom = [jnp.sum(e[kh], axis=1, keepdims=True) + jnp.exp(sink_ref[kh] - m[kh]) for kh in khs]
    o = [_bdot((e[kh] / denom[kh]).astype(BF16), vv[kh]) for kh in khs]
    outs = [o[kh][g * WINDOW:(g + 1) * WINDOW, :] for kh in khs for g in range(GROUP)]
    o_ref[...] = jnp.concatenate(outs, axis=1).astype(o_ref.dtype)


def _swa_prompt(qk_rot, proj, lw, B, T):
    nb = T // WINDOW
    rows = GROUP * WINDOW
    kcol = Q_W // KV_W
    vcol = V_OFF // KV_W
    l = lw["l"]
    cur = lambda col: (lambda b, n: (b * nb + n, col))
    prev = lambda col: (lambda b, n: (b * nb + jnp.maximum(n - 1, 0), col))
    return pl.pallas_call(
        _swa_prompt_kernel,
        grid=(B, nb),
        in_specs=[pl.BlockSpec((WINDOW, Q_W), cur(0)),
                  pl.BlockSpec((WINDOW, KV_W), cur(kcol)),
                  pl.BlockSpec((WINDOW, KV_W), prev(kcol)),
                  pl.BlockSpec((WINDOW, KV_W), cur(vcol)),
                  pl.BlockSpec((WINDOW, KV_W), prev(vcol)),
                  pl.BlockSpec((None, KV_HEADS, rows, 1), lambda b, n: (l, 0, 0, 0))],
        out_specs=pl.BlockSpec((WINDOW, Q_W), cur(0)),
        out_shape=jax.ShapeDtypeStruct((B * T, Q_W), BF16),
        compiler_params=_cparams(2),
        name="swa_prompt",
    )(qk_rot, qk_rot, qk_rot, proj, proj, lw["sink_p"])


def _swa_sample_kernel(q_ref, kn_ref, vn_ref, ck_ref, cv_ref, sink_ref, *rest, steps, n_prev):
    if n_prev:
        pk_ref, pv_ref, o_ref, nk_ref, nv_ref = rest
        for li in range(n_prev):
            nk_ref[li] = pk_ref[li]
            nv_ref[li] = pv_ref[li]
    else:
        o_ref, nk_ref, nv_ref = rest
    first_new = SUBLANE - steps
    new_rows = lax.broadcasted_iota(jnp.int32, (SUBLANE, KV_W), 0) >= first_new
    for b in range(q_ref.shape[0]):
        for c_ref, n_ref, dst in ((ck_ref, kn_ref, nk_ref), (cv_ref, vn_ref, nv_ref)):
            rolled = pltpu.roll(c_ref[b], WINDOW - steps, axis=0)
            tail = jnp.where(new_rows, n_ref[b], rolled[WINDOW - SUBLANE:, :])
            dst[n_prev, b] = jnp.concatenate([rolled[:WINDOW - SUBLANE, :], tail], axis=0)
    rows = GROUP * steps
    t_row = lax.broadcasted_iota(jnp.int32, (rows, WINDOW), 0) % steps
    c_col = lax.broadcasted_iota(jnp.int32, (rows, WINDOW), 1)
    cache_mask = c_col > t_row
    t_col = lax.broadcasted_iota(jnp.int32, (rows, 1), 0) % steps
    scale = HEAD_DIM ** -0.5
    probs = []
    units = [(b, kh) for b in range(q_ref.shape[0]) for kh in range(KV_HEADS)]
    ksl = [slice(kh * HEAD_DIM, (kh + 1) * HEAD_DIM) for kh in range(KV_HEADS)]
    qh = {u: q_ref[u[0], u[1]].astype(BF16) for u in units}
    sc = {u: lax.dot_general(qh[u], ck_ref[u[0], :, ksl[u[1]]].astype(BF16), (((1,), (1,)), ((), ())),
                             preferred_element_type=F32) for u in units}
    s_c, s_n, m_c = {}, {}, {}
    for u in units:
        b, kh = u
        kn = kn_ref[b, first_new:, :].astype(BF16).astype(F32)
        s_c[u] = jnp.where(cache_mask, sc[u] * scale, -jnp.inf)
        qf = qh[u].astype(F32)
        s_n[u] = [jnp.where(t_col >= t, jnp.sum(qf * kn[t:t + 1, ksl[kh]], axis=1, keepdims=True) * scale, -jnp.inf)
                  for t in range(steps)]
        m_c[u] = jnp.max(s_c[u], axis=1, keepdims=True)
    for u in units:
        sink = sink_ref[u[1]]
        m = jnp.maximum(m_c[u], sink)
        for d in s_n[u]:
            m = jnp.maximum(m, d)
        ec = jnp.exp(s_c[u] - m)
        en = [jnp.exp(d - m) for d in s_n[u]]
        denom = jnp.sum(ec, axis=1, keepdims=True) + jnp.exp(sink - m)
        for e in en:
            denom = denom + e
        probs.append(((ec / denom).astype(BF16), [(e / denom).astype(BF16).astype(F32) for e in en]))
    for u, (pc, pn) in zip(units, probs):
        b, kh = u
        vn = vn_ref[b, first_new:, :].astype(BF16).astype(F32)
        o = _bdot(pc, cv_ref[b, :, ksl[kh]].astype(BF16))
        for t in range(steps):
            o = o + pn[t] * vn[t:t + 1, ksl[kh]]
        o_ref[b, kh] = o


def _swa_sample(qk_rot, proj, cache_k, cache_v, prev_k, prev_v, lw, B, T):
    l = lw["l"]
    rows = GROUP * T
    qg = qk_rot[:, :Q_W].reshape(T, B, KV_HEADS, GROUP, HEAD_DIM).transpose(1, 2, 3, 0, 4)
    qg = qg.reshape(B, KV_HEADS, rows, HEAD_DIM)
    pad = ((0, 0), (SUBLANE - T, 0), (0, 0))
    kn = jnp.pad(qk_rot[:, Q_W:].reshape(T, B, KV_W).transpose(1, 0, 2), pad)
    vn = jnp.pad(proj[:, V_OFF:V_OFF + KV_W].reshape(T, B, KV_W).transpose(1, 0, 2), pad)
    nb = SUBLANE if B % SUBLANE == 0 else 1
    new_spec = pl.BlockSpec((nb, SUBLANE, KV_W), lambda b: (b, 0, 0))
    cache_spec = pl.BlockSpec((None, nb, WINDOW, KV_W), lambda b: (l, b, 0, 0))
    in_specs = [pl.BlockSpec((nb, KV_HEADS, rows, HEAD_DIM), lambda b: (b, 0, 0, 0)),
                new_spec, new_spec, cache_spec, cache_spec,
                pl.BlockSpec((None, KV_HEADS, rows, 1), lambda b: (l, 0, 0, 0))]
    args = [qg, kn, vn, cache_k, cache_v, lw["sink_s"]]
    if l:
        in_specs += [pl.BlockSpec((l, nb, WINDOW, KV_W), lambda b: (0, b, 0, 0))] * 2
        args += [prev_k, prev_v]
    stacked = pl.BlockSpec((l + 1, nb, WINDOW, KV_W), lambda b: (0, b, 0, 0))
    stacked_shape = jax.ShapeDtypeStruct((l + 1, B, WINDOW, KV_W), F32)
    o, new_k, new_v = pl.pallas_call(
        functools.partial(_swa_sample_kernel, steps=T, n_prev=l),
        grid=(B // nb,),
        in_specs=in_specs,
        out_specs=[pl.BlockSpec((nb, KV_HEADS, rows, HEAD_DIM), lambda b: (b, 0, 0, 0)), stacked, stacked],
        out_shape=[jax.ShapeDtypeStruct((B, KV_HEADS, rows, HEAD_DIM), F32), stacked_shape, stacked_shape],
        compiler_params=_cparams(1),
        name="swa_sample",
    )(*args)
    o = o.reshape(B, KV_HEADS, GROUP, T, HEAD_DIM).transpose(3, 0, 1, 2, 4).reshape(T * B, Q_W)
    return o.astype(BF16), new_k, new_v


def _rope_tables(pos):
    inv = 1.0 / (ROPE_THETA ** (jnp.arange(0, ROPE_DIM, 2, dtype=F32) / ROPE_DIM))
    ang = pos.astype(F32)[:, None] * inv[None, :]
    cos, sin = jnp.cos(ang), jnp.sin(ang)
    n = pos.shape[0]
    ones = jnp.ones((n, HEAD_DIM - ROPE_DIM), F32)
    zeros = jnp.zeros((n, HEAD_DIM - ROPE_DIM), F32)
    zh = jnp.zeros((n, ROPE_HALF), F32)
    c = jnp.concatenate([cos, cos, ones], axis=1)
    s1 = jnp.concatenate([-sin, zh, zeros], axis=1)
    s2 = jnp.concatenate([zh, sin, zeros], axis=1)
    return tuple(jnp.tile(a, (1, SEG // HEAD_DIM)) for a in (c, s1, s2))


def _stacked_params(t_sample, norm_g, ffn_w_gate, ffn_w_up, ffn_w_down, w_in, rwkv_mu, rwkv_w0, rwkv_w2,
                    rwkv_a0, rwkv_a2, rwkv_g2, rwkv_k_k, rwkv_k_a, rwkv_r_k, rwkv_ln_w, rwkv_ln_b, attn_sinks,
                    w_branch_a, w_branch_b, w_merge_gate, w_out):
    L = w_in.shape[0]
    w_proj = jnp.concatenate([w_in[:, :, :A_PROJ], jnp.zeros((L, D_MODEL, A_PAD - A_PROJ), w_in.dtype),
                              w_in[:, :, A_PROJ:]], axis=2)
    lora_rows = lambda w, off: jnp.pad(w, ((0, 0), (off, LORA_PAD - off - w.shape[1]), (0, 0))).astype(BF16)
    seg = jnp.arange(SEG) // A_HEAD_DIM
    row = lambda a: a.reshape(L, 1, -1)
    sink_rows = lambda n: jnp.repeat(attn_sinks.reshape(L, KV_HEADS, GROUP), n, axis=2).reshape(L, KV_HEADS, GROUP * n, 1)
    return {
        "norm_g": norm_g, "wg": ffn_w_gate, "wu": ffn_w_up, "wd": ffn_w_down, "w_proj": w_proj,
        "mu": row(jnp.pad(rwkv_mu, ((0, 0), (0, A_PAD - A_PROJ)))),
        "w0": row(rwkv_w0), "a0": row(rwkv_a0), "k_k": row(rwkv_k_k), "k_a": row(rwkv_k_a),
        "r_k": row(rwkv_r_k), "ln_w": row(rwkv_ln_w), "ln_b": row(rwkv_ln_b),
        "w2": lora_rows(rwkv_w2, 0), "a2": lora_rows(rwkv_a2, DECAY_LORA),
        "g2": lora_rows(rwkv_g2, DECAY_LORA + ICLR_LORA),
        "bsel": (seg[:, None] == seg[None, :]).astype(BF16),
        "sink_p": sink_rows(WINDOW), "sink_s": sink_rows(t_sample),
        "w_a": w_branch_a, "w_b": w_branch_b, "w_gate": w_merge_gate, "w_out": w_out,
    }


def _decoder_layer(x, mod, lw, B, T, shift_prev, tables, sample=None):
    M = B * T
    l = lw["l"]
    sp = jnp.pad(shift_prev, ((0, 0), (0, A_PAD - A_PROJ)))
    if sample is not None:
        tm = M
        tm_prep = B
        mod = mod.reshape(1, B, N_MOD * D_MODEL)
        gmap = lambda tile: (lambda i: 0)
        sp = sp.reshape(1, B, A_PAD)
    else:
        tm = min(T, 1024)
        tm_prep = min(tm, 256)
        mod = mod.reshape(B, 1, N_MOD * D_MODEL)
        gmap = lambda tile: (lambda i: i // (T // tile))
        sp = sp.reshape(B, 1, A_PAD)
    tm_k = min(tm, 512)

    def ffn(x, n, which):
        h = _ada_norm(x, lw["norm_g"], l, n, mod, gmap(tm), tm)
        act = _ffn_gate_up(h, lw["wg"], lw["wu"], (l, which), tm)
        return _mm_residual(act, lw["wd"], (l, which), x, mod, 3 * n + 2, gmap(tm_k), 0.5, tm_k)

    x = ffn(x, 0, 0)

    h = _ada_norm(x, lw["norm_g"], l, 1, mod, gmap(tm), tm)
    proj = _mm_plain(h, lw["w_proj"], (l,), tm, name="in_proj")
    gates = _mm_plain(h, lw["w_gate"], (l,), tm, sigmoid=True, name="merge_gates")

    r, w, k2, v, kk, kka, g, bonus = _rwkv_prep(proj, sp, lw, T, tm_prep, sample is not None)
    qk_rot = _rope(proj, tables, min(tm, 512))
    if sample is not None:
        state_wkv, cache_k, cache_v, prev_wkv, prev_k, prev_v = sample
        o, new_wkv = _rwkv_lane_scan((r, w, k2, v, kk, kka), state_wkv, prev_wkv, l, B, T)
        ya = _rwkv_post(o, bonus, g, lw, B, o_feature_major=True)
        new_shift = proj[(T - 1) * B:, :A_PROJ]
        yb, new_k, new_v = _swa_sample(qk_rot, proj, cache_k, cache_v, prev_k, prev_v, lw, B, T)
    else:
        zero_wkv = jnp.zeros((B, A_HEADS, A_HEAD_DIM, A_HEAD_DIM), F32)
        o, new_wkv = _rwkv_scan_chunked((r, w, k2, v, kk, kka), zero_wkv, B, T)
        ya = _rwkv_post(o, bonus, g, lw, min(tm, 512))
        new_shift = proj.reshape(B, T, PROJ_W)[:, -1, :A_PROJ]
        yb = _swa_prompt(qk_rot, proj, lw, B, T)
        new_k = qk_rot[:, Q_W:].reshape(B, T, KV_HEADS, HEAD_DIM)[:, -WINDOW:]
        new_v = proj[:, V_OFF:V_OFF + KV_W].reshape(B, T, KV_HEADS, HEAD_DIM)[:, -WINDOW:]

    merged = _mm_merge(ya, yb, lw["w_a"], lw["w_b"], (l,), gates, tm)
    x = _mm_residual(merged, lw["w_out"], (l,), x, mod, 5, gmap(tm), 1.0, tm)

    x = ffn(x, 2, 1)
    return x, new_shift, new_wkv, new_k, new_v


def kernel(x_prompt, x_sample, c_prompt, c_sample, state_shift, state_wkv, cache_k, cache_v, norm_g, w_mod, b_mod,
           ffn_w_gate, ffn_w_up, ffn_w_down, w_in, rwkv_mu, rwkv_w0, rwkv_w2, rwkv_a0, rwkv_a2, rwkv_g2, rwkv_k_k,
           rwkv_k_a, rwkv_r_k, rwkv_ln_w, rwkv_ln_b, attn_sinks, w_branch_a, w_branch_b, w_merge_gate, w_out,
           final_norm):
    Bp, Tp, _ = x_prompt.shape
    Bs, Ts, _ = x_sample.shape
    L = w_in.shape[0]
    tab_p = _rope_tables(jnp.arange(Tp))
    tab_s = _rope_tables(jnp.repeat(PAST_LEN + jnp.arange(Ts), Bs))
    zero_shift = jnp.zeros((Bp, A_PROJ), F32)
    xp = x_prompt.reshape(Bp * Tp, D_MODEL)
    xs = x_sample.transpose(1, 0, 2).reshape(Ts * Bs, D_MODEL)
    c_all = jnp.concatenate([c_prompt, c_sample], axis=0)
    pad_rows = (-c_all.shape[0]) % 16
    c_all = jnp.pad(c_all, ((0, pad_rows), (0, 0)))
    pw = _stacked_params(Ts, norm_g, ffn_w_gate, ffn_w_up, ffn_w_down, w_in, rwkv_mu, rwkv_w0, rwkv_w2, rwkv_a0,
                         rwkv_a2, rwkv_g2, rwkv_k_k, rwkv_k_a, rwkv_r_k, rwkv_ln_w, rwkv_ln_b, attn_sinks,
                         w_branch_a, w_branch_b, w_merge_gate, w_out)
    state_s = state_wkv.reshape(L, Bs, A_HEADS * A_HEAD_DIM * A_HEAD_DIM)
    ck = cache_k.reshape(L, Bs, WINDOW, KV_W)
    cv = cache_v.reshape(L, Bs, WINDOW, KV_W)
    outs_p, shifts_s = [], []
    wkv_s = k_s = v_s = None
    for l in range(L):
        lw = dict(pw, l=l)
        mod = _modulation(c_all, w_mod, b_mod, l)
        xp, *st_p = _decoder_layer(xp, mod[:Bp], lw, Bp, Tp, zero_shift, tab_p)
        xs, shift_s, wkv_s, k_s, v_s = _decoder_layer(xs, mod[Bp:Bp + Bs], lw, Bs, Ts, state_shift[l], tab_s,
                                                      sample=(state_s, ck, cv, wkv_s, k_s, v_s))
        outs_p.append(st_p)
        shifts_s.append(shift_s)
    y_prompt = _final_norm(xp, final_norm, 1024).reshape(Bp, Tp, D_MODEL)
    y_sample = _final_norm(xs, final_norm, Bs * Ts).reshape(Ts, Bs, D_MODEL).transpose(1, 0, 2)
    stack = lambda n: jnp.stack([o[n] for o in outs_p])
    return (y_prompt, y_sample, stack(0), stack(1), stack(2), stack(3),
            jnp.stack(shifts_s), wkv_s.reshape(state_wkv.shape), k_s.reshape(cache_k.shape),
            v_s.reshape(cache_v.shape))
```

```python
import functools

import jax
import jax.numpy as jnp
from jax import lax
from jax.experimental import pallas as pl
from jax.experimental.pallas import tpu as pltpu

D_MODEL = 2048
DEPTH = 2
A_HEADS = 16
A_HEAD_DIM = 64
A_WIDTH = A_HEADS * A_HEAD_DIM
DECAY_LORA = 64
ICLR_LORA = 64
GATE_LORA = 160
LORA_W = DECAY_LORA + ICLR_LORA + GATE_LORA
A_PROJ = 3 * A_WIDTH + LORA_W
GN_EPS = 64e-5
Q_HEADS = 16
KV_HEADS = 4
GROUP = Q_HEADS // KV_HEADS
HEAD_DIM = 64
Q_W = Q_HEADS * HEAD_DIM
KV_W = KV_HEADS * HEAD_DIM
WINDOW = 128
ROPE_DIM = HEAD_DIM // 4
ROPE_HALF = ROPE_DIM // 2
ROPE_THETA = 500000.0
D_FF = 5632
N_MOD = 9
RMS_EPS = 1e-6

PAST_LEN = 8192
LANE = 128
SUBLANE = 8
CHUNK = 64
SOLVE_BASE = 8
SEG = 256
A_PAD = 3584
LORA_PAD = A_PAD - 3 * A_WIDTH
Q_OFF = A_PAD
K_OFF = Q_OFF + Q_W
V_OFF = K_OFF + KV_W
PROJ_W = V_OFF + KV_W
QK_W = Q_W + KV_W
VMEM_LIMIT = 56 * 1024 * 1024
TN_WIDE = 1024

F32 = jnp.float32
BF16 = jnp.bfloat16


def _cparams(n_axes):
    return pltpu.CompilerParams(dimension_semantics=("arbitrary",) * n_axes,
                                vmem_limit_bytes=VMEM_LIMIT)


def _bdot(a, b):
    return jnp.dot(a, b, preferred_element_type=F32)


def _seg_sum(x, bsel):
    hi = x.astype(BF16)
    lo = (x - hi.astype(F32)).astype(BF16)
    parts = []
    for c in range(x.shape[1] // SEG):
        sl = slice(c * SEG, (c + 1) * SEG)
        parts.append(_bdot(hi[:, sl], bsel) + _bdot(lo[:, sl], bsel))
    return jnp.concatenate(parts, axis=1)


def _mod_kernel(c_ref, w_ref, b_ref, o_ref):
    c = c_ref[...]
    a = (c * jax.nn.sigmoid(c)).astype(BF16)
    o_ref[...] = _bdot(a, w_ref[...].astype(BF16)) + b_ref[...]


def _modulation(c, w_mod, b_mod, l):
    R = c.shape[0]
    N = w_mod.shape[-1]
    tn = 1024
    return pl.pallas_call(
        _mod_kernel,
        grid=(N // tn,),
        in_specs=[pl.BlockSpec((R, D_MODEL), lambda j: (0, 0)),
                  pl.BlockSpec((None, D_MODEL, tn), lambda j: (l, 0, j)),
                  pl.BlockSpec((None, 1, tn), lambda j: (l, 0, j))],
        out_specs=pl.BlockSpec((R, tn), lambda j: (0, j)),
        out_shape=jax.ShapeDtypeStruct((R, N), F32),
        compiler_params=_cparams(1),
        name="modulation",
    )(c, w_mod, b_mod.reshape(b_mod.shape[0], 1, N))


def _mod_rows(ref, tm):
    val = ref[...]
    reps = tm // val.shape[0]
    return val if val.shape[0] == 1 or reps == 1 else jnp.concatenate([val] * reps, axis=0)


def _ada_norm_kernel(x_ref, g_ref, sh_ref, sc_ref, o_ref):
    x = x_ref[...]
    tm = x.shape[0]
    y = x * lax.rsqrt(jnp.mean(x * x, axis=-1, keepdims=True) + RMS_EPS)
    y = y * g_ref[...]
    o_ref[...] = (y * (1.0 + _mod_rows(sc_ref, tm)) + _mod_rows(sh_ref, tm)).astype(o_ref.dtype)


def _ada_norm(x, norm_g, l, n, mod, gmap, tm):
    M = x.shape[0]
    R = mod.shape[1]
    mspec = lambda col: pl.BlockSpec((None, R, D_MODEL), lambda i: (gmap(i), 0, col))
    return pl.pallas_call(
        _ada_norm_kernel,
        grid=(M // tm,),
        in_specs=[pl.BlockSpec((tm, D_MODEL), lambda i: (i, 0)),
                  pl.BlockSpec((None, None, 1, D_MODEL), lambda i: (l, n, 0, 0)),
                  mspec(3 * n), mspec(3 * n + 1)],
        out_specs=pl.BlockSpec((tm, D_MODEL), lambda i: (i, 0)),
        out_shape=jax.ShapeDtypeStruct((M, D_MODEL), BF16),
        compiler_params=_cparams(1),
        name="ada_norm",
    )(x, norm_g.reshape(norm_g.shape[0], norm_g.shape[1], 1, D_MODEL), mod, mod)


def _final_norm_kernel(x_ref, g_ref, o_ref):
    x = x_ref[...]
    y = x * lax.rsqrt(jnp.mean(x * x, axis=-1, keepdims=True) + RMS_EPS)
    o_ref[...] = y * g_ref[...]


def _final_norm(x, gain, tm):
    M = x.shape[0]
    return pl.pallas_call(
        _final_norm_kernel,
        grid=(M // tm,),
        in_specs=[pl.BlockSpec((tm, D_MODEL), lambda i: (i, 0)),
                  pl.BlockSpec((1, D_MODEL), lambda i: (0, 0))],
        out_specs=pl.BlockSpec((tm, D_MODEL), lambda i: (i, 0)),
        out_shape=jax.ShapeDtypeStruct((M, D_MODEL), F32),
        compiler_params=_cparams(1),
        name="final_norm",
    )(x, gain.reshape(1, D_MODEL))


def _wspec(w, lead, tn):
    return pl.BlockSpec((None,) * len(lead) + (w.shape[-2], tn), lambda j, i: lead + (0, j))


def _wdot(a, w_ref):
    return _bdot(a, w_ref[...].astype(BF16))


def _gu_kernel(h_ref, wg_ref, wu_ref, o_ref):
    h = h_ref[...]
    g = _wdot(h, wg_ref)
    u = _wdot(h, wu_ref)
    o_ref[...] = ((g * jax.nn.sigmoid(g)) * u).astype(o_ref.dtype)


def _ffn_gate_up(h, wg, wu, lead, tm, tn=512):
    M, K = h.shape
    N = wg.shape[-1]
    return pl.pallas_call(
        _gu_kernel,
        grid=(N // tn, M // tm),
        in_specs=[pl.BlockSpec((tm, K), lambda j, i: (i, 0)), _wspec(wg, lead, tn), _wspec(wu, lead, tn)],
        out_specs=pl.BlockSpec((tm, tn), lambda j, i: (i, j)),
        out_shape=jax.ShapeDtypeStruct((M, N), BF16),
        compiler_params=_cparams(2),
        name="ffn_gate_up",
    )(h, wg, wu)


def _resid_kernel(a_ref, w_ref, x_ref, gate_ref, o_ref, *, coef):
    y = _wdot(a_ref[...], w_ref)
    o_ref[...] = x_ref[...] + (coef * _mod_rows(gate_ref, y.shape[0])) * y


def _mm_residual(a, w, lead, x, mod, n_gate, gmap, coef, tm, tn=512):
    M, K = a.shape
    N = w.shape[-1]
    R = mod.shape[1]
    nj = N // tn
    return pl.pallas_call(
        functools.partial(_resid_kernel, coef=coef),
        grid=(nj, M // tm),
        in_specs=[pl.BlockSpec((tm, K), lambda j, i: (i, 0)),
                  _wspec(w, lead, tn),
                  pl.BlockSpec((tm, tn), lambda j, i: (i, j)),
                  pl.BlockSpec((None, R, tn), lambda j, i: (gmap(i), 0, n_gate * nj + j))],
        out_specs=pl.BlockSpec((tm, tn), lambda j, i: (i, j)),
        out_shape=jax.ShapeDtypeStruct((M, N), F32),
        compiler_params=_cparams(2),
        name="mm_residual",
    )(a, w, x, mod)


def _plain_kernel(a_ref, w_ref, o_ref, *, sigmoid):
    y = _wdot(a_ref[...], w_ref)
    if sigmoid:
        y = jax.nn.sigmoid(y)
    o_ref[...] = y.astype(o_ref.dtype)


def _mm_plain(a, w, lead, tm, tn=512, sigmoid=False, name="mm_plain"):
    M, K = a.shape
    N = w.shape[-1]
    return pl.pallas_call(
        functools.partial(_plain_kernel, sigmoid=sigmoid),
        grid=(N // tn, M // tm),
        in_specs=[pl.BlockSpec((tm, K), lambda j, i: (i, 0)), _wspec(w, lead, tn)],
        out_specs=pl.BlockSpec((tm, tn), lambda j, i: (i, j)),
        out_shape=jax.ShapeDtypeStruct((M, N), F32),
        compiler_params=_cparams(2),
        name=name,
    )(a, w)


def _merge_kernel(ya_ref, yb_ref, wa_ref, wb_ref, ga_ref, gb_ref, o_ref):
    pa = _wdot(ya_ref[...], wa_ref)
    pb = _wdot(yb_ref[...], wb_ref)
    o_ref[...] = (ga_ref[...] * pa + gb_ref[...] * pb).astype(o_ref.dtype)


def _mm_merge(ya, yb, wa, wb, lead, gates, tm, tn=512):
    M, K = ya.shape
    N = wa.shape[-1]
    nj = N // tn
    return pl.pallas_call(
        _merge_kernel,
        grid=(nj, M // tm),
        in_specs=[pl.BlockSpec((tm, K), lambda j, i: (i, 0)),
                  pl.BlockSpec((tm, K), lambda j, i: (i, 0)),
                  _wspec(wa, lead, tn), _wspec(wb, lead, tn),
                  pl.BlockSpec((tm, tn), lambda j, i: (i, j)),
                  pl.BlockSpec((tm, tn), lambda j, i: (i, j + nj))],
        out_specs=pl.BlockSpec((tm, tn), lambda j, i: (i, j)),
        out_shape=jax.ShapeDtypeStruct((M, N), BF16),
        compiler_params=_cparams(2),
        name="mm_merge",
    )(ya, yb, wa, wb, gates, gates)


def _rwkv_prep_kernel(p_ref, sp_ref, mu_ref, w0_ref, a0_ref, kk_ref, ka_ref, rk_ref,
                      w2_ref, a2_ref, g2_ref, bsel_ref,
                      r_out, d_out, k_out, v_out, kk_out, kka_out, g_out, bonus_out,
                      carry_ref, *, tiles_per_seq, time_major):
    t = pl.program_id(0)
    pa = p_ref[...]
    tm = pa.shape[0]
    if time_major:
        @pl.when(t == 0)
        def _():
            carry_ref[...] = sp_ref[...]

        prev = carry_ref[...]
        carry_ref[...] = pa
    else:
        @pl.when(t % tiles_per_seq == 0)
        def _():
            carry_ref[...] = sp_ref[...]

        row = lax.broadcasted_iota(jnp.int32, pa.shape, 0)
        prev = jnp.where(row == 0, carry_ref[...], pltpu.roll(pa, 1, axis=0))
        carry_ref[...] = pa[tm - 1:tm, :]

    def put(ref, val):
        ref[...] = val.T if time_major else val

    xm = pa + (prev - pa) * mu_ref[...]
    r = xm[:, 0:A_WIDTH]
    k = xm[:, A_WIDTH:2 * A_WIDTH]
    v = xm[:, 2 * A_WIDTH:3 * A_WIDTH]
    lora = xm[:, 3 * A_WIDTH:A_PAD]
    m_w = _bdot(jnp.tanh(lora).astype(BF16), w2_ref[...])
    m_a = _bdot(lora.astype(BF16), a2_ref[...])
    g = _bdot(jax.nn.sigmoid(lora).astype(BF16), g2_ref[...])
    z = -(w0_ref[...] + m_w)
    softplus = jnp.maximum(z, 0.0) + jnp.log1p(jnp.exp(-jnp.abs(z)))
    w_log = -softplus - 0.5
    a = jax.nn.sigmoid(a0_ref[...] + m_a)
    bsel = bsel_ref[...]
    kkf = k * kk_ref[...]
    norm = jnp.sqrt(_seg_sum(kkf * kkf, bsel))
    kk = kkf / jnp.maximum(norm, 1e-12)
    k2 = k * (1.0 + (a - 1.0) * ka_ref[...])
    put(r_out, r)
    put(d_out, -jnp.exp(w_log))
    put(k_out, k2)
    put(v_out, v)
    put(kk_out, kk)
    put(kka_out, kk * a)
    g_out[...] = g
    bonus_out[...] = _seg_sum(r * k2 * rk_ref[...], bsel) * v


def _rwkv_prep(proj, sp, lw, seq_len, tm, time_major):
    M = proj.shape[0]
    R = sp.shape[1]
    l = lw["l"]
    tiles_per_seq = 1 if time_major else seq_len // tm
    sp_map = (lambda i: (0, 0, 0)) if time_major else (lambda i: (i // tiles_per_seq, 0, 0))
    row = lambda n: pl.BlockSpec((None, 1, n), lambda i: (l, 0, 0))
    full = lambda a, b: pl.BlockSpec((None, a, b), lambda i: (l, 0, 0))
    tok_spec = pl.BlockSpec((tm, A_WIDTH), lambda i: (i, 0))
    tok_shape = jax.ShapeDtypeStruct((M, A_WIDTH), F32)
    if time_major:
        op_spec = pl.BlockSpec((A_WIDTH, tm), lambda i: (0, i))
        op_shape = jax.ShapeDtypeStruct((A_WIDTH, M), F32)
    else:
        op_spec, op_shape = tok_spec, tok_shape
    return pl.pallas_call(
        functools.partial(_rwkv_prep_kernel, tiles_per_seq=tiles_per_seq, time_major=time_major),
        grid=(M // tm,),
        in_specs=[pl.BlockSpec((tm, A_PAD), lambda i: (i, 0)),
                  pl.BlockSpec((None, R, A_PAD), sp_map),
                  row(A_PAD), row(A_WIDTH), row(A_WIDTH), row(A_WIDTH), row(A_WIDTH), row(A_WIDTH),
                  full(LORA_PAD, A_WIDTH), full(LORA_PAD, A_WIDTH), full(LORA_PAD, A_WIDTH),
                  pl.BlockSpec((SEG, SEG), lambda i: (0, 0))],
        out_specs=[op_spec] * 6 + [tok_spec] * 2,
        out_shape=[op_shape] * 6 + [tok_shape] * 2,
        scratch_shapes=[pltpu.VMEM((R, A_PAD), F32)],
        compiler_params=_cparams(1),
        name="rwkv_prep",
    )(proj, sp, lw["mu"], lw["w0"], lw["a0"], lw["k_k"], lw["k_a"], lw["r_k"],
      lw["w2"], lw["a2"], lw["g2"], lw["bsel"])


def _rwkv_lane_scan_kernel(r_ref, w_ref, k_ref, v_ref, kk_ref, kka_ref, s0_ref, *rest, steps, n_prev):
    if n_prev:
        prev_ref, o_ref, s_out_ref, st_ref = rest
        for li in range(n_prev):
            s_out_ref[li] = prev_ref[li]
    else:
        o_ref, s_out_ref, st_ref = rest
    n = A_HEAD_DIM
    nb = s0_ref.shape[0]
    st_ref[...] = s0_ref[...].T
    tsl = [slice(t * nb, (t + 1) * nb) for t in range(steps)]

    def group(g, carry):
        rows8 = pl.ds(pl.multiple_of(g * SUBLANE, SUBLANE), SUBLANE)
        v8 = [v_ref[rows8, tsl[t]] for t in range(steps)]
        o8 = [[] for _ in range(steps)]
        for u in range(SUBLANE):
            srows = pl.ds(pl.multiple_of((g * SUBLANE + u) * n, n), n)
            s = st_ref[srows, :]
            for t in range(steps):
                sk = jnp.sum(s * kk_ref[:, tsl[t]], axis=0, keepdims=True)
                s = s * jnp.exp(w_ref[:, tsl[t]]) - sk * kka_ref[:, tsl[t]] + v8[t][u:u + 1, :] * k_ref[:, tsl[t]]
                o8[t].append(jnp.sum(s * r_ref[:, tsl[t]], axis=0, keepdims=True))
            st_ref[srows, :] = s
        for t in range(steps):
            o_ref[rows8, tsl[t]] = jnp.concatenate(o8[t], axis=0)
        return carry

    lax.fori_loop(0, n // SUBLANE, group, 0)
    s_out_ref[n_prev] = st_ref[...].T


def _rwkv_lane_scan(ops_t, state, prev, l, B, T):
    n2 = A_HEAD_DIM * A_HEAD_DIM
    ospec = pl.BlockSpec((A_HEAD_DIM, T * B), lambda h: (h, 0))
    in_specs = [ospec] * 6 + [pl.BlockSpec((None, B, n2), lambda h: (l, 0, h))]
    args = list(ops_t) + [state]
    if l:
        in_specs.append(pl.BlockSpec((l, B, n2), lambda h: (0, 0, h)))
        args.append(prev)
    return pl.pallas_call(
        functools.partial(_rwkv_lane_scan_kernel, steps=T, n_prev=l),
        grid=(A_HEADS,),
        in_specs=in_specs,
        out_specs=[ospec, pl.BlockSpec((l + 1, B, n2), lambda h: (0, 0, h))],
        out_shape=[jax.ShapeDtypeStruct((A_WIDTH, T * B), F32),
                   jax.ShapeDtypeStruct((l + 1, B, A_HEADS * n2), F32)],
        scratch_shapes=[pltpu.VMEM((n2, B), F32)],
        compiler_params=_cparams(1),
        name="rwkv_lane_scan",
    )(*args)


def _rwkv_chunk_kernel(r_ref, w_ref, k_ref, v_ref, kk_ref, kka_ref, bonus_ref, g_ref, lnw_ref, lnb_ref, bsel_ref,
                       s0_ref, y_ref, s_out_ref, s_ref, o_ref):
    c = pl.program_id(1)
    L = CHUNK
    P = 2 * L
    n_pairs = A_HEADS // 2
    left = lax.broadcasted_iota(jnp.int32, (L, LANE), 1) < A_HEAD_DIM
    row = lax.broadcasted_iota(jnp.int32, (P, P), 0)
    col = lax.broadcasted_iota(jnp.int32, (P, P), 1)
    strict = row > col
    lower = row >= col
    eye = (row == col).astype(F32)
    same = lambda n: (row // n) == (col // n)
    base = same(SOLVE_BASE)
    levels = []
    n = SOLVE_BASE
    while n < L:
        levels.append(same(2 * n) & jnp.logical_not(same(n)))
        n *= 2
    bf = lambda x: x.astype(BF16)
    nt = lambda a, b: lax.dot_general(a, b, (((1,), (1,)), ((), ())), preferred_element_type=F32)
    tn = lambda a, b: lax.dot_general(a, b, (((0,), (0,)), ((), ())), preferred_element_type=F32)

    @pl.when(c == 0)
    def _():
        z = jnp.zeros((A_HEAD_DIM, A_HEAD_DIM), F32)
        for p in range(n_pairs):
            s_ref[p] = jnp.concatenate([jnp.concatenate([s0_ref[2 * p], z], axis=1),
                                        jnp.concatenate([z, s0_ref[2 * p + 1]], axis=1)], axis=0)

    w = w_ref[...]
    tl = lax.broadcasted_iota(jnp.int32, (L, L), 0) >= lax.broadcasted_iota(jnp.int32, (L, L), 1)
    tri = tl.astype(BF16)
    w_hi = bf(w)
    w_r1 = w - w_hi.astype(F32)
    w_mid = bf(w_r1)
    w_lo = bf(w_r1 - w_mid.astype(F32))
    cum = _bdot(tri, w_hi) + _bdot(tri, w_mid) + _bdot(tri, w_lo)
    g_in = jnp.exp(cum)
    g_inv = jnp.exp(-cum)
    g_last = g_in[L - 1:L, :]
    at = -kk_ref[...] * jnp.exp(cum - w)
    bt = kka_ref[...] * g_inv
    kt = k_ref[...] * g_inv
    rt = r_ref[...] * g_in
    bl = bt * g_last
    kl = kt * g_last
    v = v_ref[...]

    pairs = range(n_pairs)
    lanes = [slice(p * LANE, (p + 1) * LANE) for p in pairs]

    def stack(x, p):
        xp = x[:, lanes[p]]
        return bf(jnp.concatenate([jnp.where(left, xp, 0.0), jnp.where(left, 0.0, xp)], axis=0))

    a2 = [stack(at, p) for p in pairs]
    r2 = [stack(rt, p) for p in pairs]
    b2 = [stack(bt, p) for p in pairs]
    k2 = [stack(kt, p) for p in pairs]
    v2 = [stack(v, p) for p in pairs]
    sc = [nt(jnp.concatenate([a2[p], r2[p]], axis=0), jnp.concatenate([b2[p], k2[p]], axis=0)) for p in pairs]
    a_ab = [jnp.where(strict, sc[p][0:P, 0:P], 0.0) for p in pairs]
    a_ak = [bf(jnp.where(strict, sc[p][0:P, P:2 * P], 0.0)) for p in pairs]
    a_rb = [bf(jnp.where(lower, sc[p][P:2 * P, 0:P], 0.0)) for p in pairs]
    a_rk = [bf(jnp.where(lower, sc[p][P:2 * P, P:2 * P], 0.0)) for p in pairs]
    pw = [bf(jnp.where(base, a_ab[p], 0.0)) for p in pairs]
    t_inv = [eye + pw[p].astype(F32) for p in pairs]
    m = 2
    while m < SOLVE_BASE:
        pw = [bf(_bdot(pw[p], pw[p])) for p in pairs]
        t_inv = [t_inv[p] + _bdot(bf(t_inv[p]), pw[p]) for p in pairs]
        m *= 2
    for lv in levels:
        half = [bf(_bdot(bf(t_inv[p]), bf(jnp.where(lv, a_ab[p], 0.0)))) for p in pairs]
        t_inv = [t_inv[p] + _bdot(half[p], bf(t_inv[p])) for p in pairs]
    s = [s_ref[p] for p in pairs]
    s_b = [bf(s[p]) for p in pairs]
    wmat = [bf(nt(a2[p], s_b[p]) + _bdot(a_ak[p], v2[p])) for p in pairs]
    u = [bf(_bdot(bf(t_inv[p]), wmat[p])) for p in pairs]
    uv = [jnp.concatenate([u[p], v2[p]], axis=0) for p in pairs]
    for p in pairs:
        o2 = nt(r2[p], s_b[p]) + _bdot(jnp.concatenate([a_rb[p], a_rk[p]], axis=1), uv[p])
        o_ref[:, lanes[p]] = o2[0:L, :] + o2[L:P, :]
    for p in pairs:
        upd = tn(uv[p], jnp.concatenate([stack(bl, p), stack(kl, p)], axis=0))
        s_ref[p] = s[p] * g_last[:, lanes[p]] + upd
    y_ref[...] = _rwkv_out(o_ref[...], bonus_ref[...], g_ref[...], lnw_ref[...], lnb_ref[...], bsel_ref[...])

    @pl.when(c == pl.num_programs(1) - 1)
    def _():
        for p in range(n_pairs):
            s = s_ref[p]
            s_out_ref[2 * p] = s[0:A_HEAD_DIM, 0:A_HEAD_DIM]
            s_out_ref[2 * p + 1] = s[A_HEAD_DIM:LANE, A_HEAD_DIM:LANE]


def _rwkv_scan_chunked(ops, bonus, g, lw, s0, B, T):
    l = lw["l"]
    tok3 = [a.reshape(B, T, A_WIDTH) for a in tuple(ops) + (bonus, g)]
    tspec = pl.BlockSpec((None, CHUNK, A_WIDTH), lambda b, c: (b, c, 0))
    sspec = pl.BlockSpec((None, A_HEADS, A_HEAD_DIM, A_HEAD_DIM), lambda b, c: (b, 0, 0, 0))
    row = pl.BlockSpec((None, 1, A_WIDTH), lambda b, c: (l, 0, 0))
    y, s_t = pl.pallas_call(
        _rwkv_chunk_kernel,
        grid=(B, T // CHUNK),
        in_specs=[tspec] * 8 + [row, row, pl.BlockSpec((SEG, SEG), lambda b, c: (0, 0)), sspec],
        out_specs=[tspec, sspec],
        out_shape=[jax.ShapeDtypeStruct((B, T, A_WIDTH), BF16),
                   jax.ShapeDtypeStruct((B, A_HEADS, A_HEAD_DIM, A_HEAD_DIM), F32)],
        scratch_shapes=[pltpu.VMEM((A_HEADS // 2, LANE, LANE), F32), pltpu.VMEM((CHUNK, A_WIDTH), F32)],
        compiler_params=_cparams(2),
        name="rwkv_chunk",
    )(*tok3, lw["ln_w"], lw["ln_b"], lw["bsel"], s0)
    return y.reshape(B * T, A_WIDTH), s_t


def _rwkv_out(o, bonus, g, ln_w, ln_b, bsel):
    inv_n = 1.0 / A_HEAD_DIM
    mean = _seg_sum(o, bsel) * inv_n
    c = o - mean
    var = _seg_sum(c * c, bsel) * inv_n
    y = c * lax.rsqrt(var + GN_EPS) * ln_w + ln_b
    return ((y + bonus) * g).astype(BF16)


def _rwkv_post_kernel(o_ref, bonus_ref, g_ref, lnw_ref, lnb_ref, bsel_ref, y_ref):
    y_ref[...] = _rwkv_out(o_ref[...].T, bonus_ref[...], g_ref[...], lnw_ref[...], lnb_ref[...], bsel_ref[...])


def _rwkv_post(o, bonus, g, lw, tm):
    M = bonus.shape[0]
    tile = pl.BlockSpec((tm, A_WIDTH), lambda i: (i, 0))
    o_tile = pl.BlockSpec((A_WIDTH, tm), lambda i: (0, i))
    l = lw["l"]
    row = pl.BlockSpec((None, 1, A_WIDTH), lambda i: (l, 0, 0))
    return pl.pallas_call(
        _rwkv_post_kernel,
        grid=(M // tm,),
        in_specs=[o_tile, tile, tile, row, row, pl.BlockSpec((SEG, SEG), lambda i: (0, 0))],
        out_specs=tile,
        out_shape=jax.ShapeDtypeStruct((M, A_WIDTH), BF16),
        compiler_params=_cparams(1),
        name="rwkv_post",
    )(o, bonus, g, lw["ln_w"], lw["ln_b"], lw["bsel"])


def _rope_kernel(x_ref, c_ref, s1_ref, s2_ref, o_ref):
    x = x_ref[...]
    fwd = pltpu.roll(x, x.shape[1] - ROPE_HALF, axis=1)
    bwd = pltpu.roll(x, ROPE_HALF, axis=1)
    o_ref[...] = x * c_ref[...] + fwd * s1_ref[...] + bwd * s2_ref[...]


def _rope(proj, tables, tm):
    M = proj.shape[0]
    n_tab = tables[0].shape[0] // tm
    tab = pl.BlockSpec((tm, SEG), lambda i, j: (i % n_tab, 0))
    return pl.pallas_call(
        _rope_kernel,
        grid=(M // tm, QK_W // SEG),
        in_specs=[pl.BlockSpec((tm, SEG), lambda i, j: (i, Q_OFF // SEG + j)), tab, tab, tab],
        out_specs=pl.BlockSpec((tm, SEG), lambda i, j: (i, j)),
        out_shape=jax.ShapeDtypeStruct((M, QK_W), F32),
        compiler_params=_cparams(2),
        name="rope",
    )(proj, *tables)


def _swa_prompt_kernel(q_ref, kc_ref, kp_ref, vc_ref, vp_ref, sink_ref, o_ref):
    n = pl.program_id(1)
    rows = GROUP * WINDOW
    i = lax.broadcasted_iota(jnp.int32, (rows, 2 * WINDOW), 0) % WINDOW
    j = lax.broadcasted_iota(jnp.int32, (rows, 2 * WINDOW), 1)
    diff = WINDOW + i - j
    mask = (diff >= 0) & (diff < WINDOW) & ((n > 0) | (j >= WINDOW))
    q = q_ref[...].astype(BF16)
    kc = kc_ref[...].astype(BF16)
    kp = kp_ref[...].astype(BF16)
    vc = vc_ref[...].astype(BF16)
    vp = vp_ref[...].astype(BF16)
    khs = range(KV_HEADS)
    ksl = [slice(kh * HEAD_DIM, (kh + 1) * HEAD_DIM) for kh in khs]
    kk = [jnp.concatenate([kp[:, ksl[kh]], kc[:, ksl[kh]]], axis=0) for kh in khs]
    vv = [jnp.concatenate([vp[:, ksl[kh]], vc[:, ksl[kh]]], axis=0) for kh in khs]
    qh = [jnp.concatenate([q[:, (kh * GROUP + g) * HEAD_DIM:(kh * GROUP + g + 1) * HEAD_DIM] for g in range(GROUP)],
                          axis=0) for kh in khs]
    s = [lax.dot_general(qh[kh], kk[kh], (((1,), (1,)), ((), ())), preferred_element_type=F32) for kh in khs]
    s = [jnp.where(mask, s[kh] * (HEAD_DIM ** -0.5), -jnp.inf) for kh in khs]
    m = [jnp.maximum(jnp.max(s[kh], axis=1, keepdims=True), sink_ref[kh]) for kh in khs]
    e = [jnp.exp(s[kh] - m[kh]) for kh in khs]
    denom = [jnp.sum(e[kh], axis=1, keepdims=True) + jnp.exp(sink_ref[kh] - m[kh]) for kh in khs]
    o = [_bdot((e[kh] / denom[kh]).astype(BF16), vv[kh]) for kh in khs]
    outs = [o[kh][g * WINDOW:(g + 1) * WINDOW, :] for kh in khs for g in range(GROUP)]
    o_ref[...] = jnp.concatenate(outs, axis=1).astype(o_ref.dtype)


def _swa_prompt(qk_rot, proj, lw, B, T):
    nb = T // WINDOW
    rows = GROUP * WINDOW
    kcol = Q_W // KV_W
    vcol = V_OFF // KV_W
    l = lw["l"]
    cur = lambda col: (lambda b, n: (b * nb + n, col))
    prev = lambda col: (lambda b, n: (b * nb + jnp.maximum(n - 1, 0), col))
    return pl.pallas_call(
        _swa_prompt_kernel,
        grid=(B, nb),
        in_specs=[pl.BlockSpec((WINDOW, Q_W), cur(0)),
                  pl.BlockSpec((WINDOW, KV_W), cur(kcol)),
                  pl.BlockSpec((WINDOW, KV_W), prev(kcol)),
                  pl.BlockSpec((WINDOW, KV_W), cur(vcol)),
                  pl.BlockSpec((WINDOW, KV_W), prev(vcol)),
                  pl.BlockSpec((None, KV_HEADS, rows, 1), lambda b, n: (l, 0, 0, 0))],
        out_specs=pl.BlockSpec((WINDOW, Q_W), cur(0)),
        out_shape=jax.ShapeDtypeStruct((B * T, Q_W), BF16),
        compiler_params=_cparams(2),
        name="swa_prompt",
    )(qk_rot, qk_rot, qk_rot, proj, proj, lw["sink_p"])


def _swa_sample_kernel(q_ref, kn_ref, vn_ref, ck_ref, cv_ref, sink_ref, *rest, steps, n_prev):
    if n_prev:
        pk_ref, pv_ref, o_ref, nk_ref, nv_ref = rest
        for li in range(n_prev):
            nk_ref[li] = pk_ref[li]
            nv_ref[li] = pv_ref[li]
    else:
        o_ref, nk_ref, nv_ref = rest
    first_new = SUBLANE - steps
    new_rows = lax.broadcasted_iota(jnp.int32, (SUBLANE, KV_W), 0) >= first_new
    for b in range(q_ref.shape[0]):
        for c_ref, n_ref, dst in ((ck_ref, kn_ref, nk_ref), (cv_ref, vn_ref, nv_ref)):
            rolled = pltpu.roll(c_ref[b], WINDOW - steps, axis=0)
            tail = jnp.where(new_rows, n_ref[b], rolled[WINDOW - SUBLANE:, :])
            dst[n_prev, b] = jnp.concatenate([rolled[:WINDOW - SUBLANE, :], tail], axis=0)
    rows = GROUP * steps
    t_row = lax.broadcasted_iota(jnp.int32, (rows, WINDOW), 0) % steps
    c_col = lax.broadcasted_iota(jnp.int32, (rows, WINDOW), 1)
    cache_mask = c_col > t_row
    t_col = lax.broadcasted_iota(jnp.int32, (rows, 1), 0) % steps
    scale = HEAD_DIM ** -0.5
    probs = []
    units = [(b, kh) for b in range(q_ref.shape[0]) for kh in range(KV_HEADS)]
    ksl = [slice(kh * HEAD_DIM, (kh + 1) * HEAD_DIM) for kh in range(KV_HEADS)]
    qh = {u: q_ref[u[0], u[1]].astype(BF16) for u in units}
    sc = {u: lax.dot_general(qh[u], ck_ref[u[0], :, ksl[u[1]]].astype(BF16), (((1,), (1,)), ((), ())),
                             preferred_element_type=F32) for u in units}
    s_c, s_n, m_c = {}, {}, {}
    for u in units:
        b, kh = u
        kn = kn_ref[b, first_new:, :].astype(BF16).astype(F32)
        s_c[u] = jnp.where(cache_mask, sc[u] * scale, -jnp.inf)
        qf = qh[u].astype(F32)
        s_n[u] = [jnp.where(t_col >= t, jnp.sum(qf * kn[t:t + 1, ksl[kh]], axis=1, keepdims=True) * scale, -jnp.inf)
                  for t in range(steps)]
        m_c[u] = jnp.max(s_c[u], axis=1, keepdims=True)
    for u in units:
        sink = sink_ref[u[1]]
        m = jnp.maximum(m_c[u], sink)
        for d in s_n[u]:
            m = jnp.maximum(m, d)
        ec = jnp.exp(s_c[u] - m)
        en = [jnp.exp(d - m) for d in s_n[u]]
        denom = jnp.sum(ec, axis=1, keepdims=True) + jnp.exp(sink - m)
        for e in en:
            denom = denom + e
        probs.append(((ec / denom).astype(BF16), [(e / denom).astype(BF16).astype(F32) for e in en]))
    for u, (pc, pn) in zip(units, probs):
        b, kh = u
        vn = vn_ref[b, first_new:, :].astype(BF16).astype(F32)
        o = _bdot(pc, cv_ref[b, :, ksl[kh]].astype(BF16))
        for t in range(steps):
            o = o + pn[t] * vn[t:t + 1, ksl[kh]]
        o_ref[b, kh] = o


def _swa_sample(qk_rot, proj, cache_k, cache_v, prev_k, prev_v, lw, B, T):
    l = lw["l"]
    rows = GROUP * T
    qg = qk_rot[:, :Q_W].reshape(T, B, KV_HEADS, GROUP, HEAD_DIM).transpose(1, 2, 3, 0, 4)
    qg = qg.reshape(B, KV_HEADS, rows, HEAD_DIM)
    pad = ((0, 0), (SUBLANE - T, 0), (0, 0))
    kn = jnp.pad(qk_rot[:, Q_W:].reshape(T, B, KV_W).transpose(1, 0, 2), pad)
    vn = jnp.pad(proj[:, V_OFF:V_OFF + KV_W].reshape(T, B, KV_W).transpose(1, 0, 2), pad)
    nb = SUBLANE if B % SUBLANE == 0 else 1
    new_spec = pl.BlockSpec((nb, SUBLANE, KV_W), lambda b: (b, 0, 0))
    cache_spec = pl.BlockSpec((None, nb, WINDOW, KV_W), lambda b: (l, b, 0, 0))
    in_specs = [pl.BlockSpec((nb, KV_HEADS, rows, HEAD_DIM), lambda b: (b, 0, 0, 0)),
                new_spec, new_spec, cache_spec, cache_spec,
                pl.BlockSpec((None, KV_HEADS, rows, 1), lambda b: (l, 0, 0, 0))]
    args = [qg, kn, vn, cache_k, cache_v, lw["sink_s"]]
    if l:
        in_specs += [pl.BlockSpec((l, nb, WINDOW, KV_W), lambda b: (0, b, 0, 0))] * 2
        args += [prev_k, prev_v]
    stacked = pl.BlockSpec((l + 1, nb, WINDOW, KV_W), lambda b: (0, b, 0, 0))
    stacked_shape = jax.ShapeDtypeStruct((l + 1, B, WINDOW, KV_W), F32)
    o, new_k, new_v = pl.pallas_call(
        functools.partial(_swa_sample_kernel, steps=T, n_prev=l),
        grid=(B // nb,),
        in_specs=in_specs,
        out_specs=[pl.BlockSpec((nb, KV_HEADS, rows, HEAD_DIM), lambda b: (b, 0, 0, 0)), stacked, stacked],
        out_shape=[jax.ShapeDtypeStruct((B, KV_HEADS, rows, HEAD_DIM), F32), stacked_shape, stacked_shape],
        compiler_params=_cparams(1),
        name="swa_sample",
    )(*args)
    o = o.reshape(B, KV_HEADS, GROUP, T, HEAD_DIM).transpose(3, 0, 1, 2, 4).reshape(T * B, Q_W)
    return o.astype(BF16), new_k, new_v


def _rope_tables(pos):
    inv = 1.0 / (ROPE_THETA ** (jnp.arange(0, ROPE_DIM, 2, dtype=F32) / ROPE_DIM))
    ang = pos.astype(F32)[:, None] * inv[None, :]
    cos, sin = jnp.cos(ang), jnp.sin(ang)
    n = pos.shape[0]
    ones = jnp.ones((n, HEAD_DIM - ROPE_DIM), F32)
    zeros = jnp.zeros((n, HEAD_DIM - ROPE_DIM), F32)
    zh = jnp.zeros((n, ROPE_HALF), F32)
    c = jnp.concatenate([cos, cos, ones], axis=1)
    s1 = jnp.concatenate([-sin, zh, zeros], axis=1)
    s2 = jnp.concatenate([zh, sin, zeros], axis=1)
    return tuple(jnp.tile(a, (1, SEG // HEAD_DIM)) for a in (c, s1, s2))


def _stacked_params(t_sample, norm_g, ffn_w_gate, ffn_w_up, ffn_w_down, w_in, rwkv_mu, rwkv_w0, rwkv_w2,
                    rwkv_a0, rwkv_a2, rwkv_g2, rwkv_k_k, rwkv_k_a, rwkv_r_k, rwkv_ln_w, rwkv_ln_b, attn_sinks,
                    w_branch_a, w_branch_b, w_merge_gate, w_out):
    L = w_in.shape[0]
    w_proj = jnp.concatenate([w_in[:, :, :A_PROJ], jnp.zeros((L, D_MODEL, A_PAD - A_PROJ), w_in.dtype),
                              w_in[:, :, A_PROJ:]], axis=2)
    lora_rows = lambda w, off: jnp.pad(w, ((0, 0), (off, LORA_PAD - off - w.shape[1]), (0, 0))).astype(BF16)
    seg = jnp.arange(SEG) // A_HEAD_DIM
    row = lambda a: a.reshape(L, 1, -1)
    sink_rows = lambda n: jnp.repeat(attn_sinks.reshape(L, KV_HEADS, GROUP), n, axis=2).reshape(L, KV_HEADS, GROUP * n, 1)
    return {
        "norm_g": norm_g, "wg": ffn_w_gate, "wu": ffn_w_up, "wd": ffn_w_down, "w_proj": w_proj,
        "mu": row(jnp.pad(rwkv_mu, ((0, 0), (0, A_PAD - A_PROJ)))),
        "w0": row(rwkv_w0), "a0": row(rwkv_a0), "k_k": row(rwkv_k_k), "k_a": row(rwkv_k_a),
        "r_k": row(rwkv_r_k), "ln_w": row(rwkv_ln_w), "ln_b": row(rwkv_ln_b),
        "w2": lora_rows(rwkv_w2, 0), "a2": lora_rows(rwkv_a2, DECAY_LORA),
        "g2": lora_rows(rwkv_g2, DECAY_LORA + ICLR_LORA),
        "bsel": (seg[:, None] == seg[None, :]).astype(BF16),
        "sink_p": sink_rows(WINDOW), "sink_s": sink_rows(t_sample),
        "w_a": w_branch_a, "w_b": w_branch_b, "w_gate": w_merge_gate, "w_out": w_out,
    }


def _decoder_layer(x, mod, lw, B, T, shift_prev, tables, sample=None):
    M = B * T
    l = lw["l"]
    sp = jnp.pad(shift_prev, ((0, 0), (0, A_PAD - A_PROJ)))
    if sample is not None:
        tm = M
        tm_prep = B
        mod = mod.reshape(1, B, N_MOD * D_MODEL)
        gmap = lambda tile: (lambda i: 0)
        sp = sp.reshape(1, B, A_PAD)
    else:
        tm = min(T, 1024)
        tm_prep = min(tm, 256)
        mod = mod.reshape(B, 1, N_MOD * D_MODEL)
        gmap = lambda tile: (lambda i: i // (T // tile))
        sp = sp.reshape(B, 1, A_PAD)
    tm_k = min(tm, 512)

    def ffn(x, n, which):
        h = _ada_norm(x, lw["norm_g"], l, n, mod, gmap(tm), tm)
        act = _ffn_gate_up(h, lw["wg"], lw["wu"], (l, which), tm)
        return _mm_residual(act, lw["wd"], (l, which), x, mod, 3 * n + 2, gmap(tm_k), 0.5, tm_k)

    x = ffn(x, 0, 0)

    h = _ada_norm(x, lw["norm_g"], l, 1, mod, gmap(tm), tm)
    proj = _mm_plain(h, lw["w_proj"], (l,), tm, tn=TN_WIDE, name="in_proj")
    gates = _mm_plain(h, lw["w_gate"], (l,), tm, tn=TN_WIDE, sigmoid=True, name="merge_gates")

    r, w, k2, v, kk, kka, g, bonus = _rwkv_prep(proj, sp, lw, T, tm_prep, sample is not None)
    qk_rot = _rope(proj, tables, min(tm, 512))
    if sample is not None:
        state_wkv, cache_k, cache_v, prev_wkv, prev_k, prev_v = sample
        o, new_wkv = _rwkv_lane_scan((r, w, k2, v, kk, kka), state_wkv, prev_wkv, l, B, T)
        ya = _rwkv_post(o, bonus, g, lw, B)
        new_shift = proj[(T - 1) * B:, :A_PROJ]
        yb, new_k, new_v = _swa_sample(qk_rot, proj, cache_k, cache_v, prev_k, prev_v, lw, B, T)
    else:
        zero_wkv = jnp.zeros((B, A_HEADS, A_HEAD_DIM, A_HEAD_DIM), F32)
        ya, new_wkv = _rwkv_scan_chunked((r, w, k2, v, kk, kka), bonus, g, lw, zero_wkv, B, T)
        new_shift = proj.reshape(B, T, PROJ_W)[:, -1, :A_PROJ]
        yb = _swa_prompt(qk_rot, proj, lw, B, T)
        new_k = qk_rot[:, Q_W:].reshape(B, T, KV_HEADS, HEAD_DIM)[:, -WINDOW:]
        new_v = proj[:, V_OFF:V_OFF + KV_W].reshape(B, T, KV_HEADS, HEAD_DIM)[:, -WINDOW:]

    merged = _mm_merge(ya, yb, lw["w_a"], lw["w_b"], (l,), gates, tm, tn=TN_WIDE)
    x = _mm_residual(merged, lw["w_out"], (l,), x, mod, 5, gmap(tm), 1.0, tm, tn=TN_WIDE)

    x = ffn(x, 2, 1)
    return x, new_shift, new_wkv, new_k, new_v


def kernel(x_prompt, x_sample, c_prompt, c_sample, state_shift, state_wkv, cache_k, cache_v, norm_g, w_mod, b_mod,
           ffn_w_gate, ffn_w_up, ffn_w_down, w_in, rwkv_mu, rwkv_w0, rwkv_w2, rwkv_a0, rwkv_a2, rwkv_g2, rwkv_k_k,
           rwkv_k_a, rwkv_r_k, rwkv_ln_w, rwkv_ln_b, attn_sinks, w_branch_a, w_branch_b, w_merge_gate, w_out,
           final_norm):
    Bp, Tp, _ = x_prompt.shape
    Bs, Ts, _ = x_sample.shape
    L = w_in.shape[0]
    tab_p = _rope_tables(jnp.arange(Tp))
    tab_s = _rope_tables(jnp.repeat(PAST_LEN + jnp.arange(Ts), Bs))
    zero_shift = jnp.zeros((Bp, A_PROJ), F32)
    xp = x_prompt.reshape(Bp * Tp, D_MODEL)
    xs = x_sample.transpose(1, 0, 2).reshape(Ts * Bs, D_MODEL)
    c_all = jnp.concatenate([c_prompt, c_sample], axis=0)
    pad_rows = (-c_all.shape[0]) % 16
    c_all = jnp.pad(c_all, ((0, pad_rows), (0, 0)))
    pw = _stacked_params(Ts, norm_g, ffn_w_gate, ffn_w_up, ffn_w_down, w_in, rwkv_mu, rwkv_w0, rwkv_w2, rwkv_a0,
                         rwkv_a2, rwkv_g2, rwkv_k_k, rwkv_k_a, rwkv_r_k, rwkv_ln_w, rwkv_ln_b, attn_sinks,
                         w_branch_a, w_branch_b, w_merge_gate, w_out)
    state_s = state_wkv.reshape(L, Bs, A_HEADS * A_HEAD_DIM * A_HEAD_DIM)
    ck = cache_k.reshape(L, Bs, WINDOW, KV_W)
    cv = cache_v.reshape(L, Bs, WINDOW, KV_W)
    outs_p, shifts_s = [], []
    wkv_s = k_s = v_s = None
    for l in range(L):
        lw = dict(pw, l=l)
        mod = _modulation(c_all, w_mod, b_mod, l)
        xp, *st_p = _decoder_layer(xp, mod[:Bp], lw, Bp, Tp, zero_shift, tab_p)
        xs, shift_s, wkv_s, k_s, v_s = _decoder_layer(xs, mod[Bp:Bp + Bs], lw, Bs, Ts, state_shift[l], tab_s,
                                                      sample=(state_s, ck, cv, wkv_s, k_s, v_s))
        outs_p.append(st_p)
        shifts_s.append(shift_s)
    y_prompt = _final_norm(xp, final_norm, 1024).reshape(Bp, Tp, D_MODEL)
    y_sample = _final_norm(xs, final_norm, Bs * Ts).reshape(Ts, Bs, D_MODEL).transpose(1, 0, 2)
    stack = lambda n: jnp.stack([o[n] for o in outs_p])
    return (y_prompt, y_sample, stack(0), stack(1), stack(2), stack(3),
            jnp.stack(shifts_s), wkv_s.reshape(state_wkv.shape), k_s.reshape(cache_k.shape),
            v_s.reshape(cache_v.shape))
```

```python
import functools

import jax
import jax.numpy as jnp
from jax import lax
from jax.experimental import pallas as pl
from jax.experimental.pallas import tpu as pltpu

D_MODEL = 2048
DEPTH = 2
A_HEADS = 16
A_HEAD_DIM = 64
A_WIDTH = A_HEADS * A_HEAD_DIM
DECAY_LORA = 64
ICLR_LORA = 64
GATE_LORA = 160
LORA_W = DECAY_LORA + ICLR_LORA + GATE_LORA
A_PROJ = 3 * A_WIDTH + LORA_W
GN_EPS = 64e-5
Q_HEADS = 16
KV_HEADS = 4
GROUP = Q_HEADS // KV_HEADS
HEAD_DIM = 64
Q_W = Q_HEADS * HEAD_DIM
KV_W = KV_HEADS * HEAD_DIM
WINDOW = 128
ROPE_DIM = HEAD_DIM // 4
ROPE_HALF = ROPE_DIM // 2
ROPE_THETA = 500000.0
D_FF = 5632
N_MOD = 9
RMS_EPS = 1e-6

PAST_LEN = 8192
LANE = 128
SUBLANE = 8
SWA_BLOCKS = 4
CHUNK = 64
SOLVE_BASE = 8
SEG = 256
A_PAD = 3584
LORA_PAD = A_PAD - 3 * A_WIDTH
Q_OFF = A_PAD
K_OFF = Q_OFF + Q_W
V_OFF = K_OFF + KV_W
PROJ_W = V_OFF + KV_W
QK_W = Q_W + KV_W
VMEM_LIMIT = 56 * 1024 * 1024
TN_WIDE = 1024

F32 = jnp.float32
BF16 = jnp.bfloat16


def _cparams(n_axes):
    return pltpu.CompilerParams(dimension_semantics=("arbitrary",) * n_axes,
                                vmem_limit_bytes=VMEM_LIMIT)


def _bdot(a, b):
    return jnp.dot(a, b, preferred_element_type=F32)


def _seg_sum(x, bsel):
    hi = x.astype(BF16)
    lo = (x - hi.astype(F32)).astype(BF16)
    seg = bsel.shape[0]
    parts = []
    for c in range(x.shape[1] // seg):
        sl = slice(c * seg, (c + 1) * seg)
        parts.append(_bdot(hi[:, sl], bsel) + _bdot(lo[:, sl], bsel))
    return parts[0] if len(parts) == 1 else jnp.concatenate(parts, axis=1)


def _mod_kernel(c_ref, w_ref, b_ref, o_ref):
    c = c_ref[...]
    a = (c * jax.nn.sigmoid(c)).astype(BF16)
    o_ref[...] = _bdot(a, w_ref[...].astype(BF16)) + b_ref[...]


def _modulation(c, w_mod, b_mod, l):
    R = c.shape[0]
    N = w_mod.shape[-1]
    tn = 1024
    return pl.pallas_call(
        _mod_kernel,
        grid=(N // tn,),
        in_specs=[pl.BlockSpec((R, D_MODEL), lambda j: (0, 0)),
                  pl.BlockSpec((None, D_MODEL, tn), lambda j: (l, 0, j)),
                  pl.BlockSpec((None, 1, tn), lambda j: (l, 0, j))],
        out_specs=pl.BlockSpec((R, tn), lambda j: (0, j)),
        out_shape=jax.ShapeDtypeStruct((R, N), F32),
        compiler_params=_cparams(1),
        name="modulation",
    )(c, w_mod, b_mod.reshape(b_mod.shape[0], 1, N))


def _mod_rows(ref, tm):
    val = ref[...]
    reps = tm // val.shape[0]
    return val if val.shape[0] == 1 or reps == 1 else jnp.concatenate([val] * reps, axis=0)


def _ada_norm_kernel(x_ref, g_ref, sh_ref, sc_ref, o_ref):
    x = x_ref[...]
    tm = x.shape[0]
    y = x * lax.rsqrt(jnp.mean(x * x, axis=-1, keepdims=True) + RMS_EPS)
    y = y * g_ref[...]
    o_ref[...] = (y * (1.0 + _mod_rows(sc_ref, tm)) + _mod_rows(sh_ref, tm)).astype(o_ref.dtype)


def _ada_norm(x, norm_g, l, n, mod, gmap, tm):
    M = x.shape[0]
    R = mod.shape[1]
    mspec = lambda col: pl.BlockSpec((None, R, D_MODEL), lambda i: (gmap(i), 0, col))
    return pl.pallas_call(
        _ada_norm_kernel,
        grid=(M // tm,),
        in_specs=[pl.BlockSpec((tm, D_MODEL), lambda i: (i, 0)),
                  pl.BlockSpec((None, None, 1, D_MODEL), lambda i: (l, n, 0, 0)),
                  mspec(3 * n), mspec(3 * n + 1)],
        out_specs=pl.BlockSpec((tm, D_MODEL), lambda i: (i, 0)),
        out_shape=jax.ShapeDtypeStruct((M, D_MODEL), BF16),
        compiler_params=_cparams(1),
        name="ada_norm",
    )(x, norm_g.reshape(norm_g.shape[0], norm_g.shape[1], 1, D_MODEL), mod, mod)


def _final_norm_kernel(x_ref, g_ref, o_ref):
    x = x_ref[...]
    y = x * lax.rsqrt(jnp.mean(x * x, axis=-1, keepdims=True) + RMS_EPS)
    o_ref[...] = y * g_ref[...]


def _final_norm(x, gain, tm):
    M = x.shape[0]
    return pl.pallas_call(
        _final_norm_kernel,
        grid=(M // tm,),
        in_specs=[pl.BlockSpec((tm, D_MODEL), lambda i: (i, 0)),
                  pl.BlockSpec((1, D_MODEL), lambda i: (0, 0))],
        out_specs=pl.BlockSpec((tm, D_MODEL), lambda i: (i, 0)),
        out_shape=jax.ShapeDtypeStruct((M, D_MODEL), F32),
        compiler_params=_cparams(1),
        name="final_norm",
    )(x, gain.reshape(1, D_MODEL))


def _wspec(w, lead, tn):
    return pl.BlockSpec((None,) * len(lead) + (w.shape[-2], tn), lambda j, i: lead + (0, j))


def _wdot(a, w_ref):
    return _bdot(a, w_ref[...].astype(BF16))


def _gu_kernel(h_ref, wg_ref, wu_ref, o_ref):
    h = h_ref[...]
    g = _wdot(h, wg_ref)
    u = _wdot(h, wu_ref)
    o_ref[...] = ((g * jax.nn.sigmoid(g)) * u).astype(o_ref.dtype)


def _ffn_gate_up(h, wg, wu, lead, tm, tn=512):
    M, K = h.shape
    N = wg.shape[-1]
    return pl.pallas_call(
        _gu_kernel,
        grid=(N // tn, M // tm),
        in_specs=[pl.BlockSpec((tm, K), lambda j, i: (i, 0)), _wspec(wg, lead, tn), _wspec(wu, lead, tn)],
        out_specs=pl.BlockSpec((tm, tn), lambda j, i: (i, j)),
        out_shape=jax.ShapeDtypeStruct((M, N), BF16),
        compiler_params=_cparams(2),
        name="ffn_gate_up",
    )(h, wg, wu)


def _resid_kernel(a_ref, w_ref, x_ref, gate_ref, o_ref, *, coef):
    y = _wdot(a_ref[...], w_ref)
    o_ref[...] = x_ref[...] + (coef * _mod_rows(gate_ref, y.shape[0])) * y


def _mm_residual(a, w, lead, x, mod, n_gate, gmap, coef, tm, tn=512):
    M, K = a.shape
    N = w.shape[-1]
    R = mod.shape[1]
    nj = N // tn
    return pl.pallas_call(
        functools.partial(_resid_kernel, coef=coef),
        grid=(nj, M // tm),
        in_specs=[pl.BlockSpec((tm, K), lambda j, i: (i, 0)),
                  _wspec(w, lead, tn),
                  pl.BlockSpec((tm, tn), lambda j, i: (i, j)),
                  pl.BlockSpec((None, R, tn), lambda j, i: (gmap(i), 0, n_gate * nj + j))],
        out_specs=pl.BlockSpec((tm, tn), lambda j, i: (i, j)),
        out_shape=jax.ShapeDtypeStruct((M, N), F32),
        compiler_params=_cparams(2),
        name="mm_residual",
    )(a, w, x, mod)


def _plain_kernel(a_ref, w_ref, o_ref, *, sigmoid):
    y = _wdot(a_ref[...], w_ref)
    if sigmoid:
        y = jax.nn.sigmoid(y)
    o_ref[...] = y.astype(o_ref.dtype)


def _mm_plain(a, w, lead, tm, tn=512, sigmoid=False, name="mm_plain"):
    M, K = a.shape
    N = w.shape[-1]
    return pl.pallas_call(
        functools.partial(_plain_kernel, sigmoid=sigmoid),
        grid=(N // tn, M // tm),
        in_specs=[pl.BlockSpec((tm, K), lambda j, i: (i, 0)), _wspec(w, lead, tn)],
        out_specs=pl.BlockSpec((tm, tn), lambda j, i: (i, j)),
        out_shape=jax.ShapeDtypeStruct((M, N), F32),
        compiler_params=_cparams(2),
        name=name,
    )(a, w)


def _merge_kernel(ya_ref, yb_ref, wa_ref, wb_ref, ga_ref, gb_ref, o_ref):
    pa = _wdot(ya_ref[...], wa_ref)
    pb = _wdot(yb_ref[...], wb_ref)
    o_ref[...] = (ga_ref[...] * pa + gb_ref[...] * pb).astype(o_ref.dtype)


def _mm_merge(ya, yb, wa, wb, lead, gates, tm, tn=512):
    M, K = ya.shape
    N = wa.shape[-1]
    nj = N // tn
    return pl.pallas_call(
        _merge_kernel,
        grid=(nj, M // tm),
        in_specs=[pl.BlockSpec((tm, K), lambda j, i: (i, 0)),
                  pl.BlockSpec((tm, K), lambda j, i: (i, 0)),
                  _wspec(wa, lead, tn), _wspec(wb, lead, tn),
                  pl.BlockSpec((tm, tn), lambda j, i: (i, j)),
                  pl.BlockSpec((tm, tn), lambda j, i: (i, j + nj))],
        out_specs=pl.BlockSpec((tm, tn), lambda j, i: (i, j)),
        out_shape=jax.ShapeDtypeStruct((M, N), BF16),
        compiler_params=_cparams(2),
        name="mm_merge",
    )(ya, yb, wa, wb, gates, gates)


def _rwkv_prep_kernel(p_ref, sp_ref, mu_ref, w0_ref, a0_ref, kk_ref, ka_ref, rk_ref,
                      w2_ref, a2_ref, g2_ref, bsel_ref,
                      r_out, d_out, k_out, v_out, kk_out, kka_out, g_out, bonus_out,
                      carry_ref, *, tiles_per_seq, time_major):
    t = pl.program_id(0)
    pa = p_ref[...]
    tm = pa.shape[0]
    if time_major:
        @pl.when(t == 0)
        def _():
            carry_ref[...] = sp_ref[...]

        prev = carry_ref[...]
        carry_ref[...] = pa
    else:
        @pl.when(t % tiles_per_seq == 0)
        def _():
            carry_ref[...] = sp_ref[...]

        row = lax.broadcasted_iota(jnp.int32, pa.shape, 0)
        prev = jnp.where(row == 0, carry_ref[...], pltpu.roll(pa, 1, axis=0))
        carry_ref[...] = pa[tm - 1:tm, :]

    def put(ref, val):
        ref[...] = val.T if time_major else val

    xm = pa + (prev - pa) * mu_ref[...]
    r = xm[:, 0:A_WIDTH]
    k = xm[:, A_WIDTH:2 * A_WIDTH]
    v = xm[:, 2 * A_WIDTH:3 * A_WIDTH]
    lora = xm[:, 3 * A_WIDTH:A_PAD]
    m_w = _bdot(jnp.tanh(lora).astype(BF16), w2_ref[...])
    m_a = _bdot(lora.astype(BF16), a2_ref[...])
    g = _bdot(jax.nn.sigmoid(lora).astype(BF16), g2_ref[...])
    z = -(w0_ref[...] + m_w)
    softplus = jnp.maximum(z, 0.0) + jnp.log1p(jnp.exp(-jnp.abs(z)))
    w_log = -softplus - 0.5
    a = jax.nn.sigmoid(a0_ref[...] + m_a)
    bsel = bsel_ref[...]
    kkf = k * kk_ref[...]
    norm = jnp.sqrt(_seg_sum(kkf * kkf, bsel))
    kk = kkf / jnp.maximum(norm, 1e-12)
    k2 = k * (1.0 + (a - 1.0) * ka_ref[...])
    put(r_out, r)
    put(d_out, -jnp.exp(w_log))
    put(k_out, k2)
    put(v_out, v)
    put(kk_out, kk)
    put(kka_out, kk * a)
    g_out[...] = g
    bonus_out[...] = _seg_sum(r * k2 * rk_ref[...], bsel) * v


def _rwkv_prep(proj, sp, lw, seq_len, tm, time_major):
    M = proj.shape[0]
    R = sp.shape[1]
    l = lw["l"]
    tiles_per_seq = 1 if time_major else seq_len // tm
    sp_map = (lambda i: (0, 0, 0)) if time_major else (lambda i: (i // tiles_per_seq, 0, 0))
    row = lambda n: pl.BlockSpec((None, 1, n), lambda i: (l, 0, 0))
    full = lambda a, b: pl.BlockSpec((None, a, b), lambda i: (l, 0, 0))
    tok_spec = pl.BlockSpec((tm, A_WIDTH), lambda i: (i, 0))
    tok_shape = jax.ShapeDtypeStruct((M, A_WIDTH), F32)
    if time_major:
        op_spec = pl.BlockSpec((A_WIDTH, tm), lambda i: (0, i))
        op_shape = jax.ShapeDtypeStruct((A_WIDTH, M), F32)
    else:
        op_spec, op_shape = tok_spec, tok_shape
    return pl.pallas_call(
        functools.partial(_rwkv_prep_kernel, tiles_per_seq=tiles_per_seq, time_major=time_major),
        grid=(M // tm,),
        in_specs=[pl.BlockSpec((tm, A_PAD), lambda i: (i, 0)),
                  pl.BlockSpec((None, R, A_PAD), sp_map),
                  row(A_PAD), row(A_WIDTH), row(A_WIDTH), row(A_WIDTH), row(A_WIDTH), row(A_WIDTH),
                  full(LORA_PAD, A_WIDTH), full(LORA_PAD, A_WIDTH), full(LORA_PAD, A_WIDTH),
                  pl.BlockSpec((SEG, SEG), lambda i: (0, 0))],
        out_specs=[op_spec] * 6 + [tok_spec] * 2,
        out_shape=[op_shape] * 6 + [tok_shape] * 2,
        scratch_shapes=[pltpu.VMEM((R, A_PAD), F32)],
        compiler_params=_cparams(1),
        name="rwkv_prep",
    )(proj, sp, lw["mu"], lw["w0"], lw["a0"], lw["k_k"], lw["k_a"], lw["r_k"],
      lw["w2"], lw["a2"], lw["g2"], lw["bsel"])


def _rwkv_lane_scan_kernel(r_ref, w_ref, k_ref, v_ref, kk_ref, kka_ref, s0_ref, *rest, steps, n_prev):
    if n_prev:
        prev_ref, o_ref, s_out_ref, st_ref = rest
        for li in range(n_prev):
            s_out_ref[li] = prev_ref[li]
    else:
        o_ref, s_out_ref, st_ref = rest
    n = A_HEAD_DIM
    nb = s0_ref.shape[0]
    st_ref[...] = s0_ref[...].T
    tsl = [slice(t * nb, (t + 1) * nb) for t in range(steps)]

    def group(g, carry):
        rows8 = pl.ds(pl.multiple_of(g * SUBLANE, SUBLANE), SUBLANE)
        v8 = [v_ref[rows8, tsl[t]] for t in range(steps)]
        o8 = [[] for _ in range(steps)]
        for u in range(SUBLANE):
            srows = pl.ds(pl.multiple_of((g * SUBLANE + u) * n, n), n)
            s = st_ref[srows, :]
            for t in range(steps):
                sk = jnp.sum(s * kk_ref[:, tsl[t]], axis=0, keepdims=True)
                s = s * jnp.exp(w_ref[:, tsl[t]]) - sk * kka_ref[:, tsl[t]] + v8[t][u:u + 1, :] * k_ref[:, tsl[t]]
                o8[t].append(jnp.sum(s * r_ref[:, tsl[t]], axis=0, keepdims=True))
            st_ref[srows, :] = s
        for t in range(steps):
            o_ref[rows8, tsl[t]] = jnp.concatenate(o8[t], axis=0)
        return carry

    lax.fori_loop(0, n // SUBLANE, group, 0)
    s_out_ref[n_prev] = st_ref[...].T


def _rwkv_lane_scan(ops_t, state, prev, l, B, T):
    n2 = A_HEAD_DIM * A_HEAD_DIM
    ospec = pl.BlockSpec((A_HEAD_DIM, T * B), lambda h: (h, 0))
    in_specs = [ospec] * 6 + [pl.BlockSpec((None, B, n2), lambda h: (l, 0, h))]
    args = list(ops_t) + [state]
    if l:
        in_specs.append(pl.BlockSpec((l, B, n2), lambda h: (0, 0, h)))
        args.append(prev)
    return pl.pallas_call(
        functools.partial(_rwkv_lane_scan_kernel, steps=T, n_prev=l),
        grid=(A_HEADS,),
        in_specs=in_specs,
        out_specs=[ospec, pl.BlockSpec((l + 1, B, n2), lambda h: (0, 0, h))],
        out_shape=[jax.ShapeDtypeStruct((A_WIDTH, T * B), F32),
                   jax.ShapeDtypeStruct((l + 1, B, A_HEADS * n2), F32)],
        scratch_shapes=[pltpu.VMEM((n2, B), F32)],
        compiler_params=_cparams(1),
        name="rwkv_lane_scan",
    )(*args)


def _rwkv_chunk_kernel(r_ref, w_ref, k_ref, v_ref, kk_ref, kka_ref, bonus_ref, g_ref, lnw_ref, lnb_ref,
                       s0_ref, y_ref, s_out_ref, s_ref):
    c = pl.program_id(1)
    L = CHUNK
    P = 2 * L
    n_pairs = A_HEADS // 2
    left = lax.broadcasted_iota(jnp.int32, (L, LANE), 1) < A_HEAD_DIM
    row = lax.broadcasted_iota(jnp.int32, (P, P), 0)
    col = lax.broadcasted_iota(jnp.int32, (P, P), 1)
    strict = row > col
    lower = row >= col
    eye = (row == col).astype(F32)
    same = lambda n: (row // n) == (col // n)
    base = same(SOLVE_BASE)
    levels = []
    n = SOLVE_BASE
    while n < L:
        levels.append(same(2 * n) & jnp.logical_not(same(n)))
        n *= 2
    bf = lambda x: x.astype(BF16)
    nt = lambda a, b: lax.dot_general(a, b, (((1,), (1,)), ((), ())), preferred_element_type=F32)
    tn = lambda a, b: lax.dot_general(a, b, (((0,), (0,)), ((), ())), preferred_element_type=F32)

    @pl.when(c == 0)
    def _():
        z = jnp.zeros((A_HEAD_DIM, A_HEAD_DIM), F32)
        for p in range(n_pairs):
            s_ref[p] = jnp.concatenate([jnp.concatenate([s0_ref[2 * p], z], axis=1),
                                        jnp.concatenate([z, s0_ref[2 * p + 1]], axis=1)], axis=0)

    w = w_ref[...]
    tl = lax.broadcasted_iota(jnp.int32, (L, L), 0) >= lax.broadcasted_iota(jnp.int32, (L, L), 1)
    tri = tl.astype(BF16)
    w_hi = bf(w)
    w_r1 = w - w_hi.astype(F32)
    w_mid = bf(w_r1)
    w_lo = bf(w_r1 - w_mid.astype(F32))
    cum = _bdot(tri, w_hi) + _bdot(tri, w_mid) + _bdot(tri, w_lo)
    g_in = jnp.exp(cum)
    g_inv = jnp.exp(-cum)
    g_last = g_in[L - 1:L, :]
    at = -kk_ref[...] * jnp.exp(cum - w)
    bt = kka_ref[...] * g_inv
    kt = k_ref[...] * g_inv
    rt = r_ref[...] * g_in
    bl = bt * g_last
    kl = kt * g_last
    v = v_ref[...]

    pairs = range(n_pairs)
    lanes = [slice(p * LANE, (p + 1) * LANE) for p in pairs]

    def stack(x, p):
        xp = x[:, lanes[p]]
        return bf(jnp.concatenate([jnp.where(left, xp, 0.0), jnp.where(left, 0.0, xp)], axis=0))

    a2 = [stack(at, p) for p in pairs]
    r2 = [stack(rt, p) for p in pairs]
    b2 = [stack(bt, p) for p in pairs]
    k2 = [stack(kt, p) for p in pairs]
    v2 = [stack(v, p) for p in pairs]
    sc = [nt(jnp.concatenate([a2[p], r2[p]], axis=0), jnp.concatenate([b2[p], k2[p]], axis=0)) for p in pairs]
    a_ab = [jnp.where(strict, sc[p][0:P, 0:P], 0.0) for p in pairs]
    a_ak = [bf(jnp.where(strict, sc[p][0:P, P:2 * P], 0.0)) for p in pairs]
    a_rb = [bf(jnp.where(lower, sc[p][P:2 * P, 0:P], 0.0)) for p in pairs]
    a_rk = [bf(jnp.where(lower, sc[p][P:2 * P, P:2 * P], 0.0)) for p in pairs]
    pw = [bf(jnp.where(base, a_ab[p], 0.0)) for p in pairs]
    t_inv = [eye + pw[p].astype(F32) for p in pairs]
    m = 2
    while m < SOLVE_BASE:
        pw = [bf(_bdot(pw[p], pw[p])) for p in pairs]
        t_inv = [t_inv[p] + _bdot(bf(t_inv[p]), pw[p]) for p in pairs]
        m *= 2
    for lv in levels:
        half = [bf(_bdot(bf(t_inv[p]), bf(jnp.where(lv, a_ab[p], 0.0)))) for p in pairs]
        t_inv = [t_inv[p] + _bdot(half[p], bf(t_inv[p])) for p in pairs]
    s = [s_ref[p] for p in pairs]
    s_b = [bf(s[p]) for p in pairs]
    wmat = [bf(nt(a2[p], s_b[p]) + _bdot(a_ak[p], v2[p])) for p in pairs]
    u = [bf(_bdot(bf(t_inv[p]), wmat[p])) for p in pairs]
    uv = [jnp.concatenate([u[p], v2[p]], axis=0) for p in pairs]
    o2 = [nt(r2[p], s_b[p]) + _bdot(jnp.concatenate([a_rb[p], a_rk[p]], axis=1), uv[p]) for p in pairs]
    for p in pairs:
        upd = tn(uv[p], jnp.concatenate([stack(bl, p), stack(kl, p)], axis=0))
        s_ref[p] = s[p] * g_last[:, lanes[p]] + upd
    def head_sums(x):
        sa = jnp.sum(jnp.where(left, x, 0.0), axis=1, keepdims=True)
        sb = jnp.sum(jnp.where(left, 0.0, x), axis=1, keepdims=True)
        return jnp.where(left, sa, sb)

    inv_n = 1.0 / A_HEAD_DIM
    o = [o2[p][0:L, :] + o2[p][L:P, :] for p in pairs]
    cen = [o[p] - head_sums(o[p]) * inv_n for p in pairs]
    var = [head_sums(cen[p] * cen[p]) * inv_n for p in pairs]
    for p in pairs:
        y = cen[p] * lax.rsqrt(var[p] + GN_EPS) * lnw_ref[:, lanes[p]] + lnb_ref[:, lanes[p]]
        y_ref[:, lanes[p]] = ((y + bonus_ref[:, lanes[p]]) * g_ref[:, lanes[p]]).astype(BF16)

    @pl.when(c == pl.num_programs(1) - 1)
    def _():
        for p in range(n_pairs):
            s = s_ref[p]
            s_out_ref[2 * p] = s[0:A_HEAD_DIM, 0:A_HEAD_DIM]
            s_out_ref[2 * p + 1] = s[A_HEAD_DIM:LANE, A_HEAD_DIM:LANE]


def _rwkv_scan_chunked(ops, bonus, g, lw, s0, B, T):
    l = lw["l"]
    tok3 = [a.reshape(B, T, A_WIDTH) for a in tuple(ops) + (bonus, g)]
    tspec = pl.BlockSpec((None, CHUNK, A_WIDTH), lambda b, c: (b, c, 0))
    sspec = pl.BlockSpec((None, A_HEADS, A_HEAD_DIM, A_HEAD_DIM), lambda b, c: (b, 0, 0, 0))
    row = pl.BlockSpec((None, 1, A_WIDTH), lambda b, c: (l, 0, 0))
    y, s_t = pl.pallas_call(
        _rwkv_chunk_kernel,
        grid=(B, T // CHUNK),
        in_specs=[tspec] * 8 + [row, row, sspec],
        out_specs=[tspec, sspec],
        out_shape=[jax.ShapeDtypeStruct((B, T, A_WIDTH), BF16),
                   jax.ShapeDtypeStruct((B, A_HEADS, A_HEAD_DIM, A_HEAD_DIM), F32)],
        scratch_shapes=[pltpu.VMEM((A_HEADS // 2, LANE, LANE), F32)],
        compiler_params=_cparams(2),
        name="rwkv_chunk",
    )(*tok3, lw["ln_w"], lw["ln_b"], s0)
    return y.reshape(B * T, A_WIDTH), s_t


def _rwkv_out(o, bonus, g, ln_w, ln_b, bsel):
    inv_n = 1.0 / A_HEAD_DIM
    mean = _seg_sum(o, bsel) * inv_n
    c = o - mean
    var = _seg_sum(c * c, bsel) * inv_n
    y = c * lax.rsqrt(var + GN_EPS) * ln_w + ln_b
    return ((y + bonus) * g).astype(BF16)


def _rwkv_post_kernel(o_ref, bonus_ref, g_ref, lnw_ref, lnb_ref, bsel_ref, y_ref):
    y_ref[...] = _rwkv_out(o_ref[...].T, bonus_ref[...], g_ref[...], lnw_ref[...], lnb_ref[...], bsel_ref[...])


def _rwkv_post(o, bonus, g, lw, tm):
    M = bonus.shape[0]
    tile = pl.BlockSpec((tm, A_WIDTH), lambda i: (i, 0))
    o_tile = pl.BlockSpec((A_WIDTH, tm), lambda i: (0, i))
    l = lw["l"]
    row = pl.BlockSpec((None, 1, A_WIDTH), lambda i: (l, 0, 0))
    return pl.pallas_call(
        _rwkv_post_kernel,
        grid=(M // tm,),
        in_specs=[o_tile, tile, tile, row, row, pl.BlockSpec((SEG, SEG), lambda i: (0, 0))],
        out_specs=tile,
        out_shape=jax.ShapeDtypeStruct((M, A_WIDTH), BF16),
        compiler_params=_cparams(1),
        name="rwkv_post",
    )(o, bonus, g, lw["ln_w"], lw["ln_b"], lw["bsel"])


def _rope_kernel(x_ref, c_ref, s1_ref, s2_ref, o_ref):
    x = x_ref[...]
    fwd = pltpu.roll(x, x.shape[1] - ROPE_HALF, axis=1)
    bwd = pltpu.roll(x, ROPE_HALF, axis=1)
    o_ref[...] = x * c_ref[...] + fwd * s1_ref[...] + bwd * s2_ref[...]


def _rope(proj, tables, tm):
    M = proj.shape[0]
    n_tab = tables[0].shape[0] // tm
    tab = pl.BlockSpec((tm, SEG), lambda i, j: (i % n_tab, 0))
    return pl.pallas_call(
        _rope_kernel,
        grid=(M // tm, QK_W // SEG),
        in_specs=[pl.BlockSpec((tm, SEG), lambda i, j: (i, Q_OFF // SEG + j)), tab, tab, tab],
        out_specs=pl.BlockSpec((tm, SEG), lambda i, j: (i, j)),
        out_shape=jax.ShapeDtypeStruct((M, QK_W), F32),
        compiler_params=_cparams(2),
        name="rope",
    )(proj, *tables)


def _swa_prompt_kernel(q_ref, kc_ref, kp_ref, vc_ref, vp_ref, sink_ref, o_ref):
    n = pl.program_id(1)
    blocks = q_ref.shape[0] // WINDOW
    rows = GROUP * WINDOW
    i = lax.broadcasted_iota(jnp.int32, (rows, 2 * WINDOW), 0) % WINDOW
    j = lax.broadcasted_iota(jnp.int32, (rows, 2 * WINDOW), 1)
    diff = WINDOW + i - j
    band = (diff >= 0) & (diff < WINDOW)
    q = q_ref[...].astype(BF16)
    kall = jnp.concatenate([kp_ref[...], kc_ref[...]], axis=0).astype(BF16)
    vall = jnp.concatenate([vp_ref[...], vc_ref[...]], axis=0).astype(BF16)
    units = [(qb, kh) for qb in range(blocks) for kh in range(KV_HEADS)]
    ksl = [slice(kh * HEAD_DIM, (kh + 1) * HEAD_DIM) for kh in range(KV_HEADS)]
    mask = [band & ((n > 0) | (j >= WINDOW)) if qb == 0 else band for qb, _ in units]
    kk = [kall[qb * WINDOW:(qb + 2) * WINDOW, ksl[kh]] for qb, kh in units]
    vv = [vall[qb * WINDOW:(qb + 2) * WINDOW, ksl[kh]] for qb, kh in units]
    qh = [jnp.concatenate([q[qb * WINDOW:(qb + 1) * WINDOW, (kh * GROUP + g) * HEAD_DIM:(kh * GROUP + g + 1) * HEAD_DIM]
                           for g in range(GROUP)], axis=0) for qb, kh in units]
    us = range(len(units))
    sink = [sink_ref[kh] for _, kh in units]
    s = [lax.dot_general(qh[u], kk[u], (((1,), (1,)), ((), ())), preferred_element_type=F32) for u in us]
    s = [jnp.where(mask[u], s[u] * (HEAD_DIM ** -0.5), -jnp.inf) for u in us]
    m = [jnp.maximum(jnp.max(s[u], axis=1, keepdims=True), sink[u]) for u in us]
    e = [jnp.exp(s[u] - m[u]) for u in us]
    denom = [jnp.sum(e[u], axis=1, keepdims=True) + jnp.exp(sink[u] - m[u]) for u in us]
    o = [_bdot((e[u] / denom[u]).astype(BF16), vv[u]) for u in us]
    for qb in range(blocks):
        outs = [o[qb * KV_HEADS + kh][g * WINDOW:(g + 1) * WINDOW, :] for kh in range(KV_HEADS) for g in range(GROUP)]
        o_ref[qb * WINDOW:(qb + 1) * WINDOW, :] = jnp.concatenate(outs, axis=1).astype(o_ref.dtype)


def _swa_prompt(qk_rot, proj, lw, B, T):
    nb = T // WINDOW
    rows = GROUP * WINDOW
    kcol = Q_W // KV_W
    vcol = V_OFF // KV_W
    l = lw["l"]
    blocks = min(SWA_BLOCKS, nb)
    ng = nb // blocks
    span = blocks * WINDOW
    cur = lambda col: (lambda b, n: (b * ng + n, col))
    prev = lambda col: (lambda b, n: (b * nb + jnp.maximum(n * blocks - 1, 0), col))
    return pl.pallas_call(
        _swa_prompt_kernel,
        grid=(B, ng),
        in_specs=[pl.BlockSpec((span, Q_W), cur(0)),
                  pl.BlockSpec((span, KV_W), cur(kcol)),
                  pl.BlockSpec((WINDOW, KV_W), prev(kcol)),
                  pl.BlockSpec((span, KV_W), cur(vcol)),
                  pl.BlockSpec((WINDOW, KV_W), prev(vcol)),
                  pl.BlockSpec((None, KV_HEADS, rows, 1), lambda b, n: (l, 0, 0, 0))],
        out_specs=pl.BlockSpec((span, Q_W), cur(0)),
        out_shape=jax.ShapeDtypeStruct((B * T, Q_W), BF16),
        compiler_params=_cparams(2),
        name="swa_prompt",
    )(qk_rot, qk_rot, qk_rot, proj, proj, lw["sink_p"])


def _swa_sample_kernel(q_ref, kn_ref, vn_ref, ck_ref, cv_ref, sink_ref, *rest, steps, n_prev):
    if n_prev:
        pk_ref, pv_ref, o_ref, nk_ref, nv_ref = rest
        for li in range(n_prev):
            nk_ref[li] = pk_ref[li]
            nv_ref[li] = pv_ref[li]
    else:
        o_ref, nk_ref, nv_ref = rest
    first_new = SUBLANE - steps
    new_rows = lax.broadcasted_iota(jnp.int32, (SUBLANE, KV_W), 0) >= first_new
    for b in range(q_ref.shape[0]):
        for c_ref, n_ref, dst in ((ck_ref, kn_ref, nk_ref), (cv_ref, vn_ref, nv_ref)):
            rolled = pltpu.roll(c_ref[b], WINDOW - steps, axis=0)
            tail = jnp.where(new_rows, n_ref[b], rolled[WINDOW - SUBLANE:, :])
            dst[n_prev, b] = jnp.concatenate([rolled[:WINDOW - SUBLANE, :], tail], axis=0)
    rows = GROUP * steps
    t_row = lax.broadcasted_iota(jnp.int32, (2 * rows, WINDOW), 0) % steps
    c_col = lax.broadcasted_iota(jnp.int32, (2 * rows, WINDOW), 1)
    cache_mask = c_col > t_row
    t_col = lax.broadcasted_iota(jnp.int32, (2 * rows, 1), 0) % steps
    scale = HEAD_DIM ** -0.5
    zq = jnp.zeros((rows, HEAD_DIM), F32)
    units = [(b, p) for b in range(q_ref.shape[0]) for p in range(KV_HEADS // 2)]
    us = range(len(units))
    lanes = [slice(p * LANE, (p + 1) * LANE) for _, p in units]
    q2 = [jnp.concatenate([jnp.concatenate([q_ref[b, 2 * p], zq], axis=1),
                           jnp.concatenate([zq, q_ref[b, 2 * p + 1]], axis=1)], axis=0).astype(BF16) for b, p in units]
    kc = [ck_ref[b, :, lanes[u]].astype(BF16) for u, (b, p) in enumerate(units)]
    vc = [cv_ref[b, :, lanes[u]].astype(BF16) for u, (b, p) in enumerate(units)]
    kn = [kn_ref[b, first_new:, lanes[u]].astype(BF16).astype(F32) for u, (b, p) in enumerate(units)]
    vn = [vn_ref[b, first_new:, lanes[u]].astype(BF16).astype(F32) for u, (b, p) in enumerate(units)]
    sink = [jnp.concatenate([sink_ref[2 * p], sink_ref[2 * p + 1]], axis=0) for _, p in units]
    sc = [lax.dot_general(q2[u], kc[u], (((1,), (1,)), ((), ())), preferred_element_type=F32) for u in us]
    sc = [jnp.where(cache_mask, sc[u] * scale, -jnp.inf) for u in us]
    q2f = [q2[u].astype(F32) for u in us]
    sn = [[jnp.where(t_col >= t, jnp.sum(q2f[u] * kn[u][t:t + 1, :], axis=1, keepdims=True) * scale, -jnp.inf)
           for t in range(steps)] for u in us]
    m = [jnp.maximum(jnp.max(sc[u], axis=1, keepdims=True), sink[u]) for u in us]
    for t in range(steps):
        m = [jnp.maximum(m[u], sn[u][t]) for u in us]
    ec = [jnp.exp(sc[u] - m[u]) for u in us]
    en = [[jnp.exp(sn[u][t] - m[u]) for t in range(steps)] for u in us]
    denom = [jnp.sum(ec[u], axis=1, keepdims=True) + jnp.exp(sink[u] - m[u]) for u in us]
    for t in range(steps):
        denom = [denom[u] + en[u][t] for u in us]
    o = [_bdot((ec[u] / denom[u]).astype(BF16), vc[u]) for u in us]
    for t in range(steps):
        o = [o[u] + (en[u][t] / denom[u]).astype(BF16).astype(F32) * vn[u][t:t + 1, :] for u in us]
    for u, (b, p) in enumerate(units):
        o_ref[b, 2 * p] = o[u][0:rows, 0:HEAD_DIM]
        o_ref[b, 2 * p + 1] = o[u][rows:2 * rows, HEAD_DIM:LANE]


def _swa_sample(qk_rot, proj, cache_k, cache_v, prev_k, prev_v, lw, B, T):
    l = lw["l"]
    rows = GROUP * T
    qg = qk_rot[:, :Q_W].reshape(T, B, KV_HEADS, GROUP, HEAD_DIM).transpose(1, 2, 3, 0, 4)
    qg = qg.reshape(B, KV_HEADS, rows, HEAD_DIM)
    pad = ((0, 0), (SUBLANE - T, 0), (0, 0))
    kn = jnp.pad(qk_rot[:, Q_W:].reshape(T, B, KV_W).transpose(1, 0, 2), pad)
    vn = jnp.pad(proj[:, V_OFF:V_OFF + KV_W].reshape(T, B, KV_W).transpose(1, 0, 2), pad)
    nb = SUBLANE if B % SUBLANE == 0 else 1
    new_spec = pl.BlockSpec((nb, SUBLANE, KV_W), lambda b: (b, 0, 0))
    cache_spec = pl.BlockSpec((None, nb, WINDOW, KV_W), lambda b: (l, b, 0, 0))
    in_specs = [pl.BlockSpec((nb, KV_HEADS, rows, HEAD_DIM), lambda b: (b, 0, 0, 0)),
                new_spec, new_spec, cache_spec, cache_spec,
                pl.BlockSpec((None, KV_HEADS, rows, 1), lambda b: (l, 0, 0, 0))]
    args = [qg, kn, vn, cache_k, cache_v, lw["sink_s"]]
    if l:
        in_specs += [pl.BlockSpec((l, nb, WINDOW, KV_W), lambda b: (0, b, 0, 0))] * 2
        args += [prev_k, prev_v]
    stacked = pl.BlockSpec((l + 1, nb, WINDOW, KV_W), lambda b: (0, b, 0, 0))
    stacked_shape = jax.ShapeDtypeStruct((l + 1, B, WINDOW, KV_W), F32)
    o, new_k, new_v = pl.pallas_call(
        functools.partial(_swa_sample_kernel, steps=T, n_prev=l),
        grid=(B // nb,),
        in_specs=in_specs,
        out_specs=[pl.BlockSpec((nb, KV_HEADS, rows, HEAD_DIM), lambda b: (b, 0, 0, 0)), stacked, stacked],
        out_shape=[jax.ShapeDtypeStruct((B, KV_HEADS, rows, HEAD_DIM), F32), stacked_shape, stacked_shape],
        compiler_params=_cparams(1),
        name="swa_sample",
    )(*args)
    o = o.reshape(B, KV_HEADS, GROUP, T, HEAD_DIM).transpose(3, 0, 1, 2, 4).reshape(T * B, Q_W)
    return o.astype(BF16), new_k, new_v


def _rope_tables(pos):
    inv = 1.0 / (ROPE_THETA ** (jnp.arange(0, ROPE_DIM, 2, dtype=F32) / ROPE_DIM))
    ang = pos.astype(F32)[:, None] * inv[None, :]
    cos, sin = jnp.cos(ang), jnp.sin(ang)
    n = pos.shape[0]
    ones = jnp.ones((n, HEAD_DIM - ROPE_DIM), F32)
    zeros = jnp.zeros((n, HEAD_DIM - ROPE_DIM), F32)
    zh = jnp.zeros((n, ROPE_HALF), F32)
    c = jnp.concatenate([cos, cos, ones], axis=1)
    s1 = jnp.concatenate([-sin, zh, zeros], axis=1)
    s2 = jnp.concatenate([zh, sin, zeros], axis=1)
    return tuple(jnp.tile(a, (1, SEG // HEAD_DIM)) for a in (c, s1, s2))


def _stacked_params(t_sample, norm_g, ffn_w_gate, ffn_w_up, ffn_w_down, w_in, rwkv_mu, rwkv_w0, rwkv_w2,
                    rwkv_a0, rwkv_a2, rwkv_g2, rwkv_k_k, rwkv_k_a, rwkv_r_k, rwkv_ln_w, rwkv_ln_b, attn_sinks,
                    w_branch_a, w_branch_b, w_merge_gate, w_out):
    L = w_in.shape[0]
    w_proj = jnp.concatenate([w_in[:, :, :A_PROJ], jnp.zeros((L, D_MODEL, A_PAD - A_PROJ), w_in.dtype),
                              w_in[:, :, A_PROJ:]], axis=2)
    lora_rows = lambda w, off: jnp.pad(w, ((0, 0), (off, LORA_PAD - off - w.shape[1]), (0, 0))).astype(BF16)
    seg = jnp.arange(SEG) // A_HEAD_DIM
    row = lambda a: a.reshape(L, 1, -1)
    sink_rows = lambda n: jnp.repeat(attn_sinks.reshape(L, KV_HEADS, GROUP), n, axis=2).reshape(L, KV_HEADS, GROUP * n, 1)
    return {
        "norm_g": norm_g, "wg": ffn_w_gate, "wu": ffn_w_up, "wd": ffn_w_down, "w_proj": w_proj,
        "mu": row(jnp.pad(rwkv_mu, ((0, 0), (0, A_PAD - A_PROJ)))),
        "w0": row(rwkv_w0), "a0": row(rwkv_a0), "k_k": row(rwkv_k_k), "k_a": row(rwkv_k_a),
        "r_k": row(rwkv_r_k), "ln_w": row(rwkv_ln_w), "ln_b": row(rwkv_ln_b),
        "w2": lora_rows(rwkv_w2, 0), "a2": lora_rows(rwkv_a2, DECAY_LORA),
        "g2": lora_rows(rwkv_g2, DECAY_LORA + ICLR_LORA),
        "bsel": (seg[:, None] == seg[None, :]).astype(BF16),
        "sink_p": sink_rows(WINDOW), "sink_s": sink_rows(t_sample),
        "w_a": w_branch_a, "w_b": w_branch_b, "w_gate": w_merge_gate, "w_out": w_out,
    }


def _decoder_layer(x, mod, lw, B, T, shift_prev, tables, sample=None):
    M = B * T
    l = lw["l"]
    sp = jnp.pad(shift_prev, ((0, 0), (0, A_PAD - A_PROJ)))
    if sample is not None:
        tm = M
        tm_prep = B
        mod = mod.reshape(1, B, N_MOD * D_MODEL)
        gmap = lambda tile: (lambda i: 0)
        sp = sp.reshape(1, B, A_PAD)
    else:
        tm = min(T, 1024)
        tm_prep = min(tm, 256)
        mod = mod.reshape(B, 1, N_MOD * D_MODEL)
        gmap = lambda tile: (lambda i: i // (T // tile))
        sp = sp.reshape(B, 1, A_PAD)
    tm_k = min(tm, 512)

    def ffn(x, n, which):
        h = _ada_norm(x, lw["norm_g"], l, n, mod, gmap(tm), tm)
        act = _ffn_gate_up(h, lw["wg"], lw["wu"], (l, which), tm)
        return _mm_residual(act, lw["wd"], (l, which), x, mod, 3 * n + 2, gmap(tm_k), 0.5, tm_k)

    x = ffn(x, 0, 0)

    h = _ada_norm(x, lw["norm_g"], l, 1, mod, gmap(tm), tm)
    proj = _mm_plain(h, lw["w_proj"], (l,), tm, tn=TN_WIDE, name="in_proj")
    gates = _mm_plain(h, lw["w_gate"], (l,), tm, tn=TN_WIDE, sigmoid=True, name="merge_gates")

    r, w, k2, v, kk, kka, g, bonus = _rwkv_prep(proj, sp, lw, T, tm_prep, sample is not None)
    qk_rot = _rope(proj, tables, min(tm, 512))
    if sample is not None:
        state_wkv, cache_k, cache_v, prev_wkv, prev_k, prev_v = sample
        o, new_wkv = _rwkv_lane_scan((r, w, k2, v, kk, kka), state_wkv, prev_wkv, l, B, T)
        ya = _rwkv_post(o, bonus, g, lw, B)
        new_shift = proj[(T - 1) * B:, :A_PROJ]
        yb, new_k, new_v = _swa_sample(qk_rot, proj, cache_k, cache_v, prev_k, prev_v, lw, B, T)
    else:
        zero_wkv = jnp.zeros((B, A_HEADS, A_HEAD_DIM, A_HEAD_DIM), F32)
        ya, new_wkv = _rwkv_scan_chunked((r, w, k2, v, kk, kka), bonus, g, lw, zero_wkv, B, T)
        new_shift = proj.reshape(B, T, PROJ_W)[:, -1, :A_PROJ]
        yb = _swa_prompt(qk_rot, proj, lw, B, T)
        new_k = qk_rot[:, Q_W:].reshape(B, T, KV_HEADS, HEAD_DIM)[:, -WINDOW:]
        new_v = proj[:, V_OFF:V_OFF + KV_W].reshape(B, T, KV_HEADS, HEAD_DIM)[:, -WINDOW:]

    merged = _mm_merge(ya, yb, lw["w_a"], lw["w_b"], (l,), gates, tm, tn=TN_WIDE)
    x = _mm_residual(merged, lw["w_out"], (l,), x, mod, 5, gmap(tm), 1.0, tm, tn=TN_WIDE)

    x = ffn(x, 2, 1)
    return x, new_shift, new_wkv, new_k, new_v


def kernel(x_prompt, x_sample, c_prompt, c_sample, state_shift, state_wkv, cache_k, cache_v, norm_g, w_mod, b_mod,
           ffn_w_gate, ffn_w_up, ffn_w_down, w_in, rwkv_mu, rwkv_w0, rwkv_w2, rwkv_a0, rwkv_a2, rwkv_g2, rwkv_k_k,
           rwkv_k_a, rwkv_r_k, rwkv_ln_w, rwkv_ln_b, attn_sinks, w_branch_a, w_branch_b, w_merge_gate, w_out,
           final_norm):
    Bp, Tp, _ = x_prompt.shape
    Bs, Ts, _ = x_sample.shape
    L = w_in.shape[0]
    tab_p = _rope_tables(jnp.arange(Tp))
    tab_s = _rope_tables(jnp.repeat(PAST_LEN + jnp.arange(Ts), Bs))
    zero_shift = jnp.zeros((Bp, A_PROJ), F32)
    xp = x_prompt.reshape(Bp * Tp, D_MODEL)
    xs = x_sample.transpose(1, 0, 2).reshape(Ts * Bs, D_MODEL)
    c_all = jnp.concatenate([c_prompt, c_sample], axis=0)
    pad_rows = (-c_all.shape[0]) % 16
    c_all = jnp.pad(c_all, ((0, pad_rows), (0, 0)))
    pw = _stacked_params(Ts, norm_g, ffn_w_gate, ffn_w_up, ffn_w_down, w_in, rwkv_mu, rwkv_w0, rwkv_w2, rwkv_a0,
                         rwkv_a2, rwkv_g2, rwkv_k_k, rwkv_k_a, rwkv_r_k, rwkv_ln_w, rwkv_ln_b, attn_sinks,
                         w_branch_a, w_branch_b, w_merge_gate, w_out)
    state_s = state_wkv.reshape(L, Bs, A_HEADS * A_HEAD_DIM * A_HEAD_DIM)
    ck = cache_k.reshape(L, Bs, WINDOW, KV_W)
    cv = cache_v.reshape(L, Bs, WINDOW, KV_W)
    outs_p, shifts_s = [], []
    wkv_s = k_s = v_s = None
    for l in range(L):
        lw = dict(pw, l=l)
        mod = _modulation(c_all, w_mod, b_mod, l)
        xp, *st_p = _decoder_layer(xp, mod[:Bp], lw, Bp, Tp, zero_shift, tab_p)
        xs, shift_s, wkv_s, k_s, v_s = _decoder_layer(xs, mod[Bp:Bp + Bs], lw, Bs, Ts, state_shift[l], tab_s,
                                                      sample=(state_s, ck, cv, wkv_s, k_s, v_s))
        outs_p.append(st_p)
        shifts_s.append(shift_s)
    y_prompt = _final_norm(xp, final_norm, 1024).reshape(Bp, Tp, D_MODEL)
    y_sample = _final_norm(xs, final_norm, Bs * Ts).reshape(Ts, Bs, D_MODEL).transpose(1, 0, 2)
    stack = lambda n: jnp.stack([o[n] for o in outs_p])
    return (y_prompt, y_sample, stack(0), stack(1), stack(2), stack(3),
            jnp.stack(shifts_s), wkv_s.reshape(state_wkv.shape), k_s.reshape(cache_k.shape),
            v_s.reshape(cache_v.shape))
```

```python
import functools

import jax
import jax.numpy as jnp
from jax import lax
from jax.experimental import pallas as pl
from jax.experimental.pallas import tpu as pltpu

D_MODEL = 2048
DEPTH = 2
A_HEADS = 16
A_HEAD_DIM = 64
A_WIDTH = A_HEADS * A_HEAD_DIM
DECAY_LORA = 64
ICLR_LORA = 64
GATE_LORA = 160
LORA_W = DECAY_LORA + ICLR_LORA + GATE_LORA
A_PROJ = 3 * A_WIDTH + LORA_W
GN_EPS = 64e-5
Q_HEADS = 16
KV_HEADS = 4
GROUP = Q_HEADS // KV_HEADS
HEAD_DIM = 64
Q_W = Q_HEADS * HEAD_DIM
KV_W = KV_HEADS * HEAD_DIM
WINDOW = 128
ROPE_DIM = HEAD_DIM // 4
ROPE_HALF = ROPE_DIM // 2
ROPE_THETA = 500000.0
D_FF = 5632
N_MOD = 9
RMS_EPS = 1e-6

PAST_LEN = 8192
LANE = 128
SUBLANE = 8
ROPE_ROWS = 2048
SWA_BLOCKS = 4
CHUNK = 64
SOLVE_BASE = 8
SEG = 256
A_PAD = 3584
LORA_PAD = A_PAD - 3 * A_WIDTH
Q_OFF = A_PAD
K_OFF = Q_OFF + Q_W
V_OFF = K_OFF + KV_W
PROJ_W = V_OFF + KV_W
QK_W = Q_W + KV_W
VMEM_LIMIT = 56 * 1024 * 1024
TN_WIDE = 1024

F32 = jnp.float32
BF16 = jnp.bfloat16


def _cparams(n_axes):
    return pltpu.CompilerParams(dimension_semantics=("arbitrary",) * n_axes,
                                vmem_limit_bytes=VMEM_LIMIT)


def _bdot(a, b):
    return jnp.dot(a, b, preferred_element_type=F32)


def _seg_sum(x, bsel):
    hi = x.astype(BF16)
    lo = (x - hi.astype(F32)).astype(BF16)
    seg = bsel.shape[0]
    parts = []
    for c in range(x.shape[1] // seg):
        sl = slice(c * seg, (c + 1) * seg)
        parts.append(_bdot(hi[:, sl], bsel) + _bdot(lo[:, sl], bsel))
    return parts[0] if len(parts) == 1 else jnp.concatenate(parts, axis=1)


def _mod_kernel(c_ref, w_ref, b_ref, o_ref):
    c = c_ref[...]
    a = (c * jax.nn.sigmoid(c)).astype(BF16)
    o_ref[...] = _bdot(a, w_ref[...].astype(BF16)) + b_ref[...]


def _modulation(c, w_mod, b_mod, l):
    R = c.shape[0]
    N = w_mod.shape[-1]
    tn = 1024
    return pl.pallas_call(
        _mod_kernel,
        grid=(N // tn,),
        in_specs=[pl.BlockSpec((R, D_MODEL), lambda j: (0, 0)),
                  pl.BlockSpec((None, D_MODEL, tn), lambda j: (l, 0, j)),
                  pl.BlockSpec((None, 1, tn), lambda j: (l, 0, j))],
        out_specs=pl.BlockSpec((R, tn), lambda j: (0, j)),
        out_shape=jax.ShapeDtypeStruct((R, N), F32),
        compiler_params=_cparams(1),
        name="modulation",
    )(c, w_mod, b_mod.reshape(b_mod.shape[0], 1, N))


def _mod_rows(ref, tm):
    val = ref[...]
    reps = tm // val.shape[0]
    return val if val.shape[0] == 1 or reps == 1 else jnp.concatenate([val] * reps, axis=0)


def _ada_norm_kernel(x_ref, g_ref, sh_ref, sc_ref, o_ref):
    x = x_ref[...]
    tm = x.shape[0]
    y = x * lax.rsqrt(jnp.mean(x * x, axis=-1, keepdims=True) + RMS_EPS)
    y = y * g_ref[...]
    o_ref[...] = (y * (1.0 + _mod_rows(sc_ref, tm)) + _mod_rows(sh_ref, tm)).astype(o_ref.dtype)


def _ada_norm(x, norm_g, l, n, mod, gmap, tm):
    M = x.shape[0]
    R = mod.shape[1]
    mspec = lambda col: pl.BlockSpec((None, R, D_MODEL), lambda i: (gmap(i), 0, col))
    return pl.pallas_call(
        _ada_norm_kernel,
        grid=(M // tm,),
        in_specs=[pl.BlockSpec((tm, D_MODEL), lambda i: (i, 0)),
                  pl.BlockSpec((None, None, 1, D_MODEL), lambda i: (l, n, 0, 0)),
                  mspec(3 * n), mspec(3 * n + 1)],
        out_specs=pl.BlockSpec((tm, D_MODEL), lambda i: (i, 0)),
        out_shape=jax.ShapeDtypeStruct((M, D_MODEL), BF16),
        compiler_params=_cparams(1),
        name="ada_norm",
    )(x, norm_g.reshape(norm_g.shape[0], norm_g.shape[1], 1, D_MODEL), mod, mod)


def _final_norm_kernel(x_ref, g_ref, o_ref):
    x = x_ref[...]
    y = x * lax.rsqrt(jnp.mean(x * x, axis=-1, keepdims=True) + RMS_EPS)
    o_ref[...] = y * g_ref[...]


def _final_norm(x, gain, tm):
    M = x.shape[0]
    return pl.pallas_call(
        _final_norm_kernel,
        grid=(M // tm,),
        in_specs=[pl.BlockSpec((tm, D_MODEL), lambda i: (i, 0)),
                  pl.BlockSpec((1, D_MODEL), lambda i: (0, 0))],
        out_specs=pl.BlockSpec((tm, D_MODEL), lambda i: (i, 0)),
        out_shape=jax.ShapeDtypeStruct((M, D_MODEL), F32),
        compiler_params=_cparams(1),
        name="final_norm",
    )(x, gain.reshape(1, D_MODEL))


def _wspec(w, lead, tn):
    return pl.BlockSpec((None,) * len(lead) + (w.shape[-2], tn), lambda j, i: lead + (0, j))


def _wdot(a, w_ref):
    return _bdot(a, w_ref[...].astype(BF16))


def _gu_kernel(h_ref, wg_ref, wu_ref, o_ref):
    h = h_ref[...]
    g = _wdot(h, wg_ref)
    u = _wdot(h, wu_ref)
    o_ref[...] = ((g * jax.nn.sigmoid(g)) * u).astype(o_ref.dtype)


def _ffn_gate_up(h, wg, wu, lead, tm, tn=512):
    M, K = h.shape
    N = wg.shape[-1]
    return pl.pallas_call(
        _gu_kernel,
        grid=(N // tn, M // tm),
        in_specs=[pl.BlockSpec((tm, K), lambda j, i: (i, 0)), _wspec(wg, lead, tn), _wspec(wu, lead, tn)],
        out_specs=pl.BlockSpec((tm, tn), lambda j, i: (i, j)),
        out_shape=jax.ShapeDtypeStruct((M, N), BF16),
        compiler_params=_cparams(2),
        name="ffn_gate_up",
    )(h, wg, wu)


def _resid_kernel(a_ref, w_ref, x_ref, gate_ref, o_ref, *, coef):
    y = _wdot(a_ref[...], w_ref)
    o_ref[...] = x_ref[...] + (coef * _mod_rows(gate_ref, y.shape[0])) * y


def _mm_residual(a, w, lead, x, mod, n_gate, gmap, coef, tm, tn=512):
    M, K = a.shape
    N = w.shape[-1]
    R = mod.shape[1]
    nj = N // tn
    return pl.pallas_call(
        functools.partial(_resid_kernel, coef=coef),
        grid=(nj, M // tm),
        in_specs=[pl.BlockSpec((tm, K), lambda j, i: (i, 0)),
                  _wspec(w, lead, tn),
                  pl.BlockSpec((tm, tn), lambda j, i: (i, j)),
                  pl.BlockSpec((None, R, tn), lambda j, i: (gmap(i), 0, n_gate * nj + j))],
        out_specs=pl.BlockSpec((tm, tn), lambda j, i: (i, j)),
        out_shape=jax.ShapeDtypeStruct((M, N), F32),
        compiler_params=_cparams(2),
        name="mm_residual",
    )(a, w, x, mod)


def _plain_kernel(a_ref, w_ref, o_ref, *, sigmoid):
    y = _wdot(a_ref[...], w_ref)
    if sigmoid:
        y = jax.nn.sigmoid(y)
    o_ref[...] = y.astype(o_ref.dtype)


def _mm_plain(a, w, lead, tm, tn=512, sigmoid=False, name="mm_plain"):
    M, K = a.shape
    N = w.shape[-1]
    return pl.pallas_call(
        functools.partial(_plain_kernel, sigmoid=sigmoid),
        grid=(N // tn, M // tm),
        in_specs=[pl.BlockSpec((tm, K), lambda j, i: (i, 0)), _wspec(w, lead, tn)],
        out_specs=pl.BlockSpec((tm, tn), lambda j, i: (i, j)),
        out_shape=jax.ShapeDtypeStruct((M, N), F32),
        compiler_params=_cparams(2),
        name=name,
    )(a, w)


def _merge_kernel(ya_ref, yb_ref, wa_ref, wb_ref, ga_ref, gb_ref, o_ref):
    pa = _wdot(ya_ref[...], wa_ref)
    pb = _wdot(yb_ref[...], wb_ref)
    o_ref[...] = (ga_ref[...] * pa + gb_ref[...] * pb).astype(o_ref.dtype)


def _mm_merge(ya, yb, wa, wb, lead, gates, tm, tn=512):
    M, K = ya.shape
    N = wa.shape[-1]
    nj = N // tn
    return pl.pallas_call(
        _merge_kernel,
        grid=(nj, M // tm),
        in_specs=[pl.BlockSpec((tm, K), lambda j, i: (i, 0)),
                  pl.BlockSpec((tm, K), lambda j, i: (i, 0)),
                  _wspec(wa, lead, tn), _wspec(wb, lead, tn),
                  pl.BlockSpec((tm, tn), lambda j, i: (i, j)),
                  pl.BlockSpec((tm, tn), lambda j, i: (i, j + nj))],
        out_specs=pl.BlockSpec((tm, tn), lambda j, i: (i, j)),
        out_shape=jax.ShapeDtypeStruct((M, N), BF16),
        compiler_params=_cparams(2),
        name="mm_merge",
    )(ya, yb, wa, wb, gates, gates)


def _rwkv_operands(pa, prev, mu, w0, a0, k_k, k_a, r_k, w2, a2, g2, head_sums):
    xm = pa + (prev - pa) * mu
    r = xm[:, 0:A_WIDTH]
    k = xm[:, A_WIDTH:2 * A_WIDTH]
    v = xm[:, 2 * A_WIDTH:3 * A_WIDTH]
    lora = xm[:, 3 * A_WIDTH:A_PAD]
    m_w = _bdot(jnp.tanh(lora).astype(BF16), w2)
    m_a = _bdot(lora.astype(BF16), a2)
    g = _bdot(jax.nn.sigmoid(lora).astype(BF16), g2)
    z = -(w0 + m_w)
    softplus = jnp.maximum(z, 0.0) + jnp.log1p(jnp.exp(-jnp.abs(z)))
    w_log = -softplus - 0.5
    a = jax.nn.sigmoid(a0 + m_a)
    kkf = k * k_k
    kk = kkf / jnp.maximum(jnp.sqrt(head_sums(kkf * kkf)), 1e-12)
    k2 = k * (1.0 + (a - 1.0) * k_a)
    bonus = head_sums(r * k2 * r_k) * v
    return r, -jnp.exp(w_log), k2, v, kk, kk * a, g, bonus


def _rwkv_prep_kernel(p_ref, sp_ref, mu_ref, w0_ref, a0_ref, kk_ref, ka_ref, rk_ref,
                      w2_ref, a2_ref, g2_ref, bsel_ref,
                      r_out, d_out, k_out, v_out, kk_out, kka_out, g_out, bonus_out, carry_ref):
    t = pl.program_id(0)
    pa = p_ref[...]

    @pl.when(t == 0)
    def _():
        carry_ref[...] = sp_ref[...]

    prev = carry_ref[...]
    carry_ref[...] = pa
    bsel = bsel_ref[...]
    r, w, k2, v, kk, kka, g, bonus = _rwkv_operands(
        pa, prev, mu_ref[...], w0_ref[...], a0_ref[...], kk_ref[...], ka_ref[...], rk_ref[...],
        w2_ref[...], a2_ref[...], g2_ref[...], lambda x: _seg_sum(x, bsel))
    for ref, val in ((r_out, r), (d_out, w), (k_out, k2), (v_out, v), (kk_out, kk), (kka_out, kka)):
        ref[...] = val.T
    g_out[...] = g
    bonus_out[...] = bonus


def _rwkv_prep(proj, sp, lw, n_seq):
    M = proj.shape[0]
    l = lw["l"]
    row = lambda n: pl.BlockSpec((None, 1, n), lambda i: (l, 0, 0))
    full = lambda a, b: pl.BlockSpec((None, a, b), lambda i: (l, 0, 0))
    tok_spec = pl.BlockSpec((n_seq, A_WIDTH), lambda i: (i, 0))
    tok_shape = jax.ShapeDtypeStruct((M, A_WIDTH), F32)
    op_spec = pl.BlockSpec((A_WIDTH, n_seq), lambda i: (0, i))
    op_shape = jax.ShapeDtypeStruct((A_WIDTH, M), F32)
    return pl.pallas_call(
        _rwkv_prep_kernel,
        grid=(M // n_seq,),
        in_specs=[pl.BlockSpec((n_seq, A_PAD), lambda i: (i, 0)),
                  pl.BlockSpec((None, n_seq, A_PAD), lambda i: (0, 0, 0)),
                  row(A_PAD), row(A_WIDTH), row(A_WIDTH), row(A_WIDTH), row(A_WIDTH), row(A_WIDTH),
                  full(LORA_PAD, A_WIDTH), full(LORA_PAD, A_WIDTH), full(LORA_PAD, A_WIDTH),
                  pl.BlockSpec((SEG, SEG), lambda i: (0, 0))],
        out_specs=[op_spec] * 6 + [tok_spec] * 2,
        out_shape=[op_shape] * 6 + [tok_shape] * 2,
        scratch_shapes=[pltpu.VMEM((n_seq, A_PAD), F32)],
        compiler_params=_cparams(1),
        name="rwkv_prep",
    )(proj, sp, lw["mu"], lw["w0"], lw["a0"], lw["k_k"], lw["k_a"], lw["r_k"],
      lw["w2"], lw["a2"], lw["g2"], lw["bsel"])


def _rwkv_lane_scan_kernel(r_ref, w_ref, k_ref, v_ref, kk_ref, kka_ref, s0_ref, *rest, steps, n_prev):
    if n_prev:
        prev_ref, o_ref, s_out_ref, st_ref = rest
        for li in range(n_prev):
            s_out_ref[li] = prev_ref[li]
    else:
        o_ref, s_out_ref, st_ref = rest
    n = A_HEAD_DIM
    nb = s0_ref.shape[0]
    st_ref[...] = s0_ref[...].T
    tsl = [slice(t * nb, (t + 1) * nb) for t in range(steps)]

    def group(g, carry):
        rows8 = pl.ds(pl.multiple_of(g * SUBLANE, SUBLANE), SUBLANE)
        v8 = [v_ref[rows8, tsl[t]] for t in range(steps)]
        o8 = [[] for _ in range(steps)]
        for u in range(SUBLANE):
            srows = pl.ds(pl.multiple_of((g * SUBLANE + u) * n, n), n)
            s = st_ref[srows, :]
            for t in range(steps):
                sk = jnp.sum(s * kk_ref[:, tsl[t]], axis=0, keepdims=True)
                s = s * jnp.exp(w_ref[:, tsl[t]]) - sk * kka_ref[:, tsl[t]] + v8[t][u:u + 1, :] * k_ref[:, tsl[t]]
                o8[t].append(jnp.sum(s * r_ref[:, tsl[t]], axis=0, keepdims=True))
            st_ref[srows, :] = s
        for t in range(steps):
            o_ref[rows8, tsl[t]] = jnp.concatenate(o8[t], axis=0)
        return carry

    lax.fori_loop(0, n // SUBLANE, group, 0)
    s_out_ref[n_prev] = st_ref[...].T


def _rwkv_lane_scan(ops_t, state, prev, l, B, T):
    n2 = A_HEAD_DIM * A_HEAD_DIM
    ospec = pl.BlockSpec((A_HEAD_DIM, T * B), lambda h: (h, 0))
    in_specs = [ospec] * 6 + [pl.BlockSpec((None, B, n2), lambda h: (l, 0, h))]
    args = list(ops_t) + [state]
    if l:
        in_specs.append(pl.BlockSpec((l, B, n2), lambda h: (0, 0, h)))
        args.append(prev)
    return pl.pallas_call(
        functools.partial(_rwkv_lane_scan_kernel, steps=T, n_prev=l),
        grid=(A_HEADS,),
        in_specs=in_specs,
        out_specs=[ospec, pl.BlockSpec((l + 1, B, n2), lambda h: (0, 0, h))],
        out_shape=[jax.ShapeDtypeStruct((A_WIDTH, T * B), F32),
                   jax.ShapeDtypeStruct((l + 1, B, A_HEADS * n2), F32)],
        scratch_shapes=[pltpu.VMEM((n2, B), F32)],
        compiler_params=_cparams(1),
        name="rwkv_lane_scan",
    )(*args)


def _rwkv_chunk_kernel(p_ref, sp_ref, mu_ref, w0_ref, a0_ref, kk_ref, ka_ref, rk_ref, w2_ref, a2_ref, g2_ref,
                       lnw_ref, lnb_ref, s0_ref, y_ref, s_out_ref, s_ref, carry_ref):
    c = pl.program_id(1)
    L = CHUNK
    P = 2 * L
    n_pairs = A_HEADS // 2
    left = lax.broadcasted_iota(jnp.int32, (L, LANE), 1) < A_HEAD_DIM
    row = lax.broadcasted_iota(jnp.int32, (P, P), 0)
    col = lax.broadcasted_iota(jnp.int32, (P, P), 1)
    strict = row > col
    lower = row >= col
    eye = (row == col).astype(F32)
    same = lambda n: (row // n) == (col // n)
    base = same(SOLVE_BASE)
    levels = []
    n = SOLVE_BASE
    while n < L:
        levels.append(same(2 * n) & jnp.logical_not(same(n)))
        n *= 2
    bf = lambda x: x.astype(BF16)
    nt = lambda a, b: lax.dot_general(a, b, (((1,), (1,)), ((), ())), preferred_element_type=F32)
    tn = lambda a, b: lax.dot_general(a, b, (((0,), (0,)), ((), ())), preferred_element_type=F32)

    pairs = range(n_pairs)
    lanes = [slice(p * LANE, (p + 1) * LANE) for p in pairs]

    @pl.when(c == 0)
    def _():
        z = jnp.zeros((A_HEAD_DIM, A_HEAD_DIM), F32)
        for p in range(n_pairs):
            s_ref[p] = jnp.concatenate([jnp.concatenate([s0_ref[2 * p], z], axis=1),
                                        jnp.concatenate([z, s0_ref[2 * p + 1]], axis=1)], axis=0)
        carry_ref[...] = sp_ref[...]

    def head_sums(x):
        sa = jnp.sum(jnp.where(left, x, 0.0), axis=1, keepdims=True)
        sb = jnp.sum(jnp.where(left, 0.0, x), axis=1, keepdims=True)
        return jnp.where(left, sa, sb)

    pa = p_ref[...]
    first_row = lax.broadcasted_iota(jnp.int32, pa.shape, 0) == 0
    prev = jnp.where(first_row, carry_ref[...], pltpu.roll(pa, 1, axis=0))
    carry_ref[...] = pa[L - 1:L, :]
    r, w, k, v, kk, kka, g, bonus = _rwkv_operands(
        pa, prev, mu_ref[...], w0_ref[...], a0_ref[...], kk_ref[...], ka_ref[...], rk_ref[...],
        w2_ref[...], a2_ref[...], g2_ref[...], lambda x: jnp.concatenate([head_sums(x[:, sl]) for sl in lanes], axis=1))

    tl = lax.broadcasted_iota(jnp.int32, (L, L), 0) >= lax.broadcasted_iota(jnp.int32, (L, L), 1)
    tri = tl.astype(BF16)
    w_hi = bf(w)
    w_r1 = w - w_hi.astype(F32)
    w_mid = bf(w_r1)
    w_lo = bf(w_r1 - w_mid.astype(F32))
    cum = _bdot(tri, w_hi) + _bdot(tri, w_mid) + _bdot(tri, w_lo)
    g_in = jnp.exp(cum)
    g_inv = jnp.exp(-cum)
    g_last = g_in[L - 1:L, :]
    at = -kk * jnp.exp(cum - w)
    bt = kka * g_inv
    kt = k * g_inv
    rt = r * g_in
    bl = bt * g_last
    kl = kt * g_last


    def stack(x, p):
        xp = x[:, lanes[p]]
        return bf(jnp.concatenate([jnp.where(left, xp, 0.0), jnp.where(left, 0.0, xp)], axis=0))

    a2 = [stack(at, p) for p in pairs]
    r2 = [stack(rt, p) for p in pairs]
    b2 = [stack(bt, p) for p in pairs]
    k2 = [stack(kt, p) for p in pairs]
    v2 = [stack(v, p) for p in pairs]
    sc = [nt(jnp.concatenate([a2[p], r2[p]], axis=0), jnp.concatenate([b2[p], k2[p]], axis=0)) for p in pairs]
    a_ab = [jnp.where(strict, sc[p][0:P, 0:P], 0.0) for p in pairs]
    a_ak = [bf(jnp.where(strict, sc[p][0:P, P:2 * P], 0.0)) for p in pairs]
    a_rb = [bf(jnp.where(lower, sc[p][P:2 * P, 0:P], 0.0)) for p in pairs]
    a_rk = [bf(jnp.where(lower, sc[p][P:2 * P, P:2 * P], 0.0)) for p in pairs]
    pw = [bf(jnp.where(base, a_ab[p], 0.0)) for p in pairs]
    t_inv = [eye + pw[p].astype(F32) for p in pairs]
    m = 2
    while m < SOLVE_BASE:
        pw = [bf(_bdot(pw[p], pw[p])) for p in pairs]
        t_inv = [t_inv[p] + _bdot(bf(t_inv[p]), pw[p]) for p in pairs]
        m *= 2
    for lv in levels:
        half = [bf(_bdot(bf(t_inv[p]), bf(jnp.where(lv, a_ab[p], 0.0)))) for p in pairs]
        t_inv = [t_inv[p] + _bdot(half[p], bf(t_inv[p])) for p in pairs]
    s = [s_ref[p] for p in pairs]
    s_b = [bf(s[p]) for p in pairs]
    wmat = [bf(nt(a2[p], s_b[p]) + _bdot(a_ak[p], v2[p])) for p in pairs]
    u = [bf(_bdot(bf(t_inv[p]), wmat[p])) for p in pairs]
    uv = [jnp.concatenate([u[p], v2[p]], axis=0) for p in pairs]
    o2 = [nt(r2[p], s_b[p]) + _bdot(jnp.concatenate([a_rb[p], a_rk[p]], axis=1), uv[p]) for p in pairs]
    for p in pairs:
        upd = tn(uv[p], jnp.concatenate([stack(bl, p), stack(kl, p)], axis=0))
        s_ref[p] = s[p] * g_last[:, lanes[p]] + upd
    inv_n = 1.0 / A_HEAD_DIM
    o = [o2[p][0:L, :] + o2[p][L:P, :] for p in pairs]
    cen = [o[p] - head_sums(o[p]) * inv_n for p in pairs]
    var = [head_sums(cen[p] * cen[p]) * inv_n for p in pairs]
    for p in pairs:
        y = cen[p] * lax.rsqrt(var[p] + GN_EPS) * lnw_ref[:, lanes[p]] + lnb_ref[:, lanes[p]]
        y_ref[:, lanes[p]] = ((y + bonus[:, lanes[p]]) * g[:, lanes[p]]).astype(BF16)

    @pl.when(c == pl.num_programs(1) - 1)
    def _():
        for p in range(n_pairs):
            s = s_ref[p]
            s_out_ref[2 * p] = s[0:A_HEAD_DIM, 0:A_HEAD_DIM]
            s_out_ref[2 * p + 1] = s[A_HEAD_DIM:LANE, A_HEAD_DIM:LANE]


def _rwkv_chunked(proj, sp, lw, s0, B, T):
    l = lw["l"]
    row = lambda n: pl.BlockSpec((None, 1, n), lambda b, c: (l, 0, 0))
    lora = pl.BlockSpec((None, LORA_PAD, A_WIDTH), lambda b, c: (l, 0, 0))
    sspec = pl.BlockSpec((None, A_HEADS, A_HEAD_DIM, A_HEAD_DIM), lambda b, c: (b, 0, 0, 0))
    y, s_t = pl.pallas_call(
        _rwkv_chunk_kernel,
        grid=(B, T // CHUNK),
        in_specs=[pl.BlockSpec((None, CHUNK, A_PAD), lambda b, c: (b, c, 0)),
                  pl.BlockSpec((None, 1, A_PAD), lambda b, c: (b, 0, 0)),
                  row(A_PAD), row(A_WIDTH), row(A_WIDTH), row(A_WIDTH), row(A_WIDTH), row(A_WIDTH),
                  lora, lora, lora, row(A_WIDTH), row(A_WIDTH), sspec],
        out_specs=[pl.BlockSpec((None, CHUNK, A_WIDTH), lambda b, c: (b, c, 0)), sspec],
        out_shape=[jax.ShapeDtypeStruct((B, T, A_WIDTH), BF16),
                   jax.ShapeDtypeStruct((B, A_HEADS, A_HEAD_DIM, A_HEAD_DIM), F32)],
        scratch_shapes=[pltpu.VMEM((A_HEADS // 2, LANE, LANE), F32), pltpu.VMEM((1, A_PAD), F32)],
        compiler_params=_cparams(2),
        name="rwkv_chunk",
    )(proj.reshape(B, T, PROJ_W), sp, lw["mu"], lw["w0"], lw["a0"], lw["k_k"], lw["k_a"], lw["r_k"],
      lw["w2"], lw["a2"], lw["g2"], lw["ln_w"], lw["ln_b"], s0)
    return y.reshape(B * T, A_WIDTH), s_t


def _rwkv_out(o, bonus, g, ln_w, ln_b, bsel):
    inv_n = 1.0 / A_HEAD_DIM
    mean = _seg_sum(o, bsel) * inv_n
    c = o - mean
    var = _seg_sum(c * c, bsel) * inv_n
    y = c * lax.rsqrt(var + GN_EPS) * ln_w + ln_b
    return ((y + bonus) * g).astype(BF16)


def _rwkv_post_kernel(o_ref, bonus_ref, g_ref, lnw_ref, lnb_ref, bsel_ref, y_ref):
    y_ref[...] = _rwkv_out(o_ref[...].T, bonus_ref[...], g_ref[...], lnw_ref[...], lnb_ref[...], bsel_ref[...])


def _rwkv_post(o, bonus, g, lw, tm):
    M = bonus.shape[0]
    tile = pl.BlockSpec((tm, A_WIDTH), lambda i: (i, 0))
    o_tile = pl.BlockSpec((A_WIDTH, tm), lambda i: (0, i))
    l = lw["l"]
    row = pl.BlockSpec((None, 1, A_WIDTH), lambda i: (l, 0, 0))
    return pl.pallas_call(
        _rwkv_post_kernel,
        grid=(M // tm,),
        in_specs=[o_tile, tile, tile, row, row, pl.BlockSpec((SEG, SEG), lambda i: (0, 0))],
        out_specs=tile,
        out_shape=jax.ShapeDtypeStruct((M, A_WIDTH), BF16),
        compiler_params=_cparams(1),
        name="rwkv_post",
    )(o, bonus, g, lw["ln_w"], lw["ln_b"], lw["bsel"])


def _rope_kernel(x_ref, c_ref, s1_ref, s2_ref, o_ref):
    x = x_ref[...]
    fwd = pltpu.roll(x, x.shape[1] - ROPE_HALF, axis=1)
    bwd = pltpu.roll(x, ROPE_HALF, axis=1)
    o_ref[...] = x * c_ref[...] + fwd * s1_ref[...] + bwd * s2_ref[...]


def _rope(proj, tables, tm):
    M = proj.shape[0]
    n_tab = tables[0].shape[0] // tm
    tab = pl.BlockSpec((tm, SEG), lambda i, j: (i % n_tab, 0))
    return pl.pallas_call(
        _rope_kernel,
        grid=(M // tm, QK_W // SEG),
        in_specs=[pl.BlockSpec((tm, SEG), lambda i, j: (i, Q_OFF // SEG + j)), tab, tab, tab],
        out_specs=pl.BlockSpec((tm, SEG), lambda i, j: (i, j)),
        out_shape=jax.ShapeDtypeStruct((M, QK_W), F32),
        compiler_params=_cparams(2),
        name="rope",
    )(proj, *tables)


def _swa_prompt_kernel(q_ref, kc_ref, kp_ref, vc_ref, vp_ref, sink_ref, o_ref):
    n = pl.program_id(1)
    blocks = q_ref.shape[0] // WINDOW
    rows = GROUP * WINDOW
    i = lax.broadcasted_iota(jnp.int32, (rows, 2 * WINDOW), 0) % WINDOW
    j = lax.broadcasted_iota(jnp.int32, (rows, 2 * WINDOW), 1)
    diff = WINDOW + i - j
    band = (diff >= 0) & (diff < WINDOW)
    q = q_ref[...].astype(BF16)
    kall = jnp.concatenate([kp_ref[...], kc_ref[...]], axis=0).astype(BF16)
    vall = jnp.concatenate([vp_ref[...], vc_ref[...]], axis=0).astype(BF16)
    units = [(qb, kh) for qb in range(blocks) for kh in range(KV_HEADS)]
    ksl = [slice(kh * HEAD_DIM, (kh + 1) * HEAD_DIM) for kh in range(KV_HEADS)]
    mask = [band & ((n > 0) | (j >= WINDOW)) if qb == 0 else band for qb, _ in units]
    kk = [kall[qb * WINDOW:(qb + 2) * WINDOW, ksl[kh]] for qb, kh in units]
    vv = [vall[qb * WINDOW:(qb + 2) * WINDOW, ksl[kh]] for qb, kh in units]
    qh = [jnp.concatenate([q[qb * WINDOW:(qb + 1) * WINDOW, (kh * GROUP + g) * HEAD_DIM:(kh * GROUP + g + 1) * HEAD_DIM]
                           for g in range(GROUP)], axis=0) for qb, kh in units]
    us = range(len(units))
    sink = [sink_ref[kh] for _, kh in units]
    s = [lax.dot_general(qh[u], kk[u], (((1,), (1,)), ((), ())), preferred_element_type=F32) for u in us]
    s = [jnp.where(mask[u], s[u] * (HEAD_DIM ** -0.5), -jnp.inf) for u in us]
    m = [jnp.maximum(jnp.max(s[u], axis=1, keepdims=True), sink[u]) for u in us]
    e = [jnp.exp(s[u] - m[u]) for u in us]
    denom = [jnp.sum(e[u], axis=1, keepdims=True) + jnp.exp(sink[u] - m[u]) for u in us]
    o = [_bdot((e[u] / denom[u]).astype(BF16), vv[u]) for u in us]
    for qb in range(blocks):
        outs = [o[qb * KV_HEADS + kh][g * WINDOW:(g + 1) * WINDOW, :] for kh in range(KV_HEADS) for g in range(GROUP)]
        o_ref[qb * WINDOW:(qb + 1) * WINDOW, :] = jnp.concatenate(outs, axis=1).astype(o_ref.dtype)


def _swa_prompt(qk_rot, proj, lw, B, T):
    nb = T // WINDOW
    rows = GROUP * WINDOW
    kcol = Q_W // KV_W
    vcol = V_OFF // KV_W
    l = lw["l"]
    blocks = min(SWA_BLOCKS, nb)
    ng = nb // blocks
    span = blocks * WINDOW
    cur = lambda col: (lambda b, n: (b * ng + n, col))
    prev = lambda col: (lambda b, n: (b * nb + jnp.maximum(n * blocks - 1, 0), col))
    return pl.pallas_call(
        _swa_prompt_kernel,
        grid=(B, ng),
        in_specs=[pl.BlockSpec((span, Q_W), cur(0)),
                  pl.BlockSpec((span, KV_W), cur(kcol)),
                  pl.BlockSpec((WINDOW, KV_W), prev(kcol)),
                  pl.BlockSpec((span, KV_W), cur(vcol)),
                  pl.BlockSpec((WINDOW, KV_W), prev(vcol)),
                  pl.BlockSpec((None, KV_HEADS, rows, 1), lambda b, n: (l, 0, 0, 0))],
        out_specs=pl.BlockSpec((span, Q_W), cur(0)),
        out_shape=jax.ShapeDtypeStruct((B * T, Q_W), BF16),
        compiler_params=_cparams(2),
        name="swa_prompt",
    )(qk_rot, qk_rot, qk_rot, proj, proj, lw["sink_p"])


def _swa_sample_kernel(q_ref, kn_ref, vn_ref, ck_ref, cv_ref, sink_ref, *rest, steps, n_prev):
    if n_prev:
        pk_ref, pv_ref, o_ref, nk_ref, nv_ref = rest
        for li in range(n_prev):
            nk_ref[li] = pk_ref[li]
            nv_ref[li] = pv_ref[li]
    else:
        o_ref, nk_ref, nv_ref = rest
    first_new = SUBLANE - steps
    new_rows = lax.broadcasted_iota(jnp.int32, (SUBLANE, KV_W), 0) >= first_new
    for b in range(q_ref.shape[0]):
        for c_ref, n_ref, dst in ((ck_ref, kn_ref, nk_ref), (cv_ref, vn_ref, nv_ref)):
            rolled = pltpu.roll(c_ref[b], WINDOW - steps, axis=0)
            tail = jnp.where(new_rows, n_ref[b], rolled[WINDOW - SUBLANE:, :])
            dst[n_prev, b] = jnp.concatenate([rolled[:WINDOW - SUBLANE, :], tail], axis=0)
    rows = GROUP * steps
    t_row = lax.broadcasted_iota(jnp.int32, (2 * rows, WINDOW), 0) % steps
    c_col = lax.broadcasted_iota(jnp.int32, (2 * rows, WINDOW), 1)
    cache_mask = c_col > t_row
    t_col = lax.broadcasted_iota(jnp.int32, (2 * rows, 1), 0) % steps
    scale = HEAD_DIM ** -0.5
    zq = jnp.zeros((rows, HEAD_DIM), F32)
    units = [(b, p) for b in range(q_ref.shape[0]) for p in range(KV_HEADS // 2)]
    us = range(len(units))
    lanes = [slice(p * LANE, (p + 1) * LANE) for _, p in units]
    q2 = [jnp.concatenate([jnp.concatenate([q_ref[b, 2 * p], zq], axis=1),
                           jnp.concatenate([zq, q_ref[b, 2 * p + 1]], axis=1)], axis=0).astype(BF16) for b, p in units]
    kc = [ck_ref[b, :, lanes[u]].astype(BF16) for u, (b, p) in enumerate(units)]
    vc = [cv_ref[b, :, lanes[u]].astype(BF16) for u, (b, p) in enumerate(units)]
    kn = [kn_ref[b, first_new:, lanes[u]].astype(BF16).astype(F32) for u, (b, p) in enumerate(units)]
    vn = [vn_ref[b, first_new:, lanes[u]].astype(BF16).astype(F32) for u, (b, p) in enumerate(units)]
    sink = [jnp.concatenate([sink_ref[2 * p], sink_ref[2 * p + 1]], axis=0) for _, p in units]
    sc = [lax.dot_general(q2[u], kc[u], (((1,), (1,)), ((), ())), preferred_element_type=F32) for u in us]
    sc = [jnp.where(cache_mask, sc[u] * scale, -jnp.inf) for u in us]
    q2f = [q2[u].astype(F32) for u in us]
    sn = [[jnp.where(t_col >= t, jnp.sum(q2f[u] * kn[u][t:t + 1, :], axis=1, keepdims=True) * scale, -jnp.inf)
           for t in range(steps)] for u in us]
    m = [jnp.maximum(jnp.max(sc[u], axis=1, keepdims=True), sink[u]) for u in us]
    for t in range(steps):
        m = [jnp.maximum(m[u], sn[u][t]) for u in us]
    ec = [jnp.exp(sc[u] - m[u]) for u in us]
    en = [[jnp.exp(sn[u][t] - m[u]) for t in range(steps)] for u in us]
    denom = [jnp.sum(ec[u], axis=1, keepdims=True) + jnp.exp(sink[u] - m[u]) for u in us]
    for t in range(steps):
        denom = [denom[u] + en[u][t] for u in us]
    o = [_bdot((ec[u] / denom[u]).astype(BF16), vc[u]) for u in us]
    for t in range(steps):
        o = [o[u] + (en[u][t] / denom[u]).astype(BF16).astype(F32) * vn[u][t:t + 1, :] for u in us]
    for u, (b, p) in enumerate(units):
        o_ref[b, 2 * p] = o[u][0:rows, 0:HEAD_DIM]
        o_ref[b, 2 * p + 1] = o[u][rows:2 * rows, HEAD_DIM:LANE]


def _swa_sample(qk_rot, proj, cache_k, cache_v, prev_k, prev_v, lw, B, T):
    l = lw["l"]
    rows = GROUP * T
    qg = qk_rot[:, :Q_W].reshape(T, B, KV_HEADS, GROUP, HEAD_DIM).transpose(1, 2, 3, 0, 4)
    qg = qg.reshape(B, KV_HEADS, rows, HEAD_DIM)
    pad = ((0, 0), (SUBLANE - T, 0), (0, 0))
    kn = jnp.pad(qk_rot[:, Q_W:].reshape(T, B, KV_W).transpose(1, 0, 2), pad)
    vn = jnp.pad(proj[:, V_OFF:V_OFF + KV_W].reshape(T, B, KV_W).transpose(1, 0, 2), pad)
    nb = SUBLANE if B % SUBLANE == 0 else 1
    new_spec = pl.BlockSpec((nb, SUBLANE, KV_W), lambda b: (b, 0, 0))
    cache_spec = pl.BlockSpec((None, nb, WINDOW, KV_W), lambda b: (l, b, 0, 0))
    in_specs = [pl.BlockSpec((nb, KV_HEADS, rows, HEAD_DIM), lambda b: (b, 0, 0, 0)),
                new_spec, new_spec, cache_spec, cache_spec,
                pl.BlockSpec((None, KV_HEADS, rows, 1), lambda b: (l, 0, 0, 0))]
    args = [qg, kn, vn, cache_k, cache_v, lw["sink_s"]]
    if l:
        in_specs += [pl.BlockSpec((l, nb, WINDOW, KV_W), lambda b: (0, b, 0, 0))] * 2
        args += [prev_k, prev_v]
    stacked = pl.BlockSpec((l + 1, nb, WINDOW, KV_W), lambda b: (0, b, 0, 0))
    stacked_shape = jax.ShapeDtypeStruct((l + 1, B, WINDOW, KV_W), F32)
    o, new_k, new_v = pl.pallas_call(
        functools.partial(_swa_sample_kernel, steps=T, n_prev=l),
        grid=(B // nb,),
        in_specs=in_specs,
        out_specs=[pl.BlockSpec((nb, KV_HEADS, rows, HEAD_DIM), lambda b: (b, 0, 0, 0)), stacked, stacked],
        out_shape=[jax.ShapeDtypeStruct((B, KV_HEADS, rows, HEAD_DIM), F32), stacked_shape, stacked_shape],
        compiler_params=_cparams(1),
        name="swa_sample",
    )(*args)
    o = o.reshape(B, KV_HEADS, GROUP, T, HEAD_DIM).transpose(3, 0, 1, 2, 4).reshape(T * B, Q_W)
    return o.astype(BF16), new_k, new_v


def _rope_tables(pos):
    inv = 1.0 / (ROPE_THETA ** (jnp.arange(0, ROPE_DIM, 2, dtype=F32) / ROPE_DIM))
    ang = pos.astype(F32)[:, None] * inv[None, :]
    cos, sin = jnp.cos(ang), jnp.sin(ang)
    n = pos.shape[0]
    ones = jnp.ones((n, HEAD_DIM - ROPE_DIM), F32)
    zeros = jnp.zeros((n, HEAD_DIM - ROPE_DIM), F32)
    zh = jnp.zeros((n, ROPE_HALF), F32)
    c = jnp.concatenate([cos, cos, ones], axis=1)
    s1 = jnp.concatenate([-sin, zh, zeros], axis=1)
    s2 = jnp.concatenate([zh, sin, zeros], axis=1)
    return tuple(jnp.tile(a, (1, SEG // HEAD_DIM)) for a in (c, s1, s2))


def _stacked_params(t_sample, norm_g, ffn_w_gate, ffn_w_up, ffn_w_down, w_in, rwkv_mu, rwkv_w0, rwkv_w2,
                    rwkv_a0, rwkv_a2, rwkv_g2, rwkv_k_k, rwkv_k_a, rwkv_r_k, rwkv_ln_w, rwkv_ln_b, attn_sinks,
                    w_branch_a, w_branch_b, w_merge_gate, w_out):
    L = w_in.shape[0]
    w_proj = jnp.concatenate([w_in[:, :, :A_PROJ], jnp.zeros((L, D_MODEL, A_PAD - A_PROJ), w_in.dtype),
                              w_in[:, :, A_PROJ:]], axis=2)
    lora_rows = lambda w, off: jnp.pad(w, ((0, 0), (off, LORA_PAD - off - w.shape[1]), (0, 0))).astype(BF16)
    seg = jnp.arange(SEG) // A_HEAD_DIM
    row = lambda a: a.reshape(L, 1, -1)
    sink_rows = lambda n: jnp.repeat(attn_sinks.reshape(L, KV_HEADS, GROUP), n, axis=2).reshape(L, KV_HEADS, GROUP * n, 1)
    return {
        "norm_g": norm_g, "wg": ffn_w_gate, "wu": ffn_w_up, "wd": ffn_w_down, "w_proj": w_proj,
        "mu": row(jnp.pad(rwkv_mu, ((0, 0), (0, A_PAD - A_PROJ)))),
        "w0": row(rwkv_w0), "a0": row(rwkv_a0), "k_k": row(rwkv_k_k), "k_a": row(rwkv_k_a),
        "r_k": row(rwkv_r_k), "ln_w": row(rwkv_ln_w), "ln_b": row(rwkv_ln_b),
        "w2": lora_rows(rwkv_w2, 0), "a2": lora_rows(rwkv_a2, DECAY_LORA),
        "g2": lora_rows(rwkv_g2, DECAY_LORA + ICLR_LORA),
        "bsel": (seg[:, None] == seg[None, :]).astype(BF16),
        "sink_p": sink_rows(WINDOW), "sink_s": sink_rows(t_sample),
        "w_a": w_branch_a, "w_b": w_branch_b, "w_gate": w_merge_gate, "w_out": w_out,
    }


def _decoder_layer(x, mod, lw, B, T, shift_prev, tables, sample=None):
    M = B * T
    l = lw["l"]
    sp = jnp.pad(shift_prev, ((0, 0), (0, A_PAD - A_PROJ)))
    if sample is not None:
        tm = M
        mod = mod.reshape(1, B, N_MOD * D_MODEL)
        gmap = lambda tile: (lambda i: 0)
        sp = sp.reshape(1, B, A_PAD)
    else:
        tm = min(T, 1024)
        mod = mod.reshape(B, 1, N_MOD * D_MODEL)
        gmap = lambda tile: (lambda i: i // (T // tile))
        sp = sp.reshape(B, 1, A_PAD)
    tm_k = min(tm, 512)

    def ffn(x, n, which):
        h = _ada_norm(x, lw["norm_g"], l, n, mod, gmap(tm), tm)
        act = _ffn_gate_up(h, lw["wg"], lw["wu"], (l, which), tm)
        return _mm_residual(act, lw["wd"], (l, which), x, mod, 3 * n + 2, gmap(tm_k), 0.5, tm_k)

    x = ffn(x, 0, 0)

    h = _ada_norm(x, lw["norm_g"], l, 1, mod, gmap(tm), tm)
    proj = _mm_plain(h, lw["w_proj"], (l,), tm, tn=TN_WIDE, name="in_proj")
    gates = _mm_plain(h, lw["w_gate"], (l,), tm, tn=TN_WIDE, sigmoid=True, name="merge_gates")

    qk_rot = _rope(proj, tables, min(M, ROPE_ROWS))
    if sample is not None:
        state_wkv, cache_k, cache_v, prev_wkv, prev_k, prev_v = sample
        r, w, k2, v, kk, kka, g, bonus = _rwkv_prep(proj, sp, lw, B)
        o, new_wkv = _rwkv_lane_scan((r, w, k2, v, kk, kka), state_wkv, prev_wkv, l, B, T)
        ya = _rwkv_post(o, bonus, g, lw, B)
        new_shift = proj[(T - 1) * B:, :A_PROJ]
        yb, new_k, new_v = _swa_sample(qk_rot, proj, cache_k, cache_v, prev_k, prev_v, lw, B, T)
    else:
        zero_wkv = jnp.zeros((B, A_HEADS, A_HEAD_DIM, A_HEAD_DIM), F32)
        ya, new_wkv = _rwkv_chunked(proj, sp, lw, zero_wkv, B, T)
        new_shift = proj.reshape(B, T, PROJ_W)[:, -1, :A_PROJ]
        yb = _swa_prompt(qk_rot, proj, lw, B, T)
        new_k = qk_rot[:, Q_W:].reshape(B, T, KV_HEADS, HEAD_DIM)[:, -WINDOW:]
        new_v = proj[:, V_OFF:V_OFF + KV_W].reshape(B, T, KV_HEADS, HEAD_DIM)[:, -WINDOW:]

    merged = _mm_merge(ya, yb, lw["w_a"], lw["w_b"], (l,), gates, tm, tn=TN_WIDE)
    x = _mm_residual(merged, lw["w_out"], (l,), x, mod, 5, gmap(tm), 1.0, tm, tn=TN_WIDE)

    x = ffn(x, 2, 1)
    return x, new_shift, new_wkv, new_k, new_v


def kernel(x_prompt, x_sample, c_prompt, c_sample, state_shift, state_wkv, cache_k, cache_v, norm_g, w_mod, b_mod,
           ffn_w_gate, ffn_w_up, ffn_w_down, w_in, rwkv_mu, rwkv_w0, rwkv_w2, rwkv_a0, rwkv_a2, rwkv_g2, rwkv_k_k,
           rwkv_k_a, rwkv_r_k, rwkv_ln_w, rwkv_ln_b, attn_sinks, w_branch_a, w_branch_b, w_merge_gate, w_out,
           final_norm):
    Bp, Tp, _ = x_prompt.shape
    Bs, Ts, _ = x_sample.shape
    L = w_in.shape[0]
    tab_p = _rope_tables(jnp.arange(Tp))
    tab_s = _rope_tables(jnp.repeat(PAST_LEN + jnp.arange(Ts), Bs))
    zero_shift = jnp.zeros((Bp, A_PROJ), F32)
    xp = x_prompt.reshape(Bp * Tp, D_MODEL)
    xs = x_sample.transpose(1, 0, 2).reshape(Ts * Bs, D_MODEL)
    c_all = jnp.concatenate([c_prompt, c_sample], axis=0)
    pad_rows = (-c_all.shape[0]) % 16
    c_all = jnp.pad(c_all, ((0, pad_rows), (0, 0)))
    pw = _stacked_params(Ts, norm_g, ffn_w_gate, ffn_w_up, ffn_w_down, w_in, rwkv_mu, rwkv_w0, rwkv_w2, rwkv_a0,
                         rwkv_a2, rwkv_g2, rwkv_k_k, rwkv_k_a, rwkv_r_k, rwkv_ln_w, rwkv_ln_b, attn_sinks,
                         w_branch_a, w_branch_b, w_merge_gate, w_out)
    state_s = state_wkv.reshape(L, Bs, A_HEADS * A_HEAD_DIM * A_HEAD_DIM)
    ck = cache_k.reshape(L, Bs, WINDOW, KV_W)
    cv = cache_v.reshape(L, Bs, WINDOW, KV_W)
    outs_p, shifts_s = [], []
    wkv_s = k_s = v_s = None
    for l in range(L):
        lw = dict(pw, l=l)
        mod = _modulation(c_all, w_mod, b_mod, l)
        xp, *st_p = _decoder_layer(xp, mod[:Bp], lw, Bp, Tp, zero_shift, tab_p)
        xs, shift_s, wkv_s, k_s, v_s = _decoder_layer(xs, mod[Bp:Bp + Bs], lw, Bs, Ts, state_shift[l], tab_s,
                                                      sample=(state_s, ck, cv, wkv_s, k_s, v_s))
        outs_p.append(st_p)
        shifts_s.append(shift_s)
    y_prompt = _final_norm(xp, final_norm, 1024).reshape(Bp, Tp, D_MODEL)
    y_sample = _final_norm(xs, final_norm, Bs * Ts).reshape(Ts, Bs, D_MODEL).transpose(1, 0, 2)
    stack = lambda n: jnp.stack([o[n] for o in outs_p])
    return (y_prompt, y_sample, stack(0), stack(1), stack(2), stack(3),
            jnp.stack(shifts_s), wkv_s.reshape(state_wkv.shape), k_s.reshape(cache_k.shape),
            v_s.reshape(cache_v.shape))
```

```python
import functools

import jax
import jax.numpy as jnp
from jax import lax
from jax.experimental import pallas as pl
from jax.experimental.pallas import tpu as pltpu

D_MODEL = 2048
DEPTH = 2
A_HEADS = 16
A_HEAD_DIM = 64
A_WIDTH = A_HEADS * A_HEAD_DIM
DECAY_LORA = 64
ICLR_LORA = 64
GATE_LORA = 160
LORA_W = DECAY_LORA + ICLR_LORA + GATE_LORA
A_PROJ = 3 * A_WIDTH + LORA_W
GN_EPS = 64e-5
Q_HEADS = 16
KV_HEADS = 4
GROUP = Q_HEADS // KV_HEADS
HEAD_DIM = 64
Q_W = Q_HEADS * HEAD_DIM
KV_W = KV_HEADS * HEAD_DIM
WINDOW = 128
ROPE_DIM = HEAD_DIM // 4
ROPE_HALF = ROPE_DIM // 2
ROPE_THETA = 500000.0
D_FF = 5632
N_MOD = 9
RMS_EPS = 1e-6

PAST_LEN = 8192
LANE = 128
SUBLANE = 8
ROPE_ROWS = 2048
SWA_BLOCKS = 4
CHUNK = 64
SOLVE_BASE = 8
SEG = 256
A_PAD = 3584
LORA_PAD = A_PAD - 3 * A_WIDTH
LORA_DI = 128
LORA_G = 256
assert DECAY_LORA + ICLR_LORA == LORA_DI and GATE_LORA <= LORA_G and LORA_DI + LORA_G <= LORA_PAD
Q_OFF = A_PAD
K_OFF = Q_OFF + Q_W
V_OFF = K_OFF + KV_W
PROJ_W = V_OFF + KV_W
QK_W = Q_W + KV_W
VMEM_LIMIT = 56 * 1024 * 1024
TN_WIDE = 1024

F32 = jnp.float32
BF16 = jnp.bfloat16


def _cparams(n_axes):
    return pltpu.CompilerParams(dimension_semantics=("arbitrary",) * n_axes,
                                vmem_limit_bytes=VMEM_LIMIT)


def _bdot(a, b):
    return jnp.dot(a, b, preferred_element_type=F32)


def _seg_sum(x, bsel):
    hi = x.astype(BF16)
    lo = (x - hi.astype(F32)).astype(BF16)
    seg = bsel.shape[0]
    parts = []
    for c in range(x.shape[1] // seg):
        sl = slice(c * seg, (c + 1) * seg)
        parts.append(_bdot(hi[:, sl], bsel) + _bdot(lo[:, sl], bsel))
    return parts[0] if len(parts) == 1 else jnp.concatenate(parts, axis=1)


def _mod_kernel(c_ref, w_ref, b_ref, o_ref):
    c = c_ref[...]
    a = (c * jax.nn.sigmoid(c)).astype(BF16)
    o_ref[...] = _bdot(a, w_ref[...].astype(BF16)) + b_ref[...]


def _modulation(c, w_mod, b_mod, l):
    R = c.shape[0]
    N = w_mod.shape[-1]
    tn = 1024
    return pl.pallas_call(
        _mod_kernel,
        grid=(N // tn,),
        in_specs=[pl.BlockSpec((R, D_MODEL), lambda j: (0, 0)),
                  pl.BlockSpec((None, D_MODEL, tn), lambda j: (l, 0, j)),
                  pl.BlockSpec((None, 1, tn), lambda j: (l, 0, j))],
        out_specs=pl.BlockSpec((R, tn), lambda j: (0, j)),
        out_shape=jax.ShapeDtypeStruct((R, N), F32),
        compiler_params=_cparams(1),
        name="modulation",
    )(c, w_mod, b_mod.reshape(b_mod.shape[0], 1, N))


def _mod_rows(ref, tm):
    val = ref[...]
    reps = tm // val.shape[0]
    return val if val.shape[0] == 1 or reps == 1 else jnp.concatenate([val] * reps, axis=0)


def _ada_norm_kernel(x_ref, g_ref, sh_ref, sc_ref, o_ref):
    x = x_ref[...]
    tm = x.shape[0]
    y = x * lax.rsqrt(jnp.mean(x * x, axis=-1, keepdims=True) + RMS_EPS)
    y = y * g_ref[...]
    o_ref[...] = (y * (1.0 + _mod_rows(sc_ref, tm)) + _mod_rows(sh_ref, tm)).astype(o_ref.dtype)


def _ada_norm(x, norm_g, l, n, mod, gmap, tm):
    M = x.shape[0]
    R = mod.shape[1]
    mspec = lambda col: pl.BlockSpec((None, R, D_MODEL), lambda i: (gmap(i), 0, col))
    return pl.pallas_call(
        _ada_norm_kernel,
        grid=(M // tm,),
        in_specs=[pl.BlockSpec((tm, D_MODEL), lambda i: (i, 0)),
                  pl.BlockSpec((None, None, 1, D_MODEL), lambda i: (l, n, 0, 0)),
                  mspec(3 * n), mspec(3 * n + 1)],
        out_specs=pl.BlockSpec((tm, D_MODEL), lambda i: (i, 0)),
        out_shape=jax.ShapeDtypeStruct((M, D_MODEL), BF16),
        compiler_params=_cparams(1),
        name="ada_norm",
    )(x, norm_g.reshape(norm_g.shape[0], norm_g.shape[1], 1, D_MODEL), mod, mod)


def _final_norm_kernel(x_ref, g_ref, o_ref):
    x = x_ref[...]
    y = x * lax.rsqrt(jnp.mean(x * x, axis=-1, keepdims=True) + RMS_EPS)
    o_ref[...] = y * g_ref[...]


def _final_norm(x, gain, tm):
    M = x.shape[0]
    return pl.pallas_call(
        _final_norm_kernel,
        grid=(M // tm,),
        in_specs=[pl.BlockSpec((tm, D_MODEL), lambda i: (i, 0)),
                  pl.BlockSpec((1, D_MODEL), lambda i: (0, 0))],
        out_specs=pl.BlockSpec((tm, D_MODEL), lambda i: (i, 0)),
        out_shape=jax.ShapeDtypeStruct((M, D_MODEL), F32),
        compiler_params=_cparams(1),
        name="final_norm",
    )(x, gain.reshape(1, D_MODEL))


def _wspec(w, lead, tn):
    return pl.BlockSpec((None,) * len(lead) + (w.shape[-2], tn), lambda j, i: lead + (0, j))


def _wdot(a, w_ref):
    return _bdot(a, w_ref[...].astype(BF16))


def _gu_kernel(h_ref, wg_ref, wu_ref, o_ref):
    h = h_ref[...]
    g = _wdot(h, wg_ref)
    u = _wdot(h, wu_ref)
    o_ref[...] = ((g * jax.nn.sigmoid(g)) * u).astype(o_ref.dtype)


def _ffn_gate_up(h, wg, wu, lead, tm, tn=512):
    M, K = h.shape
    N = wg.shape[-1]
    return pl.pallas_call(
        _gu_kernel,
        grid=(N // tn, M // tm),
        in_specs=[pl.BlockSpec((tm, K), lambda j, i: (i, 0)), _wspec(wg, lead, tn), _wspec(wu, lead, tn)],
        out_specs=pl.BlockSpec((tm, tn), lambda j, i: (i, j)),
        out_shape=jax.ShapeDtypeStruct((M, N), BF16),
        compiler_params=_cparams(2),
        name="ffn_gate_up",
    )(h, wg, wu)


def _resid_kernel(a_ref, w_ref, x_ref, gate_ref, o_ref, *, coef):
    y = _wdot(a_ref[...], w_ref)
    o_ref[...] = x_ref[...] + (coef * _mod_rows(gate_ref, y.shape[0])) * y


def _mm_residual(a, w, lead, x, mod, n_gate, gmap, coef, tm, tn=512):
    M, K = a.shape
    N = w.shape[-1]
    R = mod.shape[1]
    nj = N // tn
    return pl.pallas_call(
        functools.partial(_resid_kernel, coef=coef),
        grid=(nj, M // tm),
        in_specs=[pl.BlockSpec((tm, K), lambda j, i: (i, 0)),
                  _wspec(w, lead, tn),
                  pl.BlockSpec((tm, tn), lambda j, i: (i, j)),
                  pl.BlockSpec((None, R, tn), lambda j, i: (gmap(i), 0, n_gate * nj + j))],
        out_specs=pl.BlockSpec((tm, tn), lambda j, i: (i, j)),
        out_shape=jax.ShapeDtypeStruct((M, N), F32),
        compiler_params=_cparams(2),
        name="mm_residual",
    )(a, w, x, mod)


def _plain_kernel(a_ref, w_ref, o_ref, *, sigmoid):
    y = _wdot(a_ref[...], w_ref)
    if sigmoid:
        y = jax.nn.sigmoid(y)
    o_ref[...] = y.astype(o_ref.dtype)


def _mm_plain(a, w, lead, tm, tn=512, sigmoid=False, name="mm_plain"):
    M, K = a.shape
    N = w.shape[-1]
    return pl.pallas_call(
        functools.partial(_plain_kernel, sigmoid=sigmoid),
        grid=(N // tn, M // tm),
        in_specs=[pl.BlockSpec((tm, K), lambda j, i: (i, 0)), _wspec(w, lead, tn)],
        out_specs=pl.BlockSpec((tm, tn), lambda j, i: (i, j)),
        out_shape=jax.ShapeDtypeStruct((M, N), F32),
        compiler_params=_cparams(2),
        name=name,
    )(a, w)


def _merge_kernel(ya_ref, yb_ref, wa_ref, wb_ref, ga_ref, gb_ref, o_ref):
    pa = _wdot(ya_ref[...], wa_ref)
    pb = _wdot(yb_ref[...], wb_ref)
    o_ref[...] = (ga_ref[...] * pa + gb_ref[...] * pb).astype(o_ref.dtype)


def _mm_merge(ya, yb, wa, wb, lead, gates, tm, tn=512):
    M, K = ya.shape
    N = wa.shape[-1]
    nj = N // tn
    return pl.pallas_call(
        _merge_kernel,
        grid=(nj, M // tm),
        in_specs=[pl.BlockSpec((tm, K), lambda j, i: (i, 0)),
                  pl.BlockSpec((tm, K), lambda j, i: (i, 0)),
                  _wspec(wa, lead, tn), _wspec(wb, lead, tn),
                  pl.BlockSpec((tm, tn), lambda j, i: (i, j)),
                  pl.BlockSpec((tm, tn), lambda j, i: (i, j + nj))],
        out_specs=pl.BlockSpec((tm, tn), lambda j, i: (i, j)),
        out_shape=jax.ShapeDtypeStruct((M, N), BF16),
        compiler_params=_cparams(2),
        name="mm_merge",
    )(ya, yb, wa, wb, gates, gates)


def _rwkv_operands(pa, prev, mu, w0, a0, k_k, k_a, r_k, w2_ref, a2_ref, g2_ref, head_sums):
    xm = pa + (prev - pa) * mu
    r = xm[:, 0:A_WIDTH]
    k = xm[:, A_WIDTH:2 * A_WIDTH]
    v = xm[:, 2 * A_WIDTH:3 * A_WIDTH]
    lo_di = xm[:, 3 * A_WIDTH:3 * A_WIDTH + LORA_DI]
    lo_g = xm[:, 3 * A_WIDTH + LORA_DI:3 * A_WIDTH + LORA_DI + LORA_G]
    m_w = _bdot(jnp.tanh(lo_di).astype(BF16), w2_ref[0:LORA_DI, :])
    m_a = _bdot(lo_di.astype(BF16), a2_ref[0:LORA_DI, :])
    g = _bdot(jax.nn.sigmoid(lo_g).astype(BF16), g2_ref[LORA_DI:LORA_DI + LORA_G, :])
    z = -(w0 + m_w)
    softplus = jnp.maximum(z, 0.0) + jnp.log1p(jnp.exp(-jnp.abs(z)))
    w_log = -softplus - 0.5
    a = jax.nn.sigmoid(a0 + m_a)
    kkf = k * k_k
    kk = kkf / jnp.maximum(jnp.sqrt(head_sums(kkf * kkf)), 1e-12)
    k2 = k * (1.0 + (a - 1.0) * k_a)
    bonus = head_sums(r * k2 * r_k) * v
    return r, -jnp.exp(w_log), k2, v, kk, kk * a, g, bonus


def _rwkv_prep_kernel(p_ref, sp_ref, mu_ref, w0_ref, a0_ref, kk_ref, ka_ref, rk_ref,
                      w2_ref, a2_ref, g2_ref, bsel_ref,
                      r_out, d_out, k_out, v_out, kk_out, kka_out, g_out, bonus_out, carry_ref):
    t = pl.program_id(0)
    pa = p_ref[...]

    @pl.when(t == 0)
    def _():
        carry_ref[...] = sp_ref[...]

    prev = carry_ref[...]
    carry_ref[...] = pa
    bsel = bsel_ref[...]
    r, w, k2, v, kk, kka, g, bonus = _rwkv_operands(
        pa, prev, mu_ref[...], w0_ref[...], a0_ref[...], kk_ref[...], ka_ref[...], rk_ref[...],
        w2_ref, a2_ref, g2_ref, lambda x: _seg_sum(x, bsel))
    for ref, val in ((r_out, r), (d_out, w), (k_out, k2), (v_out, v), (kk_out, kk), (kka_out, kka)):
        ref[...] = val.T
    g_out[...] = g
    bonus_out[...] = bonus


def _rwkv_prep(proj, sp, lw, n_seq):
    M = proj.shape[0]
    l = lw["l"]
    row = lambda n: pl.BlockSpec((None, 1, n), lambda i: (l, 0, 0))
    full = lambda a, b: pl.BlockSpec((None, a, b), lambda i: (l, 0, 0))
    tok_spec = pl.BlockSpec((n_seq, A_WIDTH), lambda i: (i, 0))
    tok_shape = jax.ShapeDtypeStruct((M, A_WIDTH), F32)
    op_spec = pl.BlockSpec((A_WIDTH, n_seq), lambda i: (0, i))
    op_shape = jax.ShapeDtypeStruct((A_WIDTH, M), F32)
    return pl.pallas_call(
        _rwkv_prep_kernel,
        grid=(M // n_seq,),
        in_specs=[pl.BlockSpec((n_seq, A_PAD), lambda i: (i, 0)),
                  pl.BlockSpec((None, n_seq, A_PAD), lambda i: (0, 0, 0)),
                  row(A_PAD), row(A_WIDTH), row(A_WIDTH), row(A_WIDTH), row(A_WIDTH), row(A_WIDTH),
                  full(LORA_PAD, A_WIDTH), full(LORA_PAD, A_WIDTH), full(LORA_PAD, A_WIDTH),
                  pl.BlockSpec((SEG, SEG), lambda i: (0, 0))],
        out_specs=[op_spec] * 6 + [tok_spec] * 2,
        out_shape=[op_shape] * 6 + [tok_shape] * 2,
        scratch_shapes=[pltpu.VMEM((n_seq, A_PAD), F32)],
        compiler_params=_cparams(1),
        name="rwkv_prep",
    )(proj, sp, lw["mu"], lw["w0"], lw["a0"], lw["k_k"], lw["k_a"], lw["r_k"],
      lw["w2"], lw["a2"], lw["g2"], lw["bsel"])


def _rwkv_lane_scan_kernel(r_ref, w_ref, k_ref, v_ref, kk_ref, kka_ref, s0_ref, *rest, steps, n_prev):
    if n_prev:
        prev_ref, o_ref, s_out_ref, st_ref = rest
        for li in range(n_prev):
            s_out_ref[li] = prev_ref[li]
    else:
        o_ref, s_out_ref, st_ref = rest
    n = A_HEAD_DIM
    nb = s0_ref.shape[0]
    st_ref[...] = s0_ref[...].T
    tsl = [slice(t * nb, (t + 1) * nb) for t in range(steps)]

    def group(g, carry):
        rows8 = pl.ds(pl.multiple_of(g * SUBLANE, SUBLANE), SUBLANE)
        v8 = [v_ref[rows8, tsl[t]] for t in range(steps)]
        o8 = [[] for _ in range(steps)]
        for u in range(SUBLANE):
            srows = pl.ds(pl.multiple_of((g * SUBLANE + u) * n, n), n)
            s = st_ref[srows, :]
            for t in range(steps):
                sk = jnp.sum(s * kk_ref[:, tsl[t]], axis=0, keepdims=True)
                s = s * jnp.exp(w_ref[:, tsl[t]]) - sk * kka_ref[:, tsl[t]] + v8[t][u:u + 1, :] * k_ref[:, tsl[t]]
                o8[t].append(jnp.sum(s * r_ref[:, tsl[t]], axis=0, keepdims=True))
            st_ref[srows, :] = s
        for t in range(steps):
            o_ref[rows8, tsl[t]] = jnp.concatenate(o8[t], axis=0)
        return carry

    lax.fori_loop(0, n // SUBLANE, group, 0)
    s_out_ref[n_prev] = st_ref[...].T


def _rwkv_lane_scan(ops_t, state, prev, l, B, T):
    n2 = A_HEAD_DIM * A_HEAD_DIM
    ospec = pl.BlockSpec((A_HEAD_DIM, T * B), lambda h: (h, 0))
    in_specs = [ospec] * 6 + [pl.BlockSpec((None, B, n2), lambda h: (l, 0, h))]
    args = list(ops_t) + [state]
    if l:
        in_specs.append(pl.BlockSpec((l, B, n2), lambda h: (0, 0, h)))
        args.append(prev)
    return pl.pallas_call(
        functools.partial(_rwkv_lane_scan_kernel, steps=T, n_prev=l),
        grid=(A_HEADS,),
        in_specs=in_specs,
        out_specs=[ospec, pl.BlockSpec((l + 1, B, n2), lambda h: (0, 0, h))],
        out_shape=[jax.ShapeDtypeStruct((A_WIDTH, T * B), F32),
                   jax.ShapeDtypeStruct((l + 1, B, A_HEADS * n2), F32)],
        scratch_shapes=[pltpu.VMEM((n2, B), F32)],
        compiler_params=_cparams(1),
        name="rwkv_lane_scan",
    )(*args)


def _rwkv_chunk_kernel(p_ref, sp_ref, mu_ref, w0_ref, a0_ref, kk_ref, ka_ref, rk_ref, w2_ref, a2_ref, g2_ref,
                       lnw_ref, lnb_ref, s0_ref, y_ref, s_out_ref, s_ref, carry_ref):
    c = pl.program_id(1)
    L = CHUNK
    P = 2 * L
    n_pairs = A_HEADS // 2
    left = lax.broadcasted_iota(jnp.int32, (L, LANE), 1) < A_HEAD_DIM
    row = lax.broadcasted_iota(jnp.int32, (P, P), 0)
    col = lax.broadcasted_iota(jnp.int32, (P, P), 1)
    strict = row > col
    lower = row >= col
    eye = (row == col).astype(F32)
    same = lambda n: (row // n) == (col // n)
    base = same(SOLVE_BASE)
    levels = []
    n = SOLVE_BASE
    while n < L:
        levels.append(same(2 * n) & jnp.logical_not(same(n)))
        n *= 2
    bf = lambda x: x.astype(BF16)
    nt = lambda a, b: lax.dot_general(a, b, (((1,), (1,)), ((), ())), preferred_element_type=F32)
    tn = lambda a, b: lax.dot_general(a, b, (((0,), (0,)), ((), ())), preferred_element_type=F32)

    pairs = range(n_pairs)
    lanes = [slice(p * LANE, (p + 1) * LANE) for p in pairs]

    @pl.when(c == 0)
    def _():
        z = jnp.zeros((A_HEAD_DIM, A_HEAD_DIM), F32)
        for p in range(n_pairs):
            s_ref[p] = jnp.concatenate([jnp.concatenate([s0_ref[2 * p], z], axis=1),
                                        jnp.concatenate([z, s0_ref[2 * p + 1]], axis=1)], axis=0)
        carry_ref[...] = sp_ref[...]

    def head_sums(x):
        sa = jnp.sum(jnp.where(left, x, 0.0), axis=1, keepdims=True)
        sb = jnp.sum(jnp.where(left, 0.0, x), axis=1, keepdims=True)
        return jnp.where(left, sa, sb)

    pa = p_ref[...]
    first_row = lax.broadcasted_iota(jnp.int32, pa.shape, 0) == 0
    prev = jnp.where(first_row, carry_ref[...], pltpu.roll(pa, 1, axis=0))
    carry_ref[...] = pa[L - 1:L, :]
    r, w, k, v, kk, kka, g, bonus = _rwkv_operands(
        pa, prev, mu_ref[...], w0_ref[...], a0_ref[...], kk_ref[...], ka_ref[...], rk_ref[...],
        w2_ref, a2_ref, g2_ref, lambda x: jnp.concatenate([head_sums(x[:, sl]) for sl in lanes], axis=1))

    tl = lax.broadcasted_iota(jnp.int32, (L, L), 0) >= lax.broadcasted_iota(jnp.int32, (L, L), 1)
    tri = tl.astype(BF16)
    w_hi = bf(w)
    w_r1 = w - w_hi.astype(F32)
    w_mid = bf(w_r1)
    w_lo = bf(w_r1 - w_mid.astype(F32))
    cum = _bdot(tri, w_hi) + _bdot(tri, w_mid) + _bdot(tri, w_lo)
    g_in = jnp.exp(cum)
    g_inv = jnp.exp(-cum)
    g_last = g_in[L - 1:L, :]
    at = -kk * jnp.exp(cum - w)
    bt = kka * g_inv
    kt = k * g_inv
    rt = r * g_in
    bl = bt * g_last
    kl = kt * g_last


    def stack(x, p):
        xp = x[:, lanes[p]]
        return bf(jnp.concatenate([jnp.where(left, xp, 0.0), jnp.where(left, 0.0, xp)], axis=0))

    a2 = [stack(at, p) for p in pairs]
    r2 = [stack(rt, p) for p in pairs]
    b2 = [stack(bt, p) for p in pairs]
    k2 = [stack(kt, p) for p in pairs]
    v2 = [stack(v, p) for p in pairs]
    sc = [nt(jnp.concatenate([a2[p], r2[p]], axis=0), jnp.concatenate([b2[p], k2[p]], axis=0)) for p in pairs]
    a_ab = [jnp.where(strict, sc[p][0:P, 0:P], 0.0) for p in pairs]
    a_ak = [bf(jnp.where(strict, sc[p][0:P, P:2 * P], 0.0)) for p in pairs]
    a_rb = [bf(jnp.where(lower, sc[p][P:2 * P, 0:P], 0.0)) for p in pairs]
    a_rk = [bf(jnp.where(lower, sc[p][P:2 * P, P:2 * P], 0.0)) for p in pairs]
    pw = [bf(jnp.where(base, a_ab[p], 0.0)) for p in pairs]
    t_inv = [eye + pw[p].astype(F32) for p in pairs]
    m = 2
    while m < SOLVE_BASE:
        pw = [bf(_bdot(pw[p], pw[p])) for p in pairs]
        t_inv = [t_inv[p] + _bdot(bf(t_inv[p]), pw[p]) for p in pairs]
        m *= 2
    for lv in levels:
        half = [bf(_bdot(bf(t_inv[p]), bf(jnp.where(lv, a_ab[p], 0.0)))) for p in pairs]
        t_inv = [t_inv[p] + _bdot(half[p], bf(t_inv[p])) for p in pairs]
    s = [s_ref[p] for p in pairs]
    s_b = [bf(s[p]) for p in pairs]
    wmat = [bf(nt(a2[p], s_b[p]) + _bdot(a_ak[p], v2[p])) for p in pairs]
    u = [bf(_bdot(bf(t_inv[p]), wmat[p])) for p in pairs]
    uv = [jnp.concatenate([u[p], v2[p]], axis=0) for p in pairs]
    o2 = [nt(r2[p], s_b[p]) + _bdot(jnp.concatenate([a_rb[p], a_rk[p]], axis=1), uv[p]) for p in pairs]
    for p in pairs:
        upd = tn(uv[p], jnp.concatenate([stack(bl, p), stack(kl, p)], axis=0))
        s_ref[p] = s[p] * g_last[:, lanes[p]] + upd
    inv_n = 1.0 / A_HEAD_DIM
    o = [o2[p][0:L, :] + o2[p][L:P, :] for p in pairs]
    cen = [o[p] - head_sums(o[p]) * inv_n for p in pairs]
    var = [head_sums(cen[p] * cen[p]) * inv_n for p in pairs]
    for p in pairs:
        y = cen[p] * lax.rsqrt(var[p] + GN_EPS) * lnw_ref[:, lanes[p]] + lnb_ref[:, lanes[p]]
        y_ref[:, lanes[p]] = ((y + bonus[:, lanes[p]]) * g[:, lanes[p]]).astype(BF16)

    @pl.when(c == pl.num_programs(1) - 1)
    def _():
        for p in range(n_pairs):
            s = s_ref[p]
            s_out_ref[2 * p] = s[0:A_HEAD_DIM, 0:A_HEAD_DIM]
            s_out_ref[2 * p + 1] = s[A_HEAD_DIM:LANE, A_HEAD_DIM:LANE]


def _rwkv_chunked(proj, sp, lw, s0, B, T):
    l = lw["l"]
    row = lambda n: pl.BlockSpec((None, 1, n), lambda b, c: (l, 0, 0))
    lora = pl.BlockSpec((None, LORA_PAD, A_WIDTH), lambda b, c: (l, 0, 0))
    sspec = pl.BlockSpec((None, A_HEADS, A_HEAD_DIM, A_HEAD_DIM), lambda b, c: (b, 0, 0, 0))
    y, s_t = pl.pallas_call(
        _rwkv_chunk_kernel,
        grid=(B, T // CHUNK),
        in_specs=[pl.BlockSpec((None, CHUNK, A_PAD), lambda b, c: (b, c, 0)),
                  pl.BlockSpec((None, 1, A_PAD), lambda b, c: (b, 0, 0)),
                  row(A_PAD), row(A_WIDTH), row(A_WIDTH), row(A_WIDTH), row(A_WIDTH), row(A_WIDTH),
                  lora, lora, lora, row(A_WIDTH), row(A_WIDTH), sspec],
        out_specs=[pl.BlockSpec((None, CHUNK, A_WIDTH), lambda b, c: (b, c, 0)), sspec],
        out_shape=[jax.ShapeDtypeStruct((B, T, A_WIDTH), BF16),
                   jax.ShapeDtypeStruct((B, A_HEADS, A_HEAD_DIM, A_HEAD_DIM), F32)],
        scratch_shapes=[pltpu.VMEM((A_HEADS // 2, LANE, LANE), F32), pltpu.VMEM((1, A_PAD), F32)],
        compiler_params=_cparams(2),
        name="rwkv_chunk",
    )(proj.reshape(B, T, PROJ_W), sp, lw["mu"], lw["w0"], lw["a0"], lw["k_k"], lw["k_a"], lw["r_k"],
      lw["w2"], lw["a2"], lw["g2"], lw["ln_w"], lw["ln_b"], s0)
    return y.reshape(B * T, A_WIDTH), s_t


def _rwkv_out(o, bonus, g, ln_w, ln_b, bsel):
    inv_n = 1.0 / A_HEAD_DIM
    mean = _seg_sum(o, bsel) * inv_n
    c = o - mean
    var = _seg_sum(c * c, bsel) * inv_n
    y = c * lax.rsqrt(var + GN_EPS) * ln_w + ln_b
    return ((y + bonus) * g).astype(BF16)


def _rwkv_post_kernel(o_ref, bonus_ref, g_ref, lnw_ref, lnb_ref, bsel_ref, y_ref):
    y_ref[...] = _rwkv_out(o_ref[...].T, bonus_ref[...], g_ref[...], lnw_ref[...], lnb_ref[...], bsel_ref[...])


def _rwkv_post(o, bonus, g, lw, tm):
    M = bonus.shape[0]
    tile = pl.BlockSpec((tm, A_WIDTH), lambda i: (i, 0))
    o_tile = pl.BlockSpec((A_WIDTH, tm), lambda i: (0, i))
    l = lw["l"]
    row = pl.BlockSpec((None, 1, A_WIDTH), lambda i: (l, 0, 0))
    return pl.pallas_call(
        _rwkv_post_kernel,
        grid=(M // tm,),
        in_specs=[o_tile, tile, tile, row, row, pl.BlockSpec((SEG, SEG), lambda i: (0, 0))],
        out_specs=tile,
        out_shape=jax.ShapeDtypeStruct((M, A_WIDTH), BF16),
        compiler_params=_cparams(1),
        name="rwkv_post",
    )(o, bonus, g, lw["ln_w"], lw["ln_b"], lw["bsel"])


def _rope_kernel(x_ref, c_ref, s1_ref, s2_ref, o_ref):
    x = x_ref[...]
    fwd = pltpu.roll(x, x.shape[1] - ROPE_HALF, axis=1)
    bwd = pltpu.roll(x, ROPE_HALF, axis=1)
    o_ref[...] = x * c_ref[...] + fwd * s1_ref[...] + bwd * s2_ref[...]


def _rope(proj, tables, tm):
    M = proj.shape[0]
    n_tab = tables[0].shape[0] // tm
    tab = pl.BlockSpec((tm, SEG), lambda i, j: (i % n_tab, 0))
    return pl.pallas_call(
        _rope_kernel,
        grid=(M // tm, QK_W // SEG),
        in_specs=[pl.BlockSpec((tm, SEG), lambda i, j: (i, Q_OFF // SEG + j)), tab, tab, tab],
        out_specs=pl.BlockSpec((tm, SEG), lambda i, j: (i, j)),
        out_shape=jax.ShapeDtypeStruct((M, QK_W), F32),
        compiler_params=_cparams(2),
        name="rope",
    )(proj, *tables)


def _swa_prompt_kernel(q_ref, kc_ref, kp_ref, vc_ref, vp_ref, sink_ref, o_ref):
    n = pl.program_id(1)
    blocks = q_ref.shape[0] // WINDOW
    rows = GROUP * WINDOW
    i = lax.broadcasted_iota(jnp.int32, (rows, 2 * WINDOW), 0) % WINDOW
    j = lax.broadcasted_iota(jnp.int32, (rows, 2 * WINDOW), 1)
    diff = WINDOW + i - j
    band = (diff >= 0) & (diff < WINDOW)
    q = q_ref[...].astype(BF16)
    kall = jnp.concatenate([kp_ref[...], kc_ref[...]], axis=0).astype(BF16)
    vall = jnp.concatenate([vp_ref[...], vc_ref[...]], axis=0).astype(BF16)
    units = [(qb, kh) for qb in range(blocks) for kh in range(KV_HEADS)]
    ksl = [slice(kh * HEAD_DIM, (kh + 1) * HEAD_DIM) for kh in range(KV_HEADS)]
    mask = [band & ((n > 0) | (j >= WINDOW)) if qb == 0 else band for qb, _ in units]
    kk = [kall[qb * WINDOW:(qb + 2) * WINDOW, ksl[kh]] for qb, kh in units]
    vv = [vall[qb * WINDOW:(qb + 2) * WINDOW, ksl[kh]] for qb, kh in units]
    qh = [jnp.concatenate([q[qb * WINDOW:(qb + 1) * WINDOW, (kh * GROUP + g) * HEAD_DIM:(kh * GROUP + g + 1) * HEAD_DIM]
                           for g in range(GROUP)], axis=0) for qb, kh in units]
    us = range(len(units))
    sink = [sink_ref[kh] for _, kh in units]
    s = [lax.dot_general(qh[u], kk[u], (((1,), (1,)), ((), ())), preferred_element_type=F32) for u in us]
    s = [jnp.where(mask[u], s[u] * (HEAD_DIM ** -0.5), -jnp.inf) for u in us]
    m = [jnp.maximum(jnp.max(s[u], axis=1, keepdims=True), sink[u]) for u in us]
    e = [jnp.exp(s[u] - m[u]) for u in us]
    denom = [jnp.sum(e[u], axis=1, keepdims=True) + jnp.exp(sink[u] - m[u]) for u in us]
    o = [_bdot((e[u] / denom[u]).astype(BF16), vv[u]) for u in us]
    for qb in range(blocks):
        outs = [o[qb * KV_HEADS + kh][g * WINDOW:(g + 1) * WINDOW, :] for kh in range(KV_HEADS) for g in range(GROUP)]
        o_ref[qb * WINDOW:(qb + 1) * WINDOW, :] = jnp.concatenate(outs, axis=1).astype(o_ref.dtype)


def _swa_prompt(qk_rot, proj, lw, B, T):
    nb = T // WINDOW
    rows = GROUP * WINDOW
    kcol = Q_W // KV_W
    vcol = V_OFF // KV_W
    l = lw["l"]
    blocks = min(SWA_BLOCKS, nb)
    ng = nb // blocks
    span = blocks * WINDOW
    cur = lambda col: (lambda b, n: (b * ng + n, col))
    prev = lambda col: (lambda b, n: (b * nb + jnp.maximum(n * blocks - 1, 0), col))
    return pl.pallas_call(
        _swa_prompt_kernel,
        grid=(B, ng),
        in_specs=[pl.BlockSpec((span, Q_W), cur(0)),
                  pl.BlockSpec((span, KV_W), cur(kcol)),
                  pl.BlockSpec((WINDOW, KV_W), prev(kcol)),
                  pl.BlockSpec((span, KV_W), cur(vcol)),
                  pl.BlockSpec((WINDOW, KV_W), prev(vcol)),
                  pl.BlockSpec((None, KV_HEADS, rows, 1), lambda b, n: (l, 0, 0, 0))],
        out_specs=pl.BlockSpec((span, Q_W), cur(0)),
        out_shape=jax.ShapeDtypeStruct((B * T, Q_W), BF16),
        compiler_params=_cparams(2),
        name="swa_prompt",
    )(qk_rot, qk_rot, qk_rot, proj, proj, lw["sink_p"])


def _swa_sample_kernel(q_ref, kn_ref, vn_ref, ck_ref, cv_ref, sink_ref, *rest, steps, n_prev):
    if n_prev:
        pk_ref, pv_ref, o_ref, nk_ref, nv_ref = rest
        for li in range(n_prev):
            nk_ref[li] = pk_ref[li]
            nv_ref[li] = pv_ref[li]
    else:
        o_ref, nk_ref, nv_ref = rest
    first_new = SUBLANE - steps
    new_rows = lax.broadcasted_iota(jnp.int32, (SUBLANE, KV_W), 0) >= first_new
    for b in range(q_ref.shape[0]):
        for c_ref, n_ref, dst in ((ck_ref, kn_ref, nk_ref), (cv_ref, vn_ref, nv_ref)):
            rolled = pltpu.roll(c_ref[b], WINDOW - steps, axis=0)
            tail = jnp.where(new_rows, n_ref[b], rolled[WINDOW - SUBLANE:, :])
            dst[n_prev, b] = jnp.concatenate([rolled[:WINDOW - SUBLANE, :], tail], axis=0)
    rows = GROUP * steps
    t_row = lax.broadcasted_iota(jnp.int32, (2 * rows, WINDOW), 0) % steps
    c_col = lax.broadcasted_iota(jnp.int32, (2 * rows, WINDOW), 1)
    cache_mask = c_col > t_row
    t_col = lax.broadcasted_iota(jnp.int32, (2 * rows, 1), 0) % steps
    scale = HEAD_DIM ** -0.5
    zq = jnp.zeros((rows, HEAD_DIM), F32)
    units = [(b, p) for b in range(q_ref.shape[0]) for p in range(KV_HEADS // 2)]
    us = range(len(units))
    lanes = [slice(p * LANE, (p + 1) * LANE) for _, p in units]
    q2 = [jnp.concatenate([jnp.concatenate([q_ref[b, 2 * p], zq], axis=1),
                           jnp.concatenate([zq, q_ref[b, 2 * p + 1]], axis=1)], axis=0).astype(BF16) for b, p in units]
    kc = [ck_ref[b, :, lanes[u]].astype(BF16) for u, (b, p) in enumerate(units)]
    vc = [cv_ref[b, :, lanes[u]].astype(BF16) for u, (b, p) in enumerate(units)]
    kn = [kn_ref[b, first_new:, lanes[u]].astype(BF16).astype(F32) for u, (b, p) in enumerate(units)]
    vn = [vn_ref[b, first_new:, lanes[u]].astype(BF16).astype(F32) for u, (b, p) in enumerate(units)]
    sink = [jnp.concatenate([sink_ref[2 * p], sink_ref[2 * p + 1]], axis=0) for _, p in units]
    sc = [lax.dot_general(q2[u], kc[u], (((1,), (1,)), ((), ())), preferred_element_type=F32) for u in us]
    sc = [jnp.where(cache_mask, sc[u] * scale, -jnp.inf) for u in us]
    q2f = [q2[u].astype(F32) for u in us]
    sn = [[jnp.where(t_col >= t, jnp.sum(q2f[u] * kn[u][t:t + 1, :], axis=1, keepdims=True) * scale, -jnp.inf)
           for t in range(steps)] for u in us]
    m = [jnp.maximum(jnp.max(sc[u], axis=1, keepdims=True), sink[u]) for u in us]
    for t in range(steps):
        m = [jnp.maximum(m[u], sn[u][t]) for u in us]
    ec = [jnp.exp(sc[u] - m[u]) for u in us]
    en = [[jnp.exp(sn[u][t] - m[u]) for t in range(steps)] for u in us]
    denom = [jnp.sum(ec[u], axis=1, keepdims=True) + jnp.exp(sink[u] - m[u]) for u in us]
    for t in range(steps):
        denom = [denom[u] + en[u][t] for u in us]
    o = [_bdot((ec[u] / denom[u]).astype(BF16), vc[u]) for u in us]
    for t in range(steps):
        o = [o[u] + (en[u][t] / denom[u]).astype(BF16).astype(F32) * vn[u][t:t + 1, :] for u in us]
    for u, (b, p) in enumerate(units):
        o_ref[b, 2 * p] = o[u][0:rows, 0:HEAD_DIM]
        o_ref[b, 2 * p + 1] = o[u][rows:2 * rows, HEAD_DIM:LANE]


def _swa_sample(qk_rot, proj, cache_k, cache_v, prev_k, prev_v, lw, B, T):
    l = lw["l"]
    rows = GROUP * T
    qg = qk_rot[:, :Q_W].reshape(T, B, KV_HEADS, GROUP, HEAD_DIM).transpose(1, 2, 3, 0, 4)
    qg = qg.reshape(B, KV_HEADS, rows, HEAD_DIM)
    pad = ((0, 0), (SUBLANE - T, 0), (0, 0))
    kn = jnp.pad(qk_rot[:, Q_W:].reshape(T, B, KV_W).transpose(1, 0, 2), pad)
    vn = jnp.pad(proj[:, V_OFF:V_OFF + KV_W].reshape(T, B, KV_W).transpose(1, 0, 2), pad)
    nb = SUBLANE if B % SUBLANE == 0 else 1
    new_spec = pl.BlockSpec((nb, SUBLANE, KV_W), lambda b: (b, 0, 0))
    cache_spec = pl.BlockSpec((None, nb, WINDOW, KV_W), lambda b: (l, b, 0, 0))
    in_specs = [pl.BlockSpec((nb, KV_HEADS, rows, HEAD_DIM), lambda b: (b, 0, 0, 0)),
                new_spec, new_spec, cache_spec, cache_spec,
                pl.BlockSpec((None, KV_HEADS, rows, 1), lambda b: (l, 0, 0, 0))]
    args = [qg, kn, vn, cache_k, cache_v, lw["sink_s"]]
    if l:
        in_specs += [pl.BlockSpec((l, nb, WINDOW, KV_W), lambda b: (0, b, 0, 0))] * 2
        args += [prev_k, prev_v]
    stacked = pl.BlockSpec((l + 1, nb, WINDOW, KV_W), lambda b: (0, b, 0, 0))
    stacked_shape = jax.ShapeDtypeStruct((l + 1, B, WINDOW, KV_W), F32)
    o, new_k, new_v = pl.pallas_call(
        functools.partial(_swa_sample_kernel, steps=T, n_prev=l),
        grid=(B // nb,),
        in_specs=in_specs,
        out_specs=[pl.BlockSpec((nb, KV_HEADS, rows, HEAD_DIM), lambda b: (b, 0, 0, 0)), stacked, stacked],
        out_shape=[jax.ShapeDtypeStruct((B, KV_HEADS, rows, HEAD_DIM), F32), stacked_shape, stacked_shape],
        compiler_params=_cparams(1),
        name="swa_sample",
    )(*args)
    o = o.reshape(B, KV_HEADS, GROUP, T, HEAD_DIM).transpose(3, 0, 1, 2, 4).reshape(T * B, Q_W)
    return o.astype(BF16), new_k, new_v


def _rope_tables(pos):
    inv = 1.0 / (ROPE_THETA ** (jnp.arange(0, ROPE_DIM, 2, dtype=F32) / ROPE_DIM))
    ang = pos.astype(F32)[:, None] * inv[None, :]
    cos, sin = jnp.cos(ang), jnp.sin(ang)
    n = pos.shape[0]
    ones = jnp.ones((n, HEAD_DIM - ROPE_DIM), F32)
    zeros = jnp.zeros((n, HEAD_DIM - ROPE_DIM), F32)
    zh = jnp.zeros((n, ROPE_HALF), F32)
    c = jnp.concatenate([cos, cos, ones], axis=1)
    s1 = jnp.concatenate([-sin, zh, zeros], axis=1)
    s2 = jnp.concatenate([zh, sin, zeros], axis=1)
    return tuple(jnp.tile(a, (1, SEG // HEAD_DIM)) for a in (c, s1, s2))


def _stacked_params(t_sample, norm_g, ffn_w_gate, ffn_w_up, ffn_w_down, w_in, rwkv_mu, rwkv_w0, rwkv_w2,
                    rwkv_a0, rwkv_a2, rwkv_g2, rwkv_k_k, rwkv_k_a, rwkv_r_k, rwkv_ln_w, rwkv_ln_b, attn_sinks,
                    w_branch_a, w_branch_b, w_merge_gate, w_out):
    L = w_in.shape[0]
    w_proj = jnp.concatenate([w_in[:, :, :A_PROJ], jnp.zeros((L, D_MODEL, A_PAD - A_PROJ), w_in.dtype),
                              w_in[:, :, A_PROJ:]], axis=2)
    lora_rows = lambda w, off: jnp.pad(w, ((0, 0), (off, LORA_PAD - off - w.shape[1]), (0, 0))).astype(BF16)
    seg = jnp.arange(SEG) // A_HEAD_DIM
    row = lambda a: a.reshape(L, 1, -1)
    sink_rows = lambda n: jnp.repeat(attn_sinks.reshape(L, KV_HEADS, GROUP), n, axis=2).reshape(L, KV_HEADS, GROUP * n, 1)
    return {
        "norm_g": norm_g, "wg": ffn_w_gate, "wu": ffn_w_up, "wd": ffn_w_down, "w_proj": w_proj,
        "mu": row(jnp.pad(rwkv_mu, ((0, 0), (0, A_PAD - A_PROJ)))),
        "w0": row(rwkv_w0), "a0": row(rwkv_a0), "k_k": row(rwkv_k_k), "k_a": row(rwkv_k_a),
        "r_k": row(rwkv_r_k), "ln_w": row(rwkv_ln_w), "ln_b": row(rwkv_ln_b),
        "w2": lora_rows(rwkv_w2, 0), "a2": lora_rows(rwkv_a2, DECAY_LORA),
        "g2": lora_rows(rwkv_g2, DECAY_LORA + ICLR_LORA),
        "bsel": (seg[:, None] == seg[None, :]).astype(BF16),
        "sink_p": sink_rows(WINDOW), "sink_s": sink_rows(t_sample),
        "w_a": w_branch_a, "w_b": w_branch_b, "w_gate": w_merge_gate, "w_out": w_out,
    }


def _decoder_layer(x, mod, lw, B, T, shift_prev, tables, sample=None):
    M = B * T
    l = lw["l"]
    sp = jnp.pad(shift_prev, ((0, 0), (0, A_PAD - A_PROJ)))
    if sample is not None:
        tm = M
        mod = mod.reshape(1, B, N_MOD * D_MODEL)
        gmap = lambda tile: (lambda i: 0)
        sp = sp.reshape(1, B, A_PAD)
    else:
        tm = min(T, 1024)
        mod = mod.reshape(B, 1, N_MOD * D_MODEL)
        gmap = lambda tile: (lambda i: i // (T // tile))
        sp = sp.reshape(B, 1, A_PAD)
    tm_k = min(tm, 512)

    def ffn(x, n, which):
        h = _ada_norm(x, lw["norm_g"], l, n, mod, gmap(tm), tm)
        act = _ffn_gate_up(h, lw["wg"], lw["wu"], (l, which), tm)
        return _mm_residual(act, lw["wd"], (l, which), x, mod, 3 * n + 2, gmap(tm_k), 0.5, tm_k)

    x = ffn(x, 0, 0)

    h = _ada_norm(x, lw["norm_g"], l, 1, mod, gmap(tm), tm)
    proj = _mm_plain(h, lw["w_proj"], (l,), tm, tn=TN_WIDE, name="in_proj")
    gates = _mm_plain(h, lw["w_gate"], (l,), tm, tn=TN_WIDE, sigmoid=True, name="merge_gates")

    qk_rot = _rope(proj, tables, min(M, ROPE_ROWS))
    if sample is not None:
        state_wkv, cache_k, cache_v, prev_wkv, prev_k, prev_v = sample
        r, w, k2, v, kk, kka, g, bonus = _rwkv_prep(proj, sp, lw, B)
        o, new_wkv = _rwkv_lane_scan((r, w, k2, v, kk, kka), state_wkv, prev_wkv, l, B, T)
        ya = _rwkv_post(o, bonus, g, lw, B)
        new_shift = proj[(T - 1) * B:, :A_PROJ]
        yb, new_k, new_v = _swa_sample(qk_rot, proj, cache_k, cache_v, prev_k, prev_v, lw, B, T)
    else:
        zero_wkv = jnp.zeros((B, A_HEADS, A_HEAD_DIM, A_HEAD_DIM), F32)
        ya, new_wkv = _rwkv_chunked(proj, sp, lw, zero_wkv, B, T)
        new_shift = proj.reshape(B, T, PROJ_W)[:, -1, :A_PROJ]
        yb = _swa_prompt(qk_rot, proj, lw, B, T)
        new_k = qk_rot[:, Q_W:].reshape(B, T, KV_HEADS, HEAD_DIM)[:, -WINDOW:]
        new_v = proj[:, V_OFF:V_OFF + KV_W].reshape(B, T, KV_HEADS, HEAD_DIM)[:, -WINDOW:]

    merged = _mm_merge(ya, yb, lw["w_a"], lw["w_b"], (l,), gates, tm, tn=TN_WIDE)
    x = _mm_residual(merged, lw["w_out"], (l,), x, mod, 5, gmap(tm), 1.0, tm, tn=TN_WIDE)

    x = ffn(x, 2, 1)
    return x, new_shift, new_wkv, new_k, new_v


def kernel(x_prompt, x_sample, c_prompt, c_sample, state_shift, state_wkv, cache_k, cache_v, norm_g, w_mod, b_mod,
           ffn_w_gate, ffn_w_up, ffn_w_down, w_in, rwkv_mu, rwkv_w0, rwkv_w2, rwkv_a0, rwkv_a2, rwkv_g2, rwkv_k_k,
           rwkv_k_a, rwkv_r_k, rwkv_ln_w, rwkv_ln_b, attn_sinks, w_branch_a, w_branch_b, w_merge_gate, w_out,
           final_norm):
    Bp, Tp, _ = x_prompt.shape
    Bs, Ts, _ = x_sample.shape
    L = w_in.shape[0]
    tab_p = _rope_tables(jnp.arange(Tp))
    tab_s = _rope_tables(jnp.repeat(PAST_LEN + jnp.arange(Ts), Bs))
    zero_shift = jnp.zeros((Bp, A_PROJ), F32)
    xp = x_prompt.reshape(Bp * Tp, D_MODEL)
    xs = x_sample.transpose(1, 0, 2).reshape(Ts * Bs, D_MODEL)
    c_all = jnp.concatenate([c_prompt, c_sample], axis=0)
    pad_rows = (-c_all.shape[0]) % 16
    c_all = jnp.pad(c_all, ((0, pad_rows), (0, 0)))
    pw = _stacked_params(Ts, norm_g, ffn_w_gate, ffn_w_up, ffn_w_down, w_in, rwkv_mu, rwkv_w0, rwkv_w2, rwkv_a0,
                         rwkv_a2, rwkv_g2, rwkv_k_k, rwkv_k_a, rwkv_r_k, rwkv_ln_w, rwkv_ln_b, attn_sinks,
                         w_branch_a, w_branch_b, w_merge_gate, w_out)
    state_s = state_wkv.reshape(L, Bs, A_HEADS * A_HEAD_DIM * A_HEAD_DIM)
    ck = cache_k.reshape(L, Bs, WINDOW, KV_W)
    cv = cache_v.reshape(L, Bs, WINDOW, KV_W)
    outs_p, shifts_s = [], []
    wkv_s = k_s = v_s = None
    for l in range(L):
        lw = dict(pw, l=l)
        mod = _modulation(c_all, w_mod, b_mod, l)
        xp, *st_p = _decoder_layer(xp, mod[:Bp], lw, Bp, Tp, zero_shift, tab_p)
        xs, shift_s, wkv_s, k_s, v_s = _decoder_layer(xs, mod[Bp:Bp + Bs], lw, Bs, Ts, state_shift[l], tab_s,
                                                      sample=(state_s, ck, cv, wkv_s, k_s, v_s))
        outs_p.append(st_p)
        shifts_s.append(shift_s)
    y_prompt = _final_norm(xp, final_norm, 1024).reshape(Bp, Tp, D_MODEL)
    y_sample = _final_norm(xs, final_norm, Bs * Ts).reshape(Ts, Bs, D_MODEL).transpose(1, 0, 2)
    stack = lambda n: jnp.stack([o[n] for o in outs_p])
    return (y_prompt, y_sample, stack(0), stack(1), stack(2), stack(3),
            jnp.stack(shifts_s), wkv_s.reshape(state_wkv.shape), k_s.reshape(cache_k.shape),
            v_s.reshape(cache_v.shape))
```

```python
import functools

import jax
import jax.numpy as jnp
from jax import lax
from jax.experimental import pallas as pl
from jax.experimental.pallas import tpu as pltpu

D_MODEL = 2048
DEPTH = 2
A_HEADS = 16
A_HEAD_DIM = 64
A_WIDTH = A_HEADS * A_HEAD_DIM
DECAY_LORA = 64
ICLR_LORA = 64
GATE_LORA = 160
LORA_W = DECAY_LORA + ICLR_LORA + GATE_LORA
A_PROJ = 3 * A_WIDTH + LORA_W
GN_EPS = 64e-5
Q_HEADS = 16
KV_HEADS = 4
GROUP = Q_HEADS // KV_HEADS
HEAD_DIM = 64
Q_W = Q_HEADS * HEAD_DIM
KV_W = KV_HEADS * HEAD_DIM
WINDOW = 128
ROPE_DIM = HEAD_DIM // 4
ROPE_HALF = ROPE_DIM // 2
ROPE_THETA = 500000.0
D_FF = 5632
N_MOD = 9
RMS_EPS = 1e-6

PAST_LEN = 8192
LANE = 128
SUBLANE = 8
ROPE_ROWS = 2048
SWA_BLOCKS = 4
CHUNK = 64
SOLVE_BASE = 8
SEG = 256
A_PAD = 3584
LORA_PAD = A_PAD - 3 * A_WIDTH
LORA_DI = 128
LORA_G = 256
assert DECAY_LORA + ICLR_LORA == LORA_DI and GATE_LORA <= LORA_G and LORA_DI + LORA_G <= LORA_PAD
QK_W = Q_W + KV_W
TN_A = 896
TN_B = 768
VMEM_LIMIT = 56 * 1024 * 1024
TN_WIDE = 1024

F32 = jnp.float32
BF16 = jnp.bfloat16


def _cparams(n_axes):
    return pltpu.CompilerParams(dimension_semantics=("arbitrary",) * n_axes,
                                vmem_limit_bytes=VMEM_LIMIT)


def _bdot(a, b):
    return jnp.dot(a, b, preferred_element_type=F32)


def _seg_sum(x, bsel):
    hi = x.astype(BF16)
    lo = (x - hi.astype(F32)).astype(BF16)
    seg = bsel.shape[0]
    parts = []
    for c in range(x.shape[1] // seg):
        sl = slice(c * seg, (c + 1) * seg)
        parts.append(_bdot(hi[:, sl], bsel) + _bdot(lo[:, sl], bsel))
    return parts[0] if len(parts) == 1 else jnp.concatenate(parts, axis=1)


def _mod_kernel(c_ref, w_ref, b_ref, o_ref):
    c = c_ref[...]
    a = (c * jax.nn.sigmoid(c)).astype(BF16)
    o_ref[...] = _bdot(a, w_ref[...].astype(BF16)) + b_ref[...]


def _modulation(c, w_mod, b_mod, l):
    R = c.shape[0]
    N = w_mod.shape[-1]
    tn = 1024
    return pl.pallas_call(
        _mod_kernel,
        grid=(N // tn,),
        in_specs=[pl.BlockSpec((R, D_MODEL), lambda j: (0, 0)),
                  pl.BlockSpec((None, D_MODEL, tn), lambda j: (l, 0, j)),
                  pl.BlockSpec((None, 1, tn), lambda j: (l, 0, j))],
        out_specs=pl.BlockSpec((R, tn), lambda j: (0, j)),
        out_shape=jax.ShapeDtypeStruct((R, N), F32),
        compiler_params=_cparams(1),
        name="modulation",
    )(c, w_mod, b_mod.reshape(b_mod.shape[0], 1, N))


def _mod_rows(ref, tm):
    val = ref[...]
    reps = tm // val.shape[0]
    return val if val.shape[0] == 1 or reps == 1 else jnp.concatenate([val] * reps, axis=0)


def _ada_norm_kernel(x_ref, g_ref, sh_ref, sc_ref, o_ref):
    x = x_ref[...]
    tm = x.shape[0]
    y = x * lax.rsqrt(jnp.mean(x * x, axis=-1, keepdims=True) + RMS_EPS)
    y = y * g_ref[...]
    o_ref[...] = (y * (1.0 + _mod_rows(sc_ref, tm)) + _mod_rows(sh_ref, tm)).astype(o_ref.dtype)


def _ada_norm(x, norm_g, l, n, mod, gmap, tm):
    M = x.shape[0]
    R = mod.shape[1]
    mspec = lambda col: pl.BlockSpec((None, R, D_MODEL), lambda i: (gmap(i), 0, col))
    return pl.pallas_call(
        _ada_norm_kernel,
        grid=(M // tm,),
        in_specs=[pl.BlockSpec((tm, D_MODEL), lambda i: (i, 0)),
                  pl.BlockSpec((None, None, 1, D_MODEL), lambda i: (l, n, 0, 0)),
                  mspec(3 * n), mspec(3 * n + 1)],
        out_specs=pl.BlockSpec((tm, D_MODEL), lambda i: (i, 0)),
        out_shape=jax.ShapeDtypeStruct((M, D_MODEL), BF16),
        compiler_params=_cparams(1),
        name="ada_norm",
    )(x, norm_g.reshape(norm_g.shape[0], norm_g.shape[1], 1, D_MODEL), mod, mod)


def _final_norm_kernel(x_ref, g_ref, o_ref):
    x = x_ref[...]
    y = x * lax.rsqrt(jnp.mean(x * x, axis=-1, keepdims=True) + RMS_EPS)
    o_ref[...] = y * g_ref[...]


def _final_norm(x, gain, tm):
    M = x.shape[0]
    return pl.pallas_call(
        _final_norm_kernel,
        grid=(M // tm,),
        in_specs=[pl.BlockSpec((tm, D_MODEL), lambda i: (i, 0)),
                  pl.BlockSpec((1, D_MODEL), lambda i: (0, 0))],
        out_specs=pl.BlockSpec((tm, D_MODEL), lambda i: (i, 0)),
        out_shape=jax.ShapeDtypeStruct((M, D_MODEL), F32),
        compiler_params=_cparams(1),
        name="final_norm",
    )(x, gain.reshape(1, D_MODEL))


def _wspec(w, lead, tn):
    return pl.BlockSpec((None,) * len(lead) + (w.shape[-2], tn), lambda j, i: lead + (0, j))


def _wdot(a, w_ref):
    return _bdot(a, w_ref[...].astype(BF16))


def _gu_kernel(h_ref, wg_ref, wu_ref, o_ref):
    h = h_ref[...]
    g = _wdot(h, wg_ref)
    u = _wdot(h, wu_ref)
    o_ref[...] = ((g * jax.nn.sigmoid(g)) * u).astype(o_ref.dtype)


def _ffn_gate_up(h, wg, wu, lead, tm, tn=512):
    M, K = h.shape
    N = wg.shape[-1]
    return pl.pallas_call(
        _gu_kernel,
        grid=(N // tn, M // tm),
        in_specs=[pl.BlockSpec((tm, K), lambda j, i: (i, 0)), _wspec(wg, lead, tn), _wspec(wu, lead, tn)],
        out_specs=pl.BlockSpec((tm, tn), lambda j, i: (i, j)),
        out_shape=jax.ShapeDtypeStruct((M, N), BF16),
        compiler_params=_cparams(2),
        name="ffn_gate_up",
    )(h, wg, wu)


def _resid_kernel(a_ref, w_ref, x_ref, gate_ref, o_ref, *, coef):
    y = _wdot(a_ref[...], w_ref)
    o_ref[...] = x_ref[...] + (coef * _mod_rows(gate_ref, y.shape[0])) * y


def _mm_residual(a, w, lead, x, mod, n_gate, gmap, coef, tm, tn=512):
    M, K = a.shape
    N = w.shape[-1]
    R = mod.shape[1]
    nj = N // tn
    return pl.pallas_call(
        functools.partial(_resid_kernel, coef=coef),
        grid=(nj, M // tm),
        in_specs=[pl.BlockSpec((tm, K), lambda j, i: (i, 0)),
                  _wspec(w, lead, tn),
                  pl.BlockSpec((tm, tn), lambda j, i: (i, j)),
                  pl.BlockSpec((None, R, tn), lambda j, i: (gmap(i), 0, n_gate * nj + j))],
        out_specs=pl.BlockSpec((tm, tn), lambda j, i: (i, j)),
        out_shape=jax.ShapeDtypeStruct((M, N), F32),
        compiler_params=_cparams(2),
        name="mm_residual",
    )(a, w, x, mod)


def _plain_kernel(a_ref, w_ref, o_ref, *, sigmoid):
    y = _wdot(a_ref[...], w_ref)
    if sigmoid:
        y = jax.nn.sigmoid(y)
    o_ref[...] = y.astype(o_ref.dtype)


def _mm_plain(a, w, lead, tm, tn=512, sigmoid=False, n_cols=None, name="mm_plain"):
    M, K = a.shape
    N = n_cols or w.shape[-1]
    return pl.pallas_call(
        functools.partial(_plain_kernel, sigmoid=sigmoid),
        grid=(N // tn, M // tm),
        in_specs=[pl.BlockSpec((tm, K), lambda j, i: (i, 0)), _wspec(w, lead, tn)],
        out_specs=pl.BlockSpec((tm, tn), lambda j, i: (i, j)),
        out_shape=jax.ShapeDtypeStruct((M, N), F32),
        compiler_params=_cparams(2),
        name=name,
    )(a, w)


def _merge_kernel(ya_ref, yb_ref, wa_ref, wb_ref, ga_ref, gb_ref, o_ref):
    pa = _wdot(ya_ref[...], wa_ref)
    pb = _wdot(yb_ref[...], wb_ref)
    o_ref[...] = (ga_ref[...] * pa + gb_ref[...] * pb).astype(o_ref.dtype)


def _mm_merge(ya, yb, wa, wb, lead, gates, tm, tn=512):
    M, K = ya.shape
    N = wa.shape[-1]
    nj = N // tn
    return pl.pallas_call(
        _merge_kernel,
        grid=(nj, M // tm),
        in_specs=[pl.BlockSpec((tm, K), lambda j, i: (i, 0)),
                  pl.BlockSpec((tm, K), lambda j, i: (i, 0)),
                  _wspec(wa, lead, tn), _wspec(wb, lead, tn),
                  pl.BlockSpec((tm, tn), lambda j, i: (i, j)),
                  pl.BlockSpec((tm, tn), lambda j, i: (i, j + nj))],
        out_specs=pl.BlockSpec((tm, tn), lambda j, i: (i, j)),
        out_shape=jax.ShapeDtypeStruct((M, N), BF16),
        compiler_params=_cparams(2),
        name="mm_merge",
    )(ya, yb, wa, wb, gates, gates)


def _rwkv_operands(pa, prev, mu, w0, a0, k_k, k_a, r_k, w2_ref, a2_ref, g2_ref, head_sums):
    xm = pa + (prev - pa) * mu
    r = xm[:, 0:A_WIDTH]
    k = xm[:, A_WIDTH:2 * A_WIDTH]
    v = xm[:, 2 * A_WIDTH:3 * A_WIDTH]
    lo_di = xm[:, 3 * A_WIDTH:3 * A_WIDTH + LORA_DI]
    lo_g = xm[:, 3 * A_WIDTH + LORA_DI:3 * A_WIDTH + LORA_DI + LORA_G]
    m_w = _bdot(jnp.tanh(lo_di).astype(BF16), w2_ref[0:LORA_DI, :])
    m_a = _bdot(lo_di.astype(BF16), a2_ref[0:LORA_DI, :])
    g = _bdot(jax.nn.sigmoid(lo_g).astype(BF16), g2_ref[LORA_DI:LORA_DI + LORA_G, :])
    z = -(w0 + m_w)
    softplus = jnp.maximum(z, 0.0) + jnp.log1p(jnp.exp(-jnp.abs(z)))
    w_log = -softplus - 0.5
    a = jax.nn.sigmoid(a0 + m_a)
    kkf = k * k_k
    kk = kkf / jnp.maximum(jnp.sqrt(head_sums(kkf * kkf)), 1e-12)
    k2 = k * (1.0 + (a - 1.0) * k_a)
    bonus = head_sums(r * k2 * r_k) * v
    return r, -jnp.exp(w_log), k2, v, kk, kk * a, g, bonus


def _rwkv_prep_kernel(p_ref, sp_ref, mu_ref, w0_ref, a0_ref, kk_ref, ka_ref, rk_ref,
                      w2_ref, a2_ref, g2_ref, bsel_ref,
                      r_out, d_out, k_out, v_out, kk_out, kka_out, g_out, bonus_out, carry_ref):
    t = pl.program_id(0)
    pa = p_ref[...]

    @pl.when(t == 0)
    def _():
        carry_ref[...] = sp_ref[...]

    prev = carry_ref[...]
    carry_ref[...] = pa
    bsel = bsel_ref[...]
    r, w, k2, v, kk, kka, g, bonus = _rwkv_operands(
        pa, prev, mu_ref[...], w0_ref[...], a0_ref[...], kk_ref[...], ka_ref[...], rk_ref[...],
        w2_ref, a2_ref, g2_ref, lambda x: _seg_sum(x, bsel))
    for ref, val in ((r_out, r), (d_out, w), (k_out, k2), (v_out, v), (kk_out, kk), (kka_out, kka)):
        ref[...] = val.T
    g_out[...] = g
    bonus_out[...] = bonus


def _rwkv_prep(proj, sp, lw, n_seq):
    M = proj.shape[0]
    l = lw["l"]
    row = lambda n: pl.BlockSpec((None, 1, n), lambda i: (l, 0, 0))
    full = lambda a, b: pl.BlockSpec((None, a, b), lambda i: (l, 0, 0))
    tok_spec = pl.BlockSpec((n_seq, A_WIDTH), lambda i: (i, 0))
    tok_shape = jax.ShapeDtypeStruct((M, A_WIDTH), F32)
    op_spec = pl.BlockSpec((A_WIDTH, n_seq), lambda i: (0, i))
    op_shape = jax.ShapeDtypeStruct((A_WIDTH, M), F32)
    return pl.pallas_call(
        _rwkv_prep_kernel,
        grid=(M // n_seq,),
        in_specs=[pl.BlockSpec((n_seq, A_PAD), lambda i: (i, 0)),
                  pl.BlockSpec((None, n_seq, A_PAD), lambda i: (0, 0, 0)),
                  row(A_PAD), row(A_WIDTH), row(A_WIDTH), row(A_WIDTH), row(A_WIDTH), row(A_WIDTH),
                  full(LORA_PAD, A_WIDTH), full(LORA_PAD, A_WIDTH), full(LORA_PAD, A_WIDTH),
                  pl.BlockSpec((SEG, SEG), lambda i: (0, 0))],
        out_specs=[op_spec] * 6 + [tok_spec] * 2,
        out_shape=[op_shape] * 6 + [tok_shape] * 2,
        scratch_shapes=[pltpu.VMEM((n_seq, A_PAD), F32)],
        compiler_params=_cparams(1),
        name="rwkv_prep",
    )(proj, sp, lw["mu"], lw["w0"], lw["a0"], lw["k_k"], lw["k_a"], lw["r_k"],
      lw["w2"], lw["a2"], lw["g2"], lw["bsel"])


def _rwkv_lane_scan_kernel(r_ref, w_ref, k_ref, v_ref, kk_ref, kka_ref, s0_ref, *rest, steps, n_prev):
    if n_prev:
        prev_ref, o_ref, s_out_ref, st_ref = rest
        for li in range(n_prev):
            s_out_ref[li] = prev_ref[li]
    else:
        o_ref, s_out_ref, st_ref = rest
    n = A_HEAD_DIM
    nb = s0_ref.shape[0]
    st_ref[...] = s0_ref[...].T
    tsl = [slice(t * nb, (t + 1) * nb) for t in range(steps)]

    def group(g, carry):
        rows8 = pl.ds(pl.multiple_of(g * SUBLANE, SUBLANE), SUBLANE)
        v8 = [v_ref[rows8, tsl[t]] for t in range(steps)]
        o8 = [[] for _ in range(steps)]
        for u in range(SUBLANE):
            srows = pl.ds(pl.multiple_of((g * SUBLANE + u) * n, n), n)
            s = st_ref[srows, :]
            for t in range(steps):
                sk = jnp.sum(s * kk_ref[:, tsl[t]], axis=0, keepdims=True)
                s = s * jnp.exp(w_ref[:, tsl[t]]) - sk * kka_ref[:, tsl[t]] + v8[t][u:u + 1, :] * k_ref[:, tsl[t]]
                o8[t].append(jnp.sum(s * r_ref[:, tsl[t]], axis=0, keepdims=True))
            st_ref[srows, :] = s
        for t in range(steps):
            o_ref[rows8, tsl[t]] = jnp.concatenate(o8[t], axis=0)
        return carry

    lax.fori_loop(0, n // SUBLANE, group, 0)
    s_out_ref[n_prev] = st_ref[...].T


def _rwkv_lane_scan(ops_t, state, prev, l, B, T):
    n2 = A_HEAD_DIM * A_HEAD_DIM
    ospec = pl.BlockSpec((A_HEAD_DIM, T * B), lambda h: (h, 0))
    in_specs = [ospec] * 6 + [pl.BlockSpec((None, B, n2), lambda h: (l, 0, h))]
    args = list(ops_t) + [state]
    if l:
        in_specs.append(pl.BlockSpec((l, B, n2), lambda h: (0, 0, h)))
        args.append(prev)
    return pl.pallas_call(
        functools.partial(_rwkv_lane_scan_kernel, steps=T, n_prev=l),
        grid=(A_HEADS,),
        in_specs=in_specs,
        out_specs=[ospec, pl.BlockSpec((l + 1, B, n2), lambda h: (0, 0, h))],
        out_shape=[jax.ShapeDtypeStruct((A_WIDTH, T * B), F32),
                   jax.ShapeDtypeStruct((l + 1, B, A_HEADS * n2), F32)],
        scratch_shapes=[pltpu.VMEM((n2, B), F32)],
        compiler_params=_cparams(1),
        name="rwkv_lane_scan",
    )(*args)


def _rwkv_chunk_kernel(p_ref, sp_ref, mu_ref, w0_ref, a0_ref, kk_ref, ka_ref, rk_ref, w2_ref, a2_ref, g2_ref,
                       lnw_ref, lnb_ref, s0_ref, y_ref, s_out_ref, s_ref, carry_ref):
    c = pl.program_id(1)
    L = CHUNK
    P = 2 * L
    n_pairs = A_HEADS // 2
    left = lax.broadcasted_iota(jnp.int32, (L, LANE), 1) < A_HEAD_DIM
    row = lax.broadcasted_iota(jnp.int32, (P, P), 0)
    col = lax.broadcasted_iota(jnp.int32, (P, P), 1)
    strict = row > col
    lower = row >= col
    eye = (row == col).astype(F32)
    same = lambda n: (row // n) == (col // n)
    base = same(SOLVE_BASE)
    levels = []
    n = SOLVE_BASE
    while n < L:
        levels.append(same(2 * n) & jnp.logical_not(same(n)))
        n *= 2
    bf = lambda x: x.astype(BF16)
    nt = lambda a, b: lax.dot_general(a, b, (((1,), (1,)), ((), ())), preferred_element_type=F32)
    tn = lambda a, b: lax.dot_general(a, b, (((0,), (0,)), ((), ())), preferred_element_type=F32)

    pairs = range(n_pairs)
    lanes = [slice(p * LANE, (p + 1) * LANE) for p in pairs]

    @pl.when(c == 0)
    def _():
        z = jnp.zeros((A_HEAD_DIM, A_HEAD_DIM), F32)
        for p in range(n_pairs):
            s_ref[p] = jnp.concatenate([jnp.concatenate([s0_ref[2 * p], z], axis=1),
                                        jnp.concatenate([z, s0_ref[2 * p + 1]], axis=1)], axis=0)
        carry_ref[...] = sp_ref[...]

    def head_sums(x):
        sa = jnp.sum(jnp.where(left, x, 0.0), axis=1, keepdims=True)
        sb = jnp.sum(jnp.where(left, 0.0, x), axis=1, keepdims=True)
        return jnp.where(left, sa, sb)

    pa = p_ref[...]
    first_row = lax.broadcasted_iota(jnp.int32, pa.shape, 0) == 0
    prev = jnp.where(first_row, carry_ref[...], pltpu.roll(pa, 1, axis=0))
    carry_ref[...] = pa[L - 1:L, :]
    r, w, k, v, kk, kka, g, bonus = _rwkv_operands(
        pa, prev, mu_ref[...], w0_ref[...], a0_ref[...], kk_ref[...], ka_ref[...], rk_ref[...],
        w2_ref, a2_ref, g2_ref, lambda x: jnp.concatenate([head_sums(x[:, sl]) for sl in lanes], axis=1))

    tl = lax.broadcasted_iota(jnp.int32, (L, L), 0) >= lax.broadcasted_iota(jnp.int32, (L, L), 1)
    tri = tl.astype(BF16)
    w_hi = bf(w)
    w_r1 = w - w_hi.astype(F32)
    w_mid = bf(w_r1)
    w_lo = bf(w_r1 - w_mid.astype(F32))
    cum = _bdot(tri, w_hi) + _bdot(tri, w_mid) + _bdot(tri, w_lo)
    g_in = jnp.exp(cum)
    g_inv = jnp.exp(-cum)
    g_last = g_in[L - 1:L, :]
    at = -kk * jnp.exp(cum - w)
    bt = kka * g_inv
    kt = k * g_inv
    rt = r * g_in
    bl = bt * g_last
    kl = kt * g_last


    def stack(x, p):
        xp = x[:, lanes[p]]
        return bf(jnp.concatenate([jnp.where(left, xp, 0.0), jnp.where(left, 0.0, xp)], axis=0))

    a2 = [stack(at, p) for p in pairs]
    r2 = [stack(rt, p) for p in pairs]
    b2 = [stack(bt, p) for p in pairs]
    k2 = [stack(kt, p) for p in pairs]
    v2 = [stack(v, p) for p in pairs]
    sc = [nt(jnp.concatenate([a2[p], r2[p]], axis=0), jnp.concatenate([b2[p], k2[p]], axis=0)) for p in pairs]
    a_ab = [jnp.where(strict, sc[p][0:P, 0:P], 0.0) for p in pairs]
    a_ak = [bf(jnp.where(strict, sc[p][0:P, P:2 * P], 0.0)) for p in pairs]
    a_rb = [bf(jnp.where(lower, sc[p][P:2 * P, 0:P], 0.0)) for p in pairs]
    a_rk = [bf(jnp.where(lower, sc[p][P:2 * P, P:2 * P], 0.0)) for p in pairs]
    pw = [bf(jnp.where(base, a_ab[p], 0.0)) for p in pairs]
    t_inv = [eye + pw[p].astype(F32) for p in pairs]
    m = 2
    while m < SOLVE_BASE:
        pw = [bf(_bdot(pw[p], pw[p])) for p in pairs]
        t_inv = [t_inv[p] + _bdot(bf(t_inv[p]), pw[p]) for p in pairs]
        m *= 2
    for lv in levels:
        half = [bf(_bdot(bf(t_inv[p]), bf(jnp.where(lv, a_ab[p], 0.0)))) for p in pairs]
        t_inv = [t_inv[p] + _bdot(half[p], bf(t_inv[p])) for p in pairs]
    s = [s_ref[p] for p in pairs]
    s_b = [bf(s[p]) for p in pairs]
    wmat = [bf(nt(a2[p], s_b[p]) + _bdot(a_ak[p], v2[p])) for p in pairs]
    u = [bf(_bdot(bf(t_inv[p]), wmat[p])) for p in pairs]
    uv = [jnp.concatenate([u[p], v2[p]], axis=0) for p in pairs]
    o2 = [nt(r2[p], s_b[p]) + _bdot(jnp.concatenate([a_rb[p], a_rk[p]], axis=1), uv[p]) for p in pairs]
    for p in pairs:
        upd = tn(uv[p], jnp.concatenate([stack(bl, p), stack(kl, p)], axis=0))
        s_ref[p] = s[p] * g_last[:, lanes[p]] + upd
    inv_n = 1.0 / A_HEAD_DIM
    o = [o2[p][0:L, :] + o2[p][L:P, :] for p in pairs]
    cen = [o[p] - head_sums(o[p]) * inv_n for p in pairs]
    var = [head_sums(cen[p] * cen[p]) * inv_n for p in pairs]
    for p in pairs:
        y = cen[p] * lax.rsqrt(var[p] + GN_EPS) * lnw_ref[:, lanes[p]] + lnb_ref[:, lanes[p]]
        y_ref[:, lanes[p]] = ((y + bonus[:, lanes[p]]) * g[:, lanes[p]]).astype(BF16)

    @pl.when(c == pl.num_programs(1) - 1)
    def _():
        for p in range(n_pairs):
            s = s_ref[p]
            s_out_ref[2 * p] = s[0:A_HEAD_DIM, 0:A_HEAD_DIM]
            s_out_ref[2 * p + 1] = s[A_HEAD_DIM:LANE, A_HEAD_DIM:LANE]


def _rwkv_chunked(proj, sp, lw, s0, B, T):
    l = lw["l"]
    row = lambda n: pl.BlockSpec((None, 1, n), lambda b, c: (l, 0, 0))
    lora = pl.BlockSpec((None, LORA_PAD, A_WIDTH), lambda b, c: (l, 0, 0))
    sspec = pl.BlockSpec((None, A_HEADS, A_HEAD_DIM, A_HEAD_DIM), lambda b, c: (b, 0, 0, 0))
    y, s_t = pl.pallas_call(
        _rwkv_chunk_kernel,
        grid=(B, T // CHUNK),
        in_specs=[pl.BlockSpec((None, CHUNK, A_PAD), lambda b, c: (b, c, 0)),
                  pl.BlockSpec((None, 1, A_PAD), lambda b, c: (b, 0, 0)),
                  row(A_PAD), row(A_WIDTH), row(A_WIDTH), row(A_WIDTH), row(A_WIDTH), row(A_WIDTH),
                  lora, lora, lora, row(A_WIDTH), row(A_WIDTH), sspec],
        out_specs=[pl.BlockSpec((None, CHUNK, A_WIDTH), lambda b, c: (b, c, 0)), sspec],
        out_shape=[jax.ShapeDtypeStruct((B, T, A_WIDTH), BF16),
                   jax.ShapeDtypeStruct((B, A_HEADS, A_HEAD_DIM, A_HEAD_DIM), F32)],
        scratch_shapes=[pltpu.VMEM((A_HEADS // 2, LANE, LANE), F32), pltpu.VMEM((1, A_PAD), F32)],
        compiler_params=_cparams(2),
        name="rwkv_chunk",
    )(proj.reshape(B, T, A_PAD), sp, lw["mu"], lw["w0"], lw["a0"], lw["k_k"], lw["k_a"], lw["r_k"],
      lw["w2"], lw["a2"], lw["g2"], lw["ln_w"], lw["ln_b"], s0)
    return y.reshape(B * T, A_WIDTH), s_t


def _rwkv_out(o, bonus, g, ln_w, ln_b, bsel):
    inv_n = 1.0 / A_HEAD_DIM
    mean = _seg_sum(o, bsel) * inv_n
    c = o - mean
    var = _seg_sum(c * c, bsel) * inv_n
    y = c * lax.rsqrt(var + GN_EPS) * ln_w + ln_b
    return ((y + bonus) * g).astype(BF16)


def _rwkv_post_kernel(o_ref, bonus_ref, g_ref, lnw_ref, lnb_ref, bsel_ref, y_ref):
    y_ref[...] = _rwkv_out(o_ref[...].T, bonus_ref[...], g_ref[...], lnw_ref[...], lnb_ref[...], bsel_ref[...])


def _rwkv_post(o, bonus, g, lw, tm):
    M = bonus.shape[0]
    tile = pl.BlockSpec((tm, A_WIDTH), lambda i: (i, 0))
    o_tile = pl.BlockSpec((A_WIDTH, tm), lambda i: (0, i))
    l = lw["l"]
    row = pl.BlockSpec((None, 1, A_WIDTH), lambda i: (l, 0, 0))
    return pl.pallas_call(
        _rwkv_post_kernel,
        grid=(M // tm,),
        in_specs=[o_tile, tile, tile, row, row, pl.BlockSpec((SEG, SEG), lambda i: (0, 0))],
        out_specs=tile,
        out_shape=jax.ShapeDtypeStruct((M, A_WIDTH), BF16),
        compiler_params=_cparams(1),
        name="rwkv_post",
    )(o, bonus, g, lw["ln_w"], lw["ln_b"], lw["bsel"])


def _rope_kernel(x_ref, c_ref, s1_ref, s2_ref, o_ref):
    x = x_ref[...]
    fwd = pltpu.roll(x, x.shape[1] - ROPE_HALF, axis=1)
    bwd = pltpu.roll(x, ROPE_HALF, axis=1)
    o_ref[...] = x * c_ref[...] + fwd * s1_ref[...] + bwd * s2_ref[...]


def _rope(proj, tables, tm):
    M = proj.shape[0]
    n_tab = tables[0].shape[0] // tm
    tab = pl.BlockSpec((tm, SEG), lambda i, j: (i % n_tab, 0))
    return pl.pallas_call(
        _rope_kernel,
        grid=(M // tm, QK_W // SEG),
        in_specs=[pl.BlockSpec((tm, SEG), lambda i, j: (i, j)), tab, tab, tab],
        out_specs=pl.BlockSpec((tm, SEG), lambda i, j: (i, j)),
        out_shape=jax.ShapeDtypeStruct((M, QK_W), F32),
        compiler_params=_cparams(2),
        name="rope",
    )(proj, *tables)


def _swa_prompt_kernel(q_ref, kc_ref, kp_ref, vc_ref, vp_ref, sink_ref, o_ref):
    n = pl.program_id(1)
    blocks = q_ref.shape[0] // WINDOW
    rows = GROUP * WINDOW
    i = lax.broadcasted_iota(jnp.int32, (rows, 2 * WINDOW), 0) % WINDOW
    j = lax.broadcasted_iota(jnp.int32, (rows, 2 * WINDOW), 1)
    diff = WINDOW + i - j
    band = (diff >= 0) & (diff < WINDOW)
    q = q_ref[...].astype(BF16)
    kall = jnp.concatenate([kp_ref[...], kc_ref[...]], axis=0).astype(BF16)
    vall = jnp.concatenate([vp_ref[...], vc_ref[...]], axis=0).astype(BF16)
    units = [(qb, kh) for qb in range(blocks) for kh in range(KV_HEADS)]
    ksl = [slice(kh * HEAD_DIM, (kh + 1) * HEAD_DIM) for kh in range(KV_HEADS)]
    mask = [band & ((n > 0) | (j >= WINDOW)) if qb == 0 else band for qb, _ in units]
    kk = [kall[qb * WINDOW:(qb + 2) * WINDOW, ksl[kh]] for qb, kh in units]
    vv = [vall[qb * WINDOW:(qb + 2) * WINDOW, ksl[kh]] for qb, kh in units]
    qh = [jnp.concatenate([q[qb * WINDOW:(qb + 1) * WINDOW, (kh * GROUP + g) * HEAD_DIM:(kh * GROUP + g + 1) * HEAD_DIM]
                           for g in range(GROUP)], axis=0) for qb, kh in units]
    us = range(len(units))
    sink = [sink_ref[kh] for _, kh in units]
    s = [lax.dot_general(qh[u], kk[u], (((1,), (1,)), ((), ())), preferred_element_type=F32) for u in us]
    s = [jnp.where(mask[u], s[u] * (HEAD_DIM ** -0.5), -jnp.inf) for u in us]
    m = [jnp.maximum(jnp.max(s[u], axis=1, keepdims=True), sink[u]) for u in us]
    e = [jnp.exp(s[u] - m[u]) for u in us]
    denom = [jnp.sum(e[u], axis=1, keepdims=True) + jnp.exp(sink[u] - m[u]) for u in us]
    o = [_bdot((e[u] / denom[u]).astype(BF16), vv[u]) for u in us]
    for qb in range(blocks):
        outs = [o[qb * KV_HEADS + kh][g * WINDOW:(g + 1) * WINDOW, :] for kh in range(KV_HEADS) for g in range(GROUP)]
        o_ref[qb * WINDOW:(qb + 1) * WINDOW, :] = jnp.concatenate(outs, axis=1).astype(o_ref.dtype)


def _swa_prompt(qk_rot, proj, lw, B, T):
    nb = T // WINDOW
    rows = GROUP * WINDOW
    kcol = Q_W // KV_W
    vcol = QK_W // KV_W
    l = lw["l"]
    blocks = min(SWA_BLOCKS, nb)
    ng = nb // blocks
    span = blocks * WINDOW
    cur = lambda col: (lambda b, n: (b * ng + n, col))
    prev = lambda col: (lambda b, n: (b * nb + jnp.maximum(n * blocks - 1, 0), col))
    return pl.pallas_call(
        _swa_prompt_kernel,
        grid=(B, ng),
        in_specs=[pl.BlockSpec((span, Q_W), cur(0)),
                  pl.BlockSpec((span, KV_W), cur(kcol)),
                  pl.BlockSpec((WINDOW, KV_W), prev(kcol)),
                  pl.BlockSpec((span, KV_W), cur(vcol)),
                  pl.BlockSpec((WINDOW, KV_W), prev(vcol)),
                  pl.BlockSpec((None, KV_HEADS, rows, 1), lambda b, n: (l, 0, 0, 0))],
        out_specs=pl.BlockSpec((span, Q_W), cur(0)),
        out_shape=jax.ShapeDtypeStruct((B * T, Q_W), BF16),
        compiler_params=_cparams(2),
        name="swa_prompt",
    )(qk_rot, qk_rot, qk_rot, proj, proj, lw["sink_p"])


def _swa_sample_kernel(q_ref, kn_ref, vn_ref, ck_ref, cv_ref, sink_ref, *rest, steps, n_prev):
    if n_prev:
        pk_ref, pv_ref, o_ref, nk_ref, nv_ref = rest
        for li in range(n_prev):
            nk_ref[li] = pk_ref[li]
            nv_ref[li] = pv_ref[li]
    else:
        o_ref, nk_ref, nv_ref = rest
    first_new = SUBLANE - steps
    new_rows = lax.broadcasted_iota(jnp.int32, (SUBLANE, KV_W), 0) >= first_new
    for b in range(q_ref.shape[0]):
        for c_ref, n_ref, dst in ((ck_ref, kn_ref, nk_ref), (cv_ref, vn_ref, nv_ref)):
            rolled = pltpu.roll(c_ref[b], WINDOW - steps, axis=0)
            tail = jnp.where(new_rows, n_ref[b], rolled[WINDOW - SUBLANE:, :])
            dst[n_prev, b] = jnp.concatenate([rolled[:WINDOW - SUBLANE, :], tail], axis=0)
    rows = GROUP * steps
    t_row = lax.broadcasted_iota(jnp.int32, (2 * rows, WINDOW), 0) % steps
    c_col = lax.broadcasted_iota(jnp.int32, (2 * rows, WINDOW), 1)
    cache_mask = c_col > t_row
    t_col = lax.broadcasted_iota(jnp.int32, (2 * rows, 1), 0) % steps
    scale = HEAD_DIM ** -0.5
    zq = jnp.zeros((rows, HEAD_DIM), F32)
    units = [(b, p) for b in range(q_ref.shape[0]) for p in range(KV_HEADS // 2)]
    us = range(len(units))
    lanes = [slice(p * LANE, (p + 1) * LANE) for _, p in units]
    q2 = [jnp.concatenate([jnp.concatenate([q_ref[b, 2 * p], zq], axis=1),
                           jnp.concatenate([zq, q_ref[b, 2 * p + 1]], axis=1)], axis=0).astype(BF16) for b, p in units]
    kc = [ck_ref[b, :, lanes[u]].astype(BF16) for u, (b, p) in enumerate(units)]
    vc = [cv_ref[b, :, lanes[u]].astype(BF16) for u, (b, p) in enumerate(units)]
    kn = [kn_ref[b, first_new:, lanes[u]].astype(BF16).astype(F32) for u, (b, p) in enumerate(units)]
    vn = [vn_ref[b, first_new:, lanes[u]].astype(BF16).astype(F32) for u, (b, p) in enumerate(units)]
    sink = [jnp.concatenate([sink_ref[2 * p], sink_ref[2 * p + 1]], axis=0) for _, p in units]
    sc = [lax.dot_general(q2[u], kc[u], (((1,), (1,)), ((), ())), preferred_element_type=F32) for u in us]
    sc = [jnp.where(cache_mask, sc[u] * scale, -jnp.inf) for u in us]
    q2f = [q2[u].astype(F32) for u in us]
    sn = [[jnp.where(t_col >= t, jnp.sum(q2f[u] * kn[u][t:t + 1, :], axis=1, keepdims=True) * scale, -jnp.inf)
           for t in range(steps)] for u in us]
    m = [jnp.maximum(jnp.max(sc[u], axis=1, keepdims=True), sink[u]) for u in us]
    for t in range(steps):
        m = [jnp.maximum(m[u], sn[u][t]) for u in us]
    ec = [jnp.exp(sc[u] - m[u]) for u in us]
    en = [[jnp.exp(sn[u][t] - m[u]) for t in range(steps)] for u in us]
    denom = [jnp.sum(ec[u], axis=1, keepdims=True) + jnp.exp(sink[u] - m[u]) for u in us]
    for t in range(steps):
        denom = [denom[u] + en[u][t] for u in us]
    o = [_bdot((ec[u] / denom[u]).astype(BF16), vc[u]) for u in us]
    for t in range(steps):
        o = [o[u] + (en[u][t] / denom[u]).astype(BF16).astype(F32) * vn[u][t:t + 1, :] for u in us]
    for u, (b, p) in enumerate(units):
        o_ref[b, 2 * p] = o[u][0:rows, 0:HEAD_DIM]
        o_ref[b, 2 * p + 1] = o[u][rows:2 * rows, HEAD_DIM:LANE]


def _swa_sample(qk_rot, proj, cache_k, cache_v, prev_k, prev_v, lw, B, T):
    l = lw["l"]
    rows = GROUP * T
    qg = qk_rot[:, :Q_W].reshape(T, B, KV_HEADS, GROUP, HEAD_DIM).transpose(1, 2, 3, 0, 4)
    qg = qg.reshape(B, KV_HEADS, rows, HEAD_DIM)
    pad = ((0, 0), (SUBLANE - T, 0), (0, 0))
    kn = jnp.pad(qk_rot[:, Q_W:].reshape(T, B, KV_W).transpose(1, 0, 2), pad)
    vn = jnp.pad(proj[:, QK_W:].reshape(T, B, KV_W).transpose(1, 0, 2), pad)
    nb = SUBLANE if B % SUBLANE == 0 else 1
    new_spec = pl.BlockSpec((nb, SUBLANE, KV_W), lambda b: (b, 0, 0))
    cache_spec = pl.BlockSpec((None, nb, WINDOW, KV_W), lambda b: (l, b, 0, 0))
    in_specs = [pl.BlockSpec((nb, KV_HEADS, rows, HEAD_DIM), lambda b: (b, 0, 0, 0)),
                new_spec, new_spec, cache_spec, cache_spec,
                pl.BlockSpec((None, KV_HEADS, rows, 1), lambda b: (l, 0, 0, 0))]
    args = [qg, kn, vn, cache_k, cache_v, lw["sink_s"]]
    if l:
        in_specs += [pl.BlockSpec((l, nb, WINDOW, KV_W), lambda b: (0, b, 0, 0))] * 2
        args += [prev_k, prev_v]
    stacked = pl.BlockSpec((l + 1, nb, WINDOW, KV_W), lambda b: (0, b, 0, 0))
    stacked_shape = jax.ShapeDtypeStruct((l + 1, B, WINDOW, KV_W), F32)
    o, new_k, new_v = pl.pallas_call(
        functools.partial(_swa_sample_kernel, steps=T, n_prev=l),
        grid=(B // nb,),
        in_specs=in_specs,
        out_specs=[pl.BlockSpec((nb, KV_HEADS, rows, HEAD_DIM), lambda b: (b, 0, 0, 0)), stacked, stacked],
        out_shape=[jax.ShapeDtypeStruct((B, KV_HEADS, rows, HEAD_DIM), F32), stacked_shape, stacked_shape],
        compiler_params=_cparams(1),
        name="swa_sample",
    )(*args)
    o = o.reshape(B, KV_HEADS, GROUP, T, HEAD_DIM).transpose(3, 0, 1, 2, 4).reshape(T * B, Q_W)
    return o.astype(BF16), new_k, new_v


def _rope_tables(pos):
    inv = 1.0 / (ROPE_THETA ** (jnp.arange(0, ROPE_DIM, 2, dtype=F32) / ROPE_DIM))
    ang = pos.astype(F32)[:, None] * inv[None, :]
    cos, sin = jnp.cos(ang), jnp.sin(ang)
    n = pos.shape[0]
    ones = jnp.ones((n, HEAD_DIM - ROPE_DIM), F32)
    zeros = jnp.zeros((n, HEAD_DIM - ROPE_DIM), F32)
    zh = jnp.zeros((n, ROPE_HALF), F32)
    c = jnp.concatenate([cos, cos, ones], axis=1)
    s1 = jnp.concatenate([-sin, zh, zeros], axis=1)
    s2 = jnp.concatenate([zh, sin, zeros], axis=1)
    return tuple(jnp.tile(a, (1, SEG // HEAD_DIM)) for a in (c, s1, s2))


def _stacked_params(t_sample, norm_g, ffn_w_gate, ffn_w_up, ffn_w_down, w_in, rwkv_mu, rwkv_w0, rwkv_w2,
                    rwkv_a0, rwkv_a2, rwkv_g2, rwkv_k_k, rwkv_k_a, rwkv_r_k, rwkv_ln_w, rwkv_ln_b, attn_sinks,
                    w_branch_a, w_branch_b, w_merge_gate, w_out):
    L = w_in.shape[0]
    lora_rows = lambda w, off: jnp.pad(w, ((0, 0), (off, LORA_PAD - off - w.shape[1]), (0, 0))).astype(BF16)
    seg = jnp.arange(SEG) // A_HEAD_DIM
    row = lambda a: a.reshape(L, 1, -1)
    sink_rows = lambda n: jnp.repeat(attn_sinks.reshape(L, KV_HEADS, GROUP), n, axis=2).reshape(L, KV_HEADS, GROUP * n, 1)
    return {
        "norm_g": norm_g, "wg": ffn_w_gate, "wu": ffn_w_up, "wd": ffn_w_down,
        "w_in": w_in, "w_qkv": w_in[:, :, A_PROJ:],
        "mu": row(jnp.pad(rwkv_mu, ((0, 0), (0, A_PAD - A_PROJ)))),
        "w0": row(rwkv_w0), "a0": row(rwkv_a0), "k_k": row(rwkv_k_k), "k_a": row(rwkv_k_a),
        "r_k": row(rwkv_r_k), "ln_w": row(rwkv_ln_w), "ln_b": row(rwkv_ln_b),
        "w2": lora_rows(rwkv_w2, 0), "a2": lora_rows(rwkv_a2, DECAY_LORA),
        "g2": lora_rows(rwkv_g2, DECAY_LORA + ICLR_LORA),
        "bsel": (seg[:, None] == seg[None, :]).astype(BF16),
        "sink_p": sink_rows(WINDOW), "sink_s": sink_rows(t_sample),
        "w_a": w_branch_a, "w_b": w_branch_b, "w_gate": w_merge_gate, "w_out": w_out,
    }


def _decoder_layer(x, mod, lw, B, T, shift_prev, tables, sample=None):
    M = B * T
    l = lw["l"]
    sp = jnp.pad(shift_prev, ((0, 0), (0, A_PAD - A_PROJ)))
    if sample is not None:
        tm = M
        mod = mod.reshape(1, B, N_MOD * D_MODEL)
        gmap = lambda tile: (lambda i: 0)
        sp = sp.reshape(1, B, A_PAD)
    else:
        tm = min(T, 1024)
        mod = mod.reshape(B, 1, N_MOD * D_MODEL)
        gmap = lambda tile: (lambda i: i // (T // tile))
        sp = sp.reshape(B, 1, A_PAD)
    tm_k = min(tm, 512)

    def ffn(x, n, which):
        h = _ada_norm(x, lw["norm_g"], l, n, mod, gmap(tm), tm)
        act = _ffn_gate_up(h, lw["wg"], lw["wu"], (l, which), tm)
        return _mm_residual(act, lw["wd"], (l, which), x, mod, 3 * n + 2, gmap(tm_k), 0.5, tm_k)

    x = ffn(x, 0, 0)

    h = _ada_norm(x, lw["norm_g"], l, 1, mod, gmap(tm), tm)
    proj_a = _mm_plain(h, lw["w_in"], (l,), tm, tn=TN_A, n_cols=A_PAD, name="in_proj_a")
    proj_b = _mm_plain(h, lw["w_qkv"], (l,), tm, tn=TN_B, name="in_proj_b")
    gates = _mm_plain(h, lw["w_gate"], (l,), tm, tn=TN_WIDE, sigmoid=True, name="merge_gates")

    qk_rot = _rope(proj_b, tables, min(M, ROPE_ROWS))
    if sample is not None:
        state_wkv, cache_k, cache_v, prev_wkv, prev_k, prev_v = sample
        r, w, k2, v, kk, kka, g, bonus = _rwkv_prep(proj_a, sp, lw, B)
        o, new_wkv = _rwkv_lane_scan((r, w, k2, v, kk, kka), state_wkv, prev_wkv, l, B, T)
        ya = _rwkv_post(o, bonus, g, lw, B)
        new_shift = proj_a[(T - 1) * B:, :A_PROJ]
        yb, new_k, new_v = _swa_sample(qk_rot, proj_b, cache_k, cache_v, prev_k, prev_v, lw, B, T)
    else:
        zero_wkv = jnp.zeros((B, A_HEADS, A_HEAD_DIM, A_HEAD_DIM), F32)
        ya, new_wkv = _rwkv_chunked(proj_a, sp, lw, zero_wkv, B, T)
        new_shift = proj_a.reshape(B, T, A_PAD)[:, -1, :A_PROJ]
        yb = _swa_prompt(qk_rot, proj_b, lw, B, T)
        new_k = qk_rot[:, Q_W:].reshape(B, T, KV_HEADS, HEAD_DIM)[:, -WINDOW:]
        new_v = proj_b[:, QK_W:].reshape(B, T, KV_HEADS, HEAD_DIM)[:, -WINDOW:]

    merged = _mm_merge(ya, yb, lw["w_a"], lw["w_b"], (l,), gates, tm, tn=TN_WIDE)
    x = _mm_residual(merged, lw["w_out"], (l,), x, mod, 5, gmap(tm), 1.0, tm, tn=TN_WIDE)

    x = ffn(x, 2, 1)
    return x, new_shift, new_wkv, new_k, new_v


def kernel(x_prompt, x_sample, c_prompt, c_sample, state_shift, state_wkv, cache_k, cache_v, norm_g, w_mod, b_mod,
           ffn_w_gate, ffn_w_up, ffn_w_down, w_in, rwkv_mu, rwkv_w0, rwkv_w2, rwkv_a0, rwkv_a2, rwkv_g2, rwkv_k_k,
           rwkv_k_a, rwkv_r_k, rwkv_ln_w, rwkv_ln_b, attn_sinks, w_branch_a, w_branch_b, w_merge_gate, w_out,
           final_norm):
    Bp, Tp, _ = x_prompt.shape
    Bs, Ts, _ = x_sample.shape
    L = w_in.shape[0]
    tab_p = _rope_tables(jnp.arange(Tp))
    tab_s = _rope_tables(jnp.repeat(PAST_LEN + jnp.arange(Ts), Bs))
    zero_shift = jnp.zeros((Bp, A_PROJ), F32)
    xp = x_prompt.reshape(Bp * Tp, D_MODEL)
    xs = x_sample.transpose(1, 0, 2).reshape(Ts * Bs, D_MODEL)
    c_all = jnp.concatenate([c_prompt, c_sample], axis=0)
    pad_rows = (-c_all.shape[0]) % 16
    c_all = jnp.pad(c_all, ((0, pad_rows), (0, 0)))
    pw = _stacked_params(Ts, norm_g, ffn_w_gate, ffn_w_up, ffn_w_down, w_in, rwkv_mu, rwkv_w0, rwkv_w2, rwkv_a0,
                         rwkv_a2, rwkv_g2, rwkv_k_k, rwkv_k_a, rwkv_r_k, rwkv_ln_w, rwkv_ln_b, attn_sinks,
                         w_branch_a, w_branch_b, w_merge_gate, w_out)
    state_s = state_wkv.reshape(L, Bs, A_HEADS * A_HEAD_DIM * A_HEAD_DIM)
    ck = cache_k.reshape(L, Bs, WINDOW, KV_W)
    cv = cache_v.reshape(L, Bs, WINDOW, KV_W)
    outs_p, shifts_s = [], []
    wkv_s = k_s = v_s = None
    for l in range(L):
        lw = dict(pw, l=l)
        mod = _modulation(c_all, w_mod, b_mod, l)
        xp, *st_p = _decoder_layer(xp, mod[:Bp], lw, Bp, Tp, zero_shift, tab_p)
        xs, shift_s, wkv_s, k_s, v_s = _decoder_layer(xs, mod[Bp:Bp + Bs], lw, Bs, Ts, state_shift[l], tab_s,
                                                      sample=(state_s, ck, cv, wkv_s, k_s, v_s))
        outs_p.append(st_p)
        shifts_s.append(shift_s)
    y_prompt = _final_norm(xp, final_norm, 1024).reshape(Bp, Tp, D_MODEL)
    y_sample = _final_norm(xs, final_norm, Bs * Ts).reshape(Ts, Bs, D_MODEL).transpose(1, 0, 2)
    stack = lambda n: jnp.stack([o[n] for o in outs_p])
    return (y_prompt, y_sample, stack(0), stack(1), stack(2), stack(3),
            jnp.stack(shifts_s), wkv_s.reshape(state_wkv.shape), k_s.reshape(cache_k.shape),
            v_s.reshape(cache_v.shape))
```

```python
import functools

import jax
import jax.numpy as jnp
from jax import lax
from jax.experimental import pallas as pl
from jax.experimental.pallas import tpu as pltpu

D_MODEL = 2048
DEPTH = 2
A_HEADS = 16
A_HEAD_DIM = 64
A_WIDTH = A_HEADS * A_HEAD_DIM
DECAY_LORA = 64
ICLR_LORA = 64
GATE_LORA = 160
LORA_W = DECAY_LORA + ICLR_LORA + GATE_LORA
A_PROJ = 3 * A_WIDTH + LORA_W
GN_EPS = 64e-5
Q_HEADS = 16
KV_HEADS = 4
GROUP = Q_HEADS // KV_HEADS
HEAD_DIM = 64
Q_W = Q_HEADS * HEAD_DIM
KV_W = KV_HEADS * HEAD_DIM
WINDOW = 128
ROPE_DIM = HEAD_DIM // 4
ROPE_HALF = ROPE_DIM // 2
ROPE_THETA = 500000.0
D_FF = 5632
N_MOD = 9
RMS_EPS = 1e-6

PAST_LEN = 8192
LANE = 128
SUBLANE = 8
ROPE_ROWS = 2048
SWA_BLOCKS = 4
CHUNK = 64
SOLVE_BASE = 8
SEG = 256
A_PAD = 3584
LORA_PAD = A_PAD - 3 * A_WIDTH
LORA_DI = 128
LORA_G = 256
assert DECAY_LORA + ICLR_LORA == LORA_DI and GATE_LORA <= LORA_G and LORA_DI + LORA_G <= LORA_PAD
QK_W = Q_W + KV_W
TN_A = 896
TN_B = 768
VMEM_LIMIT = 56 * 1024 * 1024
TN_WIDE = 1024

F32 = jnp.float32
BF16 = jnp.bfloat16


def _cparams(n_axes):
    return pltpu.CompilerParams(dimension_semantics=("arbitrary",) * n_axes,
                                vmem_limit_bytes=VMEM_LIMIT)


def _bdot(a, b):
    return jnp.dot(a, b, preferred_element_type=F32)


def _seg_sum(x, bsel):
    hi = x.astype(BF16)
    lo = (x - hi.astype(F32)).astype(BF16)
    seg = bsel.shape[0]
    parts = []
    for c in range(x.shape[1] // seg):
        sl = slice(c * seg, (c + 1) * seg)
        parts.append(_bdot(hi[:, sl], bsel) + _bdot(lo[:, sl], bsel))
    return parts[0] if len(parts) == 1 else jnp.concatenate(parts, axis=1)


def _mod_kernel(c_ref, w_ref, b_ref, o_ref):
    c = c_ref[...]
    a = (c * jax.nn.sigmoid(c)).astype(BF16)
    o_ref[...] = _bdot(a, w_ref[...].astype(BF16)) + b_ref[...]


def _modulation(c, w_mod, b_mod, l):
    R = c.shape[0]
    N = w_mod.shape[-1]
    tn = 1024
    return pl.pallas_call(
        _mod_kernel,
        grid=(N // tn,),
        in_specs=[pl.BlockSpec((R, D_MODEL), lambda j: (0, 0)),
                  pl.BlockSpec((None, D_MODEL, tn), lambda j: (l, 0, j)),
                  pl.BlockSpec((None, 1, tn), lambda j: (l, 0, j))],
        out_specs=pl.BlockSpec((R, tn), lambda j: (0, j)),
        out_shape=jax.ShapeDtypeStruct((R, N), F32),
        compiler_params=_cparams(1),
        name="modulation",
    )(c, w_mod, b_mod.reshape(b_mod.shape[0], 1, N))


def _mod_rows(ref, tm):
    val = ref[...]
    reps = tm // val.shape[0]
    return val if val.shape[0] == 1 or reps == 1 else jnp.concatenate([val] * reps, axis=0)


def _ada_norm_kernel(x_ref, g_ref, sh_ref, sc_ref, o_ref):
    x = x_ref[...]
    tm = x.shape[0]
    y = x * lax.rsqrt(jnp.mean(x * x, axis=-1, keepdims=True) + RMS_EPS)
    y = y * g_ref[...]
    o_ref[...] = (y * (1.0 + _mod_rows(sc_ref, tm)) + _mod_rows(sh_ref, tm)).astype(o_ref.dtype)


def _ada_norm(x, norm_g, l, n, mod, gmap, tm):
    M = x.shape[0]
    R = mod.shape[1]
    mspec = lambda col: pl.BlockSpec((None, R, D_MODEL), lambda i: (gmap(i), 0, col))
    return pl.pallas_call(
        _ada_norm_kernel,
        grid=(M // tm,),
        in_specs=[pl.BlockSpec((tm, D_MODEL), lambda i: (i, 0)),
                  pl.BlockSpec((None, None, 1, D_MODEL), lambda i: (l, n, 0, 0)),
                  mspec(3 * n), mspec(3 * n + 1)],
        out_specs=pl.BlockSpec((tm, D_MODEL), lambda i: (i, 0)),
        out_shape=jax.ShapeDtypeStruct((M, D_MODEL), BF16),
        compiler_params=_cparams(1),
        name="ada_norm",
    )(x, norm_g.reshape(norm_g.shape[0], norm_g.shape[1], 1, D_MODEL), mod, mod)


def _final_norm_kernel(x_ref, g_ref, o_ref):
    x = x_ref[...]
    y = x * lax.rsqrt(jnp.mean(x * x, axis=-1, keepdims=True) + RMS_EPS)
    o_ref[...] = y * g_ref[...]


def _final_norm(x, gain, tm):
    M = x.shape[0]
    return pl.pallas_call(
        _final_norm_kernel,
        grid=(M // tm,),
        in_specs=[pl.BlockSpec((tm, D_MODEL), lambda i: (i, 0)),
                  pl.BlockSpec((1, D_MODEL), lambda i: (0, 0))],
        out_specs=pl.BlockSpec((tm, D_MODEL), lambda i: (i, 0)),
        out_shape=jax.ShapeDtypeStruct((M, D_MODEL), F32),
        compiler_params=_cparams(1),
        name="final_norm",
    )(x, gain.reshape(1, D_MODEL))


def _wspec(w, lead, tn, single_buffer=False):
    block = (None,) * len(lead) + (w.shape[-2], tn)
    index = lambda j, i: lead + (0, j)
    if single_buffer:
        return pl.BlockSpec(block, index, pipeline_mode=pl.Buffered(1))
    return pl.BlockSpec(block, index)


def _wdot(a, w_ref):
    return _bdot(a, w_ref[...].astype(BF16))


def _gu_kernel(h_ref, wg_ref, wu_ref, o_ref):
    h = h_ref[...]
    g = _wdot(h, wg_ref)
    u = _wdot(h, wu_ref)
    o_ref[...] = ((g * jax.nn.sigmoid(g)) * u).astype(o_ref.dtype)


def _ffn_gate_up(h, wg, wu, lead, tm, tn=512):
    M, K = h.shape
    N = wg.shape[-1]
    return pl.pallas_call(
        _gu_kernel,
        grid=(N // tn, M // tm),
        in_specs=[pl.BlockSpec((tm, K), lambda j, i: (i, 0)), _wspec(wg, lead, tn), _wspec(wu, lead, tn)],
        out_specs=pl.BlockSpec((tm, tn), lambda j, i: (i, j)),
        out_shape=jax.ShapeDtypeStruct((M, N), BF16),
        compiler_params=_cparams(2),
        name="ffn_gate_up",
    )(h, wg, wu)


def _resid_kernel(a_ref, w_ref, x_ref, gate_ref, o_ref, *, coef):
    y = _wdot(a_ref[...], w_ref)
    o_ref[...] = x_ref[...] + (coef * _mod_rows(gate_ref, y.shape[0])) * y


def _mm_residual(a, w, lead, x, mod, n_gate, gmap, coef, tm, tn=512, single_buffer_w=False):
    M, K = a.shape
    N = w.shape[-1]
    R = mod.shape[1]
    nj = N // tn
    return pl.pallas_call(
        functools.partial(_resid_kernel, coef=coef),
        grid=(nj, M // tm),
        in_specs=[pl.BlockSpec((tm, K), lambda j, i: (i, 0)),
                  _wspec(w, lead, tn, single_buffer_w),
                  pl.BlockSpec((tm, tn), lambda j, i: (i, j)),
                  pl.BlockSpec((None, R, tn), lambda j, i: (gmap(i), 0, n_gate * nj + j))],
        out_specs=pl.BlockSpec((tm, tn), lambda j, i: (i, j)),
        out_shape=jax.ShapeDtypeStruct((M, N), F32),
        compiler_params=_cparams(2),
        name="mm_residual",
    )(a, w, x, mod)


def _plain_kernel(a_ref, w_ref, o_ref, *, sigmoid):
    y = _wdot(a_ref[...], w_ref)
    if sigmoid:
        y = jax.nn.sigmoid(y)
    o_ref[...] = y.astype(o_ref.dtype)


def _mm_plain(a, w, lead, tm, tn=512, sigmoid=False, n_cols=None, name="mm_plain"):
    M, K = a.shape
    N = n_cols or w.shape[-1]
    return pl.pallas_call(
        functools.partial(_plain_kernel, sigmoid=sigmoid),
        grid=(N // tn, M // tm),
        in_specs=[pl.BlockSpec((tm, K), lambda j, i: (i, 0)), _wspec(w, lead, tn)],
        out_specs=pl.BlockSpec((tm, tn), lambda j, i: (i, j)),
        out_shape=jax.ShapeDtypeStruct((M, N), F32),
        compiler_params=_cparams(2),
        name=name,
    )(a, w)


def _merge_kernel(ya_ref, yb_ref, wa_ref, wb_ref, ga_ref, gb_ref, o_ref):
    pa = _wdot(ya_ref[...], wa_ref)
    pb = _wdot(yb_ref[...], wb_ref)
    o_ref[...] = (ga_ref[...] * pa + gb_ref[...] * pb).astype(o_ref.dtype)


def _mm_merge(ya, yb, wa, wb, lead, gates, tm, tn=512):
    M, K = ya.shape
    N = wa.shape[-1]
    nj = N // tn
    return pl.pallas_call(
        _merge_kernel,
        grid=(nj, M // tm),
        in_specs=[pl.BlockSpec((tm, K), lambda j, i: (i, 0)),
                  pl.BlockSpec((tm, K), lambda j, i: (i, 0)),
                  _wspec(wa, lead, tn), _wspec(wb, lead, tn),
                  pl.BlockSpec((tm, tn), lambda j, i: (i, j)),
                  pl.BlockSpec((tm, tn), lambda j, i: (i, j + nj))],
        out_specs=pl.BlockSpec((tm, tn), lambda j, i: (i, j)),
        out_shape=jax.ShapeDtypeStruct((M, N), BF16),
        compiler_params=_cparams(2),
        name="mm_merge",
    )(ya, yb, wa, wb, gates, gates)


def _rwkv_operands(pa, prev, mu, w0, a0, k_k, k_a, r_k, w2_ref, a2_ref, g2_ref, head_sums):
    xm = pa + (prev - pa) * mu
    r = xm[:, 0:A_WIDTH]
    k = xm[:, A_WIDTH:2 * A_WIDTH]
    v = xm[:, 2 * A_WIDTH:3 * A_WIDTH]
    lo_di = xm[:, 3 * A_WIDTH:3 * A_WIDTH + LORA_DI]
    lo_g = xm[:, 3 * A_WIDTH + LORA_DI:3 * A_WIDTH + LORA_DI + LORA_G]
    m_w = _bdot(jnp.tanh(lo_di).astype(BF16), w2_ref[0:LORA_DI, :])
    m_a = _bdot(lo_di.astype(BF16), a2_ref[0:LORA_DI, :])
    g = _bdot(jax.nn.sigmoid(lo_g).astype(BF16), g2_ref[LORA_DI:LORA_DI + LORA_G, :])
    z = -(w0 + m_w)
    softplus = jnp.maximum(z, 0.0) + jnp.log1p(jnp.exp(-jnp.abs(z)))
    w_log = -softplus - 0.5
    a = jax.nn.sigmoid(a0 + m_a)
    kkf = k * k_k
    kk = kkf / jnp.maximum(jnp.sqrt(head_sums(kkf * kkf)), 1e-12)
    k2 = k * (1.0 + (a - 1.0) * k_a)
    bonus = head_sums(r * k2 * r_k) * v
    return r, -jnp.exp(w_log), k2, v, kk, kk * a, g, bonus


def _rwkv_prep_kernel(p_ref, sp_ref, mu_ref, w0_ref, a0_ref, kk_ref, ka_ref, rk_ref,
                      w2_ref, a2_ref, g2_ref, bsel_ref,
                      r_out, d_out, k_out, v_out, kk_out, kka_out, g_out, bonus_out, carry_ref):
    t = pl.program_id(0)
    pa = p_ref[...]

    @pl.when(t == 0)
    def _():
        carry_ref[...] = sp_ref[...]

    prev = carry_ref[...]
    carry_ref[...] = pa
    bsel = bsel_ref[...]
    r, w, k2, v, kk, kka, g, bonus = _rwkv_operands(
        pa, prev, mu_ref[...], w0_ref[...], a0_ref[...], kk_ref[...], ka_ref[...], rk_ref[...],
        w2_ref, a2_ref, g2_ref, lambda x: _seg_sum(x, bsel))
    for ref, val in ((r_out, r), (d_out, w), (k_out, k2), (v_out, v), (kk_out, kk), (kka_out, kka)):
        ref[...] = val.T
    g_out[...] = g
    bonus_out[...] = bonus


def _rwkv_prep(proj, sp, lw, n_seq):
    M = proj.shape[0]
    l = lw["l"]
    row = lambda n: pl.BlockSpec((None, 1, n), lambda i: (l, 0, 0))
    full = lambda a, b: pl.BlockSpec((None, a, b), lambda i: (l, 0, 0))
    tok_spec = pl.BlockSpec((n_seq, A_WIDTH), lambda i: (i, 0))
    tok_shape = jax.ShapeDtypeStruct((M, A_WIDTH), F32)
    op_spec = pl.BlockSpec((A_WIDTH, n_seq), lambda i: (0, i))
    op_shape = jax.ShapeDtypeStruct((A_WIDTH, M), F32)
    return pl.pallas_call(
        _rwkv_prep_kernel,
        grid=(M // n_seq,),
        in_specs=[pl.BlockSpec((n_seq, A_PAD), lambda i: (i, 0)),
                  pl.BlockSpec((None, n_seq, A_PAD), lambda i: (0, 0, 0)),
                  row(A_PAD), row(A_WIDTH), row(A_WIDTH), row(A_WIDTH), row(A_WIDTH), row(A_WIDTH),
                  full(LORA_PAD, A_WIDTH), full(LORA_PAD, A_WIDTH), full(LORA_PAD, A_WIDTH),
                  pl.BlockSpec((SEG, SEG), lambda i: (0, 0))],
        out_specs=[op_spec] * 6 + [tok_spec] * 2,
        out_shape=[op_shape] * 6 + [tok_shape] * 2,
        scratch_shapes=[pltpu.VMEM((n_seq, A_PAD), F32)],
        compiler_params=_cparams(1),
        name="rwkv_prep",
    )(proj, sp, lw["mu"], lw["w0"], lw["a0"], lw["k_k"], lw["k_a"], lw["r_k"],
      lw["w2"], lw["a2"], lw["g2"], lw["bsel"])


def _rwkv_lane_scan_kernel(r_ref, w_ref, k_ref, v_ref, kk_ref, kka_ref, s0_ref, *rest, steps, n_prev):
    if n_prev:
        prev_ref, o_ref, s_out_ref, st_ref = rest
        for li in range(n_prev):
            s_out_ref[li] = prev_ref[li]
    else:
        o_ref, s_out_ref, st_ref = rest
    n = A_HEAD_DIM
    nb = s0_ref.shape[0]
    st_ref[...] = s0_ref[...].T
    tsl = [slice(t * nb, (t + 1) * nb) for t in range(steps)]

    def group(g, carry):
        rows8 = pl.ds(pl.multiple_of(g * SUBLANE, SUBLANE), SUBLANE)
        v8 = [v_ref[rows8, tsl[t]] for t in range(steps)]
        o8 = [[] for _ in range(steps)]
        for u in range(SUBLANE):
            srows = pl.ds(pl.multiple_of((g * SUBLANE + u) * n, n), n)
            s = st_ref[srows, :]
            for t in range(steps):
                sk = jnp.sum(s * kk_ref[:, tsl[t]], axis=0, keepdims=True)
                s = s * jnp.exp(w_ref[:, tsl[t]]) - sk * kka_ref[:, tsl[t]] + v8[t][u:u + 1, :] * k_ref[:, tsl[t]]
                o8[t].append(jnp.sum(s * r_ref[:, tsl[t]], axis=0, keepdims=True))
            st_ref[srows, :] = s
        for t in range(steps):
            o_ref[rows8, tsl[t]] = jnp.concatenate(o8[t], axis=0)
        return carry

    lax.fori_loop(0, n // SUBLANE, group, 0)
    s_out_ref[n_prev] = st_ref[...].T


def _rwkv_lane_scan(ops_t, state, prev, l, B, T):
    n2 = A_HEAD_DIM * A_HEAD_DIM
    ospec = pl.BlockSpec((A_HEAD_DIM, T * B), lambda h: (h, 0))
    in_specs = [ospec] * 6 + [pl.BlockSpec((None, B, n2), lambda h: (l, 0, h))]
    args = list(ops_t) + [state]
    if l:
        in_specs.append(pl.BlockSpec((l, B, n2), lambda h: (0, 0, h)))
        args.append(prev)
    return pl.pallas_call(
        functools.partial(_rwkv_lane_scan_kernel, steps=T, n_prev=l),
        grid=(A_HEADS,),
        in_specs=in_specs,
        out_specs=[ospec, pl.BlockSpec((l + 1, B, n2), lambda h: (0, 0, h))],
        out_shape=[jax.ShapeDtypeStruct((A_WIDTH, T * B), F32),
                   jax.ShapeDtypeStruct((l + 1, B, A_HEADS * n2), F32)],
        scratch_shapes=[pltpu.VMEM((n2, B), F32)],
        compiler_params=_cparams(1),
        name="rwkv_lane_scan",
    )(*args)


def _rwkv_chunk_kernel(p_ref, sp_ref, mu_ref, w0_ref, a0_ref, kk_ref, ka_ref, rk_ref, w2_ref, a2_ref, g2_ref,
                       lnw_ref, lnb_ref, s0_ref, y_ref, s_out_ref, s_ref, carry_ref):
    c = pl.program_id(1)
    L = CHUNK
    P = 2 * L
    n_pairs = A_HEADS // 2
    left = lax.broadcasted_iota(jnp.int32, (L, LANE), 1) < A_HEAD_DIM
    row = lax.broadcasted_iota(jnp.int32, (P, P), 0)
    col = lax.broadcasted_iota(jnp.int32, (P, P), 1)
    strict = row > col
    lower = row >= col
    eye = (row == col).astype(F32)
    same = lambda n: (row // n) == (col // n)
    base = same(SOLVE_BASE)
    levels = []
    n = SOLVE_BASE
    while n < L:
        levels.append(same(2 * n) & jnp.logical_not(same(n)))
        n *= 2
    bf = lambda x: x.astype(BF16)
    nt = lambda a, b: lax.dot_general(a, b, (((1,), (1,)), ((), ())), preferred_element_type=F32)
    tn = lambda a, b: lax.dot_general(a, b, (((0,), (0,)), ((), ())), preferred_element_type=F32)

    pairs = range(n_pairs)
    lanes = [slice(p * LANE, (p + 1) * LANE) for p in pairs]

    @pl.when(c == 0)
    def _():
        z = jnp.zeros((A_HEAD_DIM, A_HEAD_DIM), F32)
        for p in range(n_pairs):
            s_ref[p] = jnp.concatenate([jnp.concatenate([s0_ref[2 * p], z], axis=1),
                                        jnp.concatenate([z, s0_ref[2 * p + 1]], axis=1)], axis=0)
        carry_ref[...] = sp_ref[...]

    def head_sums(x):
        sa = jnp.sum(jnp.where(left, x, 0.0), axis=1, keepdims=True)
        sb = jnp.sum(jnp.where(left, 0.0, x), axis=1, keepdims=True)
        return jnp.where(left, sa, sb)

    pa = p_ref[...]
    first_row = lax.broadcasted_iota(jnp.int32, pa.shape, 0) == 0
    prev = jnp.where(first_row, carry_ref[...], pltpu.roll(pa, 1, axis=0))
    carry_ref[...] = pa[L - 1:L, :]
    r, w, k, v, kk, kka, g, bonus = _rwkv_operands(
        pa, prev, mu_ref[...], w0_ref[...], a0_ref[...], kk_ref[...], ka_ref[...], rk_ref[...],
        w2_ref, a2_ref, g2_ref, lambda x: jnp.concatenate([head_sums(x[:, sl]) for sl in lanes], axis=1))

    tl = lax.broadcasted_iota(jnp.int32, (L, L), 0) >= lax.broadcasted_iota(jnp.int32, (L, L), 1)
    tri = tl.astype(BF16)
    w_hi = bf(w)
    w_r1 = w - w_hi.astype(F32)
    w_mid = bf(w_r1)
    w_lo = bf(w_r1 - w_mid.astype(F32))
    cum = _bdot(tri, w_hi) + _bdot(tri, w_mid) + _bdot(tri, w_lo)
    g_in = jnp.exp(cum)
    g_inv = jnp.exp(-cum)
    g_last = g_in[L - 1:L, :]
    at = -kk * jnp.exp(cum - w)
    bt = kka * g_inv
    kt = k * g_inv
    rt = r * g_in
    bl = bt * g_last
    kl = kt * g_last


    def stack(x, p):
        xp = x[:, lanes[p]]
        return bf(jnp.concatenate([jnp.where(left, xp, 0.0), jnp.where(left, 0.0, xp)], axis=0))

    a2 = [stack(at, p) for p in pairs]
    r2 = [stack(rt, p) for p in pairs]
    b2 = [stack(bt, p) for p in pairs]
    k2 = [stack(kt, p) for p in pairs]
    v2 = [stack(v, p) for p in pairs]
    sc = [nt(jnp.concatenate([a2[p], r2[p]], axis=0), jnp.concatenate([b2[p], k2[p]], axis=0)) for p in pairs]
    a_ab = [jnp.where(strict, sc[p][0:P, 0:P], 0.0) for p in pairs]
    a_ak = [bf(jnp.where(strict, sc[p][0:P, P:2 * P], 0.0)) for p in pairs]
    a_rb = [bf(jnp.where(lower, sc[p][P:2 * P, 0:P], 0.0)) for p in pairs]
    a_rk = [bf(jnp.where(lower, sc[p][P:2 * P, P:2 * P], 0.0)) for p in pairs]
    pw = [bf(jnp.where(base, a_ab[p], 0.0)) for p in pairs]
    t_inv = [eye + pw[p].astype(F32) for p in pairs]
    m = 2
    while m < SOLVE_BASE:
        pw = [bf(_bdot(pw[p], pw[p])) for p in pairs]
        t_inv = [t_inv[p] + _bdot(bf(t_inv[p]), pw[p]) for p in pairs]
        m *= 2
    for lv in levels:
        half = [bf(_bdot(bf(t_inv[p]), bf(jnp.where(lv, a_ab[p], 0.0)))) for p in pairs]
        t_inv = [t_inv[p] + _bdot(half[p], bf(t_inv[p])) for p in pairs]
    s = [s_ref[p] for p in pairs]
    s_b = [bf(s[p]) for p in pairs]
    wmat = [bf(nt(a2[p], s_b[p]) + _bdot(a_ak[p], v2[p])) for p in pairs]
    u = [bf(_bdot(bf(t_inv[p]), wmat[p])) for p in pairs]
    uv = [jnp.concatenate([u[p], v2[p]], axis=0) for p in pairs]
    o2 = [nt(r2[p], s_b[p]) + _bdot(jnp.concatenate([a_rb[p], a_rk[p]], axis=1), uv[p]) for p in pairs]
    for p in pairs:
        upd = tn(uv[p], jnp.concatenate([stack(bl, p), stack(kl, p)], axis=0))
        s_ref[p] = s[p] * g_last[:, lanes[p]] + upd
    inv_n = 1.0 / A_HEAD_DIM
    o = [o2[p][0:L, :] + o2[p][L:P, :] for p in pairs]
    cen = [o[p] - head_sums(o[p]) * inv_n for p in pairs]
    var = [head_sums(cen[p] * cen[p]) * inv_n for p in pairs]
    for p in pairs:
        y = cen[p] * lax.rsqrt(var[p] + GN_EPS) * lnw_ref[:, lanes[p]] + lnb_ref[:, lanes[p]]
        y_ref[:, lanes[p]] = ((y + bonus[:, lanes[p]]) * g[:, lanes[p]]).astype(BF16)

    @pl.when(c == pl.num_programs(1) - 1)
    def _():
        for p in range(n_pairs):
            s = s_ref[p]
            s_out_ref[2 * p] = s[0:A_HEAD_DIM, 0:A_HEAD_DIM]
            s_out_ref[2 * p + 1] = s[A_HEAD_DIM:LANE, A_HEAD_DIM:LANE]


def _rwkv_chunked(proj, sp, lw, s0, B, T):
    l = lw["l"]
    row = lambda n: pl.BlockSpec((None, 1, n), lambda b, c: (l, 0, 0))
    lora = pl.BlockSpec((None, LORA_PAD, A_WIDTH), lambda b, c: (l, 0, 0))
    sspec = pl.BlockSpec((None, A_HEADS, A_HEAD_DIM, A_HEAD_DIM), lambda b, c: (b, 0, 0, 0))
    y, s_t = pl.pallas_call(
        _rwkv_chunk_kernel,
        grid=(B, T // CHUNK),
        in_specs=[pl.BlockSpec((None, CHUNK, A_PAD), lambda b, c: (b, c, 0)),
                  pl.BlockSpec((None, 1, A_PAD), lambda b, c: (b, 0, 0)),
                  row(A_PAD), row(A_WIDTH), row(A_WIDTH), row(A_WIDTH), row(A_WIDTH), row(A_WIDTH),
                  lora, lora, lora, row(A_WIDTH), row(A_WIDTH), sspec],
        out_specs=[pl.BlockSpec((None, CHUNK, A_WIDTH), lambda b, c: (b, c, 0)), sspec],
        out_shape=[jax.ShapeDtypeStruct((B, T, A_WIDTH), BF16),
                   jax.ShapeDtypeStruct((B, A_HEADS, A_HEAD_DIM, A_HEAD_DIM), F32)],
        scratch_shapes=[pltpu.VMEM((A_HEADS // 2, LANE, LANE), F32), pltpu.VMEM((1, A_PAD), F32)],
        compiler_params=_cparams(2),
        name="rwkv_chunk",
    )(proj.reshape(B, T, A_PAD), sp, lw["mu"], lw["w0"], lw["a0"], lw["k_k"], lw["k_a"], lw["r_k"],
      lw["w2"], lw["a2"], lw["g2"], lw["ln_w"], lw["ln_b"], s0)
    return y.reshape(B * T, A_WIDTH), s_t


def _rwkv_out(o, bonus, g, ln_w, ln_b, bsel):
    inv_n = 1.0 / A_HEAD_DIM
    mean = _seg_sum(o, bsel) * inv_n
    c = o - mean
    var = _seg_sum(c * c, bsel) * inv_n
    y = c * lax.rsqrt(var + GN_EPS) * ln_w + ln_b
    return ((y + bonus) * g).astype(BF16)


def _rwkv_post_kernel(o_ref, bonus_ref, g_ref, lnw_ref, lnb_ref, bsel_ref, y_ref):
    y_ref[...] = _rwkv_out(o_ref[...].T, bonus_ref[...], g_ref[...], lnw_ref[...], lnb_ref[...], bsel_ref[...])


def _rwkv_post(o, bonus, g, lw, tm):
    M = bonus.shape[0]
    tile = pl.BlockSpec((tm, A_WIDTH), lambda i: (i, 0))
    o_tile = pl.BlockSpec((A_WIDTH, tm), lambda i: (0, i))
    l = lw["l"]
    row = pl.BlockSpec((None, 1, A_WIDTH), lambda i: (l, 0, 0))
    return pl.pallas_call(
        _rwkv_post_kernel,
        grid=(M // tm,),
        in_specs=[o_tile, tile, tile, row, row, pl.BlockSpec((SEG, SEG), lambda i: (0, 0))],
        out_specs=tile,
        out_shape=jax.ShapeDtypeStruct((M, A_WIDTH), BF16),
        compiler_params=_cparams(1),
        name="rwkv_post",
    )(o, bonus, g, lw["ln_w"], lw["ln_b"], lw["bsel"])


def _rope_kernel(x_ref, c_ref, s1_ref, s2_ref, o_ref):
    x = x_ref[...]
    fwd = pltpu.roll(x, x.shape[1] - ROPE_HALF, axis=1)
    bwd = pltpu.roll(x, ROPE_HALF, axis=1)
    o_ref[...] = x * c_ref[...] + fwd * s1_ref[...] + bwd * s2_ref[...]


def _rope(proj, tables, tm):
    M = proj.shape[0]
    n_tab = tables[0].shape[0] // tm
    tab = pl.BlockSpec((tm, SEG), lambda i, j: (i % n_tab, 0))
    return pl.pallas_call(
        _rope_kernel,
        grid=(M // tm, QK_W // SEG),
        in_specs=[pl.BlockSpec((tm, SEG), lambda i, j: (i, j)), tab, tab, tab],
        out_specs=pl.BlockSpec((tm, SEG), lambda i, j: (i, j)),
        out_shape=jax.ShapeDtypeStruct((M, QK_W), F32),
        compiler_params=_cparams(2),
        name="rope",
    )(proj, *tables)


def _swa_prompt_kernel(q_ref, kc_ref, kp_ref, vc_ref, vp_ref, sink_ref, o_ref):
    n = pl.program_id(1)
    blocks = q_ref.shape[0] // WINDOW
    rows = GROUP * WINDOW
    i = lax.broadcasted_iota(jnp.int32, (rows, 2 * WINDOW), 0) % WINDOW
    j = lax.broadcasted_iota(jnp.int32, (rows, 2 * WINDOW), 1)
    diff = WINDOW + i - j
    band = (diff >= 0) & (diff < WINDOW)
    q = q_ref[...].astype(BF16)
    kall = jnp.concatenate([kp_ref[...], kc_ref[...]], axis=0).astype(BF16)
    vall = jnp.concatenate([vp_ref[...], vc_ref[...]], axis=0).astype(BF16)
    units = [(qb, kh) for qb in range(blocks) for kh in range(KV_HEADS)]
    ksl = [slice(kh * HEAD_DIM, (kh + 1) * HEAD_DIM) for kh in range(KV_HEADS)]
    mask = [band & ((n > 0) | (j >= WINDOW)) if qb == 0 else band for qb, _ in units]
    kk = [kall[qb * WINDOW:(qb + 2) * WINDOW, ksl[kh]] for qb, kh in units]
    vv = [vall[qb * WINDOW:(qb + 2) * WINDOW, ksl[kh]] for qb, kh in units]
    qh = [jnp.concatenate([q[qb * WINDOW:(qb + 1) * WINDOW, (kh * GROUP + g) * HEAD_DIM:(kh * GROUP + g + 1) * HEAD_DIM]
                           for g in range(GROUP)], axis=0) for qb, kh in units]
    us = range(len(units))
    sink = [sink_ref[kh] for _, kh in units]
    s = [lax.dot_general(qh[u], kk[u], (((1,), (1,)), ((), ())), preferred_element_type=F32) for u in us]
    s = [jnp.where(mask[u], s[u] * (HEAD_DIM ** -0.5), -jnp.inf) for u in us]
    m = [jnp.maximum(jnp.max(s[u], axis=1, keepdims=True), sink[u]) for u in us]
    e = [jnp.exp(s[u] - m[u]) for u in us]
    denom = [jnp.sum(e[u], axis=1, keepdims=True) + jnp.exp(sink[u] - m[u]) for u in us]
    o = [_bdot((e[u] / denom[u]).astype(BF16), vv[u]) for u in us]
    for qb in range(blocks):
        outs = [o[qb * KV_HEADS + kh][g * WINDOW:(g + 1) * WINDOW, :] for kh in range(KV_HEADS) for g in range(GROUP)]
        o_ref[qb * WINDOW:(qb + 1) * WINDOW, :] = jnp.concatenate(outs, axis=1).astype(o_ref.dtype)


def _swa_prompt(qk_rot, proj, lw, B, T):
    nb = T // WINDOW
    rows = GROUP * WINDOW
    kcol = Q_W // KV_W
    vcol = QK_W // KV_W
    l = lw["l"]
    blocks = min(SWA_BLOCKS, nb)
    ng = nb // blocks
    span = blocks * WINDOW
    cur = lambda col: (lambda b, n: (b * ng + n, col))
    prev = lambda col: (lambda b, n: (b * nb + jnp.maximum(n * blocks - 1, 0), col))
    return pl.pallas_call(
        _swa_prompt_kernel,
        grid=(B, ng),
        in_specs=[pl.BlockSpec((span, Q_W), cur(0)),
                  pl.BlockSpec((span, KV_W), cur(kcol)),
                  pl.BlockSpec((WINDOW, KV_W), prev(kcol)),
                  pl.BlockSpec((span, KV_W), cur(vcol)),
                  pl.BlockSpec((WINDOW, KV_W), prev(vcol)),
                  pl.BlockSpec((None, KV_HEADS, rows, 1), lambda b, n: (l, 0, 0, 0))],
        out_specs=pl.BlockSpec((span, Q_W), cur(0)),
        out_shape=jax.ShapeDtypeStruct((B * T, Q_W), BF16),
        compiler_params=_cparams(2),
        name="swa_prompt",
    )(qk_rot, qk_rot, qk_rot, proj, proj, lw["sink_p"])


def _swa_sample_kernel(q_ref, kn_ref, vn_ref, ck_ref, cv_ref, sink_ref, *rest, steps, n_prev):
    if n_prev:
        pk_ref, pv_ref, o_ref, nk_ref, nv_ref = rest
        for li in range(n_prev):
            nk_ref[li] = pk_ref[li]
            nv_ref[li] = pv_ref[li]
    else:
        o_ref, nk_ref, nv_ref = rest
    first_new = SUBLANE - steps
    new_rows = lax.broadcasted_iota(jnp.int32, (SUBLANE, KV_W), 0) >= first_new
    for b in range(q_ref.shape[0]):
        for c_ref, n_ref, dst in ((ck_ref, kn_ref, nk_ref), (cv_ref, vn_ref, nv_ref)):
            rolled = pltpu.roll(c_ref[b], WINDOW - steps, axis=0)
            tail = jnp.where(new_rows, n_ref[b], rolled[WINDOW - SUBLANE:, :])
            dst[n_prev, b] = jnp.concatenate([rolled[:WINDOW - SUBLANE, :], tail], axis=0)
    rows = GROUP * steps
    t_row = lax.broadcasted_iota(jnp.int32, (2 * rows, WINDOW), 0) % steps
    c_col = lax.broadcasted_iota(jnp.int32, (2 * rows, WINDOW), 1)
    cache_mask = c_col > t_row
    t_col = lax.broadcasted_iota(jnp.int32, (2 * rows, 1), 0) % steps
    scale = HEAD_DIM ** -0.5
    zq = jnp.zeros((rows, HEAD_DIM), F32)
    units = [(b, p) for b in range(q_ref.shape[0]) for p in range(KV_HEADS // 2)]
    us = range(len(units))
    lanes = [slice(p * LANE, (p + 1) * LANE) for _, p in units]
    q2 = [jnp.concatenate([jnp.concatenate([q_ref[b, 2 * p], zq], axis=1),
                           jnp.concatenate([zq, q_ref[b, 2 * p + 1]], axis=1)], axis=0).astype(BF16) for b, p in units]
    kc = [ck_ref[b, :, lanes[u]].astype(BF16) for u, (b, p) in enumerate(units)]
    vc = [cv_ref[b, :, lanes[u]].astype(BF16) for u, (b, p) in enumerate(units)]
    kn = [kn_ref[b, first_new:, lanes[u]].astype(BF16).astype(F32) for u, (b, p) in enumerate(units)]
    vn = [vn_ref[b, first_new:, lanes[u]].astype(BF16).astype(F32) for u, (b, p) in enumerate(units)]
    sink = [jnp.concatenate([sink_ref[2 * p], sink_ref[2 * p + 1]], axis=0) for _, p in units]
    sc = [lax.dot_general(q2[u], kc[u], (((1,), (1,)), ((), ())), preferred_element_type=F32) for u in us]
    sc = [jnp.where(cache_mask, sc[u] * scale, -jnp.inf) for u in us]
    q2f = [q2[u].astype(F32) for u in us]
    sn = [[jnp.where(t_col >= t, jnp.sum(q2f[u] * kn[u][t:t + 1, :], axis=1, keepdims=True) * scale, -jnp.inf)
           for t in range(steps)] for u in us]
    m = [jnp.maximum(jnp.max(sc[u], axis=1, keepdims=True), sink[u]) for u in us]
    for t in range(steps):
        m = [jnp.maximum(m[u], sn[u][t]) for u in us]
    ec = [jnp.exp(sc[u] - m[u]) for u in us]
    en = [[jnp.exp(sn[u][t] - m[u]) for t in range(steps)] for u in us]
    denom = [jnp.sum(ec[u], axis=1, keepdims=True) + jnp.exp(sink[u] - m[u]) for u in us]
    for t in range(steps):
        denom = [denom[u] + en[u][t] for u in us]
    o = [_bdot((ec[u] / denom[u]).astype(BF16), vc[u]) for u in us]
    for t in range(steps):
        o = [o[u] + (en[u][t] / denom[u]).astype(BF16).astype(F32) * vn[u][t:t + 1, :] for u in us]
    for u, (b, p) in enumerate(units):
        o_ref[b, 2 * p] = o[u][0:rows, 0:HEAD_DIM]
        o_ref[b, 2 * p + 1] = o[u][rows:2 * rows, HEAD_DIM:LANE]


def _swa_sample(qk_rot, proj, cache_k, cache_v, prev_k, prev_v, lw, B, T):
    l = lw["l"]
    rows = GROUP * T
    qg = qk_rot[:, :Q_W].reshape(T, B, KV_HEADS, GROUP, HEAD_DIM).transpose(1, 2, 3, 0, 4)
    qg = qg.reshape(B, KV_HEADS, rows, HEAD_DIM)
    pad = ((0, 0), (SUBLANE - T, 0), (0, 0))
    kn = jnp.pad(qk_rot[:, Q_W:].reshape(T, B, KV_W).transpose(1, 0, 2), pad)
    vn = jnp.pad(proj[:, QK_W:].reshape(T, B, KV_W).transpose(1, 0, 2), pad)
    nb = SUBLANE if B % SUBLANE == 0 else 1
    new_spec = pl.BlockSpec((nb, SUBLANE, KV_W), lambda b: (b, 0, 0))
    cache_spec = pl.BlockSpec((None, nb, WINDOW, KV_W), lambda b: (l, b, 0, 0))
    in_specs = [pl.BlockSpec((nb, KV_HEADS, rows, HEAD_DIM), lambda b: (b, 0, 0, 0)),
                new_spec, new_spec, cache_spec, cache_spec,
                pl.BlockSpec((None, KV_HEADS, rows, 1), lambda b: (l, 0, 0, 0))]
    args = [qg, kn, vn, cache_k, cache_v, lw["sink_s"]]
    if l:
        in_specs += [pl.BlockSpec((l, nb, WINDOW, KV_W), lambda b: (0, b, 0, 0))] * 2
        args += [prev_k, prev_v]
    stacked = pl.BlockSpec((l + 1, nb, WINDOW, KV_W), lambda b: (0, b, 0, 0))
    stacked_shape = jax.ShapeDtypeStruct((l + 1, B, WINDOW, KV_W), F32)
    o, new_k, new_v = pl.pallas_call(
        functools.partial(_swa_sample_kernel, steps=T, n_prev=l),
        grid=(B // nb,),
        in_specs=in_specs,
        out_specs=[pl.BlockSpec((nb, KV_HEADS, rows, HEAD_DIM), lambda b: (b, 0, 0, 0)), stacked, stacked],
        out_shape=[jax.ShapeDtypeStruct((B, KV_HEADS, rows, HEAD_DIM), F32), stacked_shape, stacked_shape],
        compiler_params=_cparams(1),
        name="swa_sample",
    )(*args)
    o = o.reshape(B, KV_HEADS, GROUP, T, HEAD_DIM).transpose(3, 0, 1, 2, 4).reshape(T * B, Q_W)
    return o.astype(BF16), new_k, new_v


def _rope_tables(pos):
    inv = 1.0 / (ROPE_THETA ** (jnp.arange(0, ROPE_DIM, 2, dtype=F32) / ROPE_DIM))
    ang = pos.astype(F32)[:, None] * inv[None, :]
    cos, sin = jnp.cos(ang), jnp.sin(ang)
    n = pos.shape[0]
    ones = jnp.ones((n, HEAD_DIM - ROPE_DIM), F32)
    zeros = jnp.zeros((n, HEAD_DIM - ROPE_DIM), F32)
    zh = jnp.zeros((n, ROPE_HALF), F32)
    c = jnp.concatenate([cos, cos, ones], axis=1)
    s1 = jnp.concatenate([-sin, zh, zeros], axis=1)
    s2 = jnp.concatenate([zh, sin, zeros], axis=1)
    return tuple(jnp.tile(a, (1, SEG // HEAD_DIM)) for a in (c, s1, s2))


def _stacked_params(t_sample, norm_g, ffn_w_gate, ffn_w_up, ffn_w_down, w_in, rwkv_mu, rwkv_w0, rwkv_w2,
                    rwkv_a0, rwkv_a2, rwkv_g2, rwkv_k_k, rwkv_k_a, rwkv_r_k, rwkv_ln_w, rwkv_ln_b, attn_sinks,
                    w_branch_a, w_branch_b, w_merge_gate, w_out):
    L = w_in.shape[0]
    lora_rows = lambda w, off: jnp.pad(w, ((0, 0), (off, LORA_PAD - off - w.shape[1]), (0, 0))).astype(BF16)
    seg = jnp.arange(SEG) // A_HEAD_DIM
    row = lambda a: a.reshape(L, 1, -1)
    sink_rows = lambda n: jnp.repeat(attn_sinks.reshape(L, KV_HEADS, GROUP), n, axis=2).reshape(L, KV_HEADS, GROUP * n, 1)
    return {
        "norm_g": norm_g, "wg": ffn_w_gate, "wu": ffn_w_up, "wd": ffn_w_down,
        "w_in": w_in, "w_qkv": w_in[:, :, A_PROJ:],
        "mu": row(jnp.pad(rwkv_mu, ((0, 0), (0, A_PAD - A_PROJ)))),
        "w0": row(rwkv_w0), "a0": row(rwkv_a0), "k_k": row(rwkv_k_k), "k_a": row(rwkv_k_a),
        "r_k": row(rwkv_r_k), "ln_w": row(rwkv_ln_w), "ln_b": row(rwkv_ln_b),
        "w2": lora_rows(rwkv_w2, 0), "a2": lora_rows(rwkv_a2, DECAY_LORA),
        "g2": lora_rows(rwkv_g2, DECAY_LORA + ICLR_LORA),
        "bsel": (seg[:, None] == seg[None, :]).astype(BF16),
        "sink_p": sink_rows(WINDOW), "sink_s": sink_rows(t_sample),
        "w_a": w_branch_a, "w_b": w_branch_b, "w_gate": w_merge_gate, "w_out": w_out,
    }


def _decoder_layer(x, mod, lw, B, T, shift_prev, tables, sample=None):
    M = B * T
    l = lw["l"]
    sp = jnp.pad(shift_prev, ((0, 0), (0, A_PAD - A_PROJ)))
    if sample is not None:
        tm = M
        mod = mod.reshape(1, B, N_MOD * D_MODEL)
        gmap = lambda tile: (lambda i: 0)
        sp = sp.reshape(1, B, A_PAD)
    else:
        tm = min(T, 1024)
        mod = mod.reshape(B, 1, N_MOD * D_MODEL)
        gmap = lambda tile: (lambda i: i // (T // tile))
        sp = sp.reshape(B, 1, A_PAD)
    tm_k = tm

    def ffn(x, n, which):
        h = _ada_norm(x, lw["norm_g"], l, n, mod, gmap(tm), tm)
        act = _ffn_gate_up(h, lw["wg"], lw["wu"], (l, which), tm)
        return _mm_residual(act, lw["wd"], (l, which), x, mod, 3 * n + 2, gmap(tm_k), 0.5, tm_k,
                            single_buffer_w=True)

    x = ffn(x, 0, 0)

    h = _ada_norm(x, lw["norm_g"], l, 1, mod, gmap(tm), tm)
    proj_a = _mm_plain(h, lw["w_in"], (l,), tm, tn=TN_A, n_cols=A_PAD, name="in_proj_a")
    proj_b = _mm_plain(h, lw["w_qkv"], (l,), tm, tn=TN_B, name="in_proj_b")
    gates = _mm_plain(h, lw["w_gate"], (l,), tm, tn=TN_WIDE, sigmoid=True, name="merge_gates")

    qk_rot = _rope(proj_b, tables, min(M, ROPE_ROWS))
    if sample is not None:
        state_wkv, cache_k, cache_v, prev_wkv, prev_k, prev_v = sample
        r, w, k2, v, kk, kka, g, bonus = _rwkv_prep(proj_a, sp, lw, B)
        o, new_wkv = _rwkv_lane_scan((r, w, k2, v, kk, kka), state_wkv, prev_wkv, l, B, T)
        ya = _rwkv_post(o, bonus, g, lw, B)
        new_shift = proj_a[(T - 1) * B:, :A_PROJ]
        yb, new_k, new_v = _swa_sample(qk_rot, proj_b, cache_k, cache_v, prev_k, prev_v, lw, B, T)
    else:
        zero_wkv = jnp.zeros((B, A_HEADS, A_HEAD_DIM, A_HEAD_DIM), F32)
        ya, new_wkv = _rwkv_chunked(proj_a, sp, lw, zero_wkv, B, T)
        new_shift = proj_a.reshape(B, T, A_PAD)[:, -1, :A_PROJ]
        yb = _swa_prompt(qk_rot, proj_b, lw, B, T)
        new_k = qk_rot[:, Q_W:].reshape(B, T, KV_HEADS, HEAD_DIM)[:, -WINDOW:]
        new_v = proj_b[:, QK_W:].reshape(B, T, KV_HEADS, HEAD_DIM)[:, -WINDOW:]

    merged = _mm_merge(ya, yb, lw["w_a"], lw["w_b"], (l,), gates, tm, tn=TN_WIDE)
    x = _mm_residual(merged, lw["w_out"], (l,), x, mod, 5, gmap(tm), 1.0, tm, tn=TN_WIDE)

    x = ffn(x, 2, 1)
    return x, new_shift, new_wkv, new_k, new_v


def kernel(x_prompt, x_sample, c_prompt, c_sample, state_shift, state_wkv, cache_k, cache_v, norm_g, w_mod, b_mod,
           ffn_w_gate, ffn_w_up, ffn_w_down, w_in, rwkv_mu, rwkv_w0, rwkv_w2, rwkv_a0, rwkv_a2, rwkv_g2, rwkv_k_k,
           rwkv_k_a, rwkv_r_k, rwkv_ln_w, rwkv_ln_b, attn_sinks, w_branch_a, w_branch_b, w_merge_gate, w_out,
           final_norm):
    Bp, Tp, _ = x_prompt.shape
    Bs, Ts, _ = x_sample.shape
    L = w_in.shape[0]
    tab_p = _rope_tables(jnp.arange(Tp))
    tab_s = _rope_tables(jnp.repeat(PAST_LEN + jnp.arange(Ts), Bs))
    zero_shift = jnp.zeros((Bp, A_PROJ), F32)
    xp = x_prompt.reshape(Bp * Tp, D_MODEL)
    xs = x_sample.transpose(1, 0, 2).reshape(Ts * Bs, D_MODEL)
    c_all = jnp.concatenate([c_prompt, c_sample], axis=0)
    pad_rows = (-c_all.shape[0]) % 16
    c_all = jnp.pad(c_all, ((0, pad_rows), (0, 0)))
    pw = _stacked_params(Ts, norm_g, ffn_w_gate, ffn_w_up, ffn_w_down, w_in, rwkv_mu, rwkv_w0, rwkv_w2, rwkv_a0,
                         rwkv_a2, rwkv_g2, rwkv_k_k, rwkv_k_a, rwkv_r_k, rwkv_ln_w, rwkv_ln_b, attn_sinks,
                         w_branch_a, w_branch_b, w_merge_gate, w_out)
    state_s = state_wkv.reshape(L, Bs, A_HEADS * A_HEAD_DIM * A_HEAD_DIM)
    ck = cache_k.reshape(L, Bs, WINDOW, KV_W)
    cv = cache_v.reshape(L, Bs, WINDOW, KV_W)
    outs_p, shifts_s = [], []
    wkv_s = k_s = v_s = None
    for l in range(L):
        lw = dict(pw, l=l)
        mod = _modulation(c_all, w_mod, b_mod, l)
        xp, *st_p = _decoder_layer(xp, mod[:Bp], lw, Bp, Tp, zero_shift, tab_p)
        xs, shift_s, wkv_s, k_s, v_s = _decoder_layer(xs, mod[Bp:Bp + Bs], lw, Bs, Ts, state_shift[l], tab_s,
                                                      sample=(state_s, ck, cv, wkv_s, k_s, v_s))
        outs_p.append(st_p)
        shifts_s.append(shift_s)
    y_prompt = _final_norm(xp, final_norm, 1024).reshape(Bp, Tp, D_MODEL)
    y_sample = _final_norm(xs, final_norm, Bs * Ts).reshape(Ts, Bs, D_MODEL).transpose(1, 0, 2)
    stack = lambda n: jnp.stack([o[n] for o in outs_p])
    return (y_prompt, y_sample, stack(0), stack(1), stack(2), stack(3),
            jnp.stack(shifts_s), wkv_s.reshape(state_wkv.shape), k_s.reshape(cache_k.shape),
            v_s.reshape(cache_v.shape))
```
